```python
import jax, jax.numpy as jnp
from jax import lax
import numpy as np

D_MODEL = 1024
BATCH = 1
SEQ = 16384
DEPTH = 4
DEC_BATCH = 16
DEC_SEQ = 4096
PAST_LEN = 128

N_MIXERS = 2
EPS = 1e-6
N_MEM = 256
MEM_HEADS = 4
MEM_HDIM = 64
MEM_WIDTH = MEM_HEADS * MEM_HDIM
MLA_HEADS = 12
QK_NOPE = 64
QK_ROPE = 32
QK_HEAD = QK_NOPE + QK_ROPE
V_HEAD = 64
Q_LORA = 384
KV_LORA = 256
MLA_WIDTH = MLA_HEADS * V_HEAD
ROPE_BASE = 10000.0
Q_BLOCK = 128
CHUNK = 128
SG_GROUPS = 8
SG_WIDTH = 768
SG_GDIM = SG_WIDTH // SG_GROUPS
BRANCH = MLA_WIDTH + MEM_WIDTH
IN_A = Q_LORA + KV_LORA + QK_ROPE + MEM_WIDTH + BRANCH
IN_B = 2 * SG_WIDTH + MEM_WIDTH + BRANCH
N_A = (DEPTH + 1) // 2
N_B = DEPTH // 2

kernel_name = 'hybrid_mla_sgu_memory_encoder'


def rmsnorm(x, g):
    xf = x.astype(jnp.float32)
    y = xf * lax.rsqrt(jnp.mean(xf * xf, axis=-1, keepdims=True) + EPS)
    return (y * g.astype(jnp.float32)).astype(x.dtype)


def rope_tables(seq):
    inv = 1.0 / (ROPE_BASE ** (jnp.arange(0, QK_ROPE, 2, dtype=jnp.float32) / QK_ROPE))
    ang = jnp.arange(seq, dtype=jnp.float32)[:, None] * inv[None, :]
    return jnp.cos(ang), jnp.sin(ang)


def apply_rope(x, cos, sin):
    x1, x2 = jnp.split(x, 2, axis=-1)
    c = cos[None, :, None, :]
    s = sin[None, :, None, :]
    return jnp.concatenate([x1 * c - x2 * s, x1 * s + x2 * c], axis=-1).astype(x.dtype)


def block_attention(q, k, v, scale):
    B, S, H, Dk = q.shape
    Dv = v.shape[-1]
    nb = S // Q_BLOCK
    qb = q.reshape(B, nb, Q_BLOCK, H, Dk).transpose(1, 0, 2, 3, 4)

    def one_block(qblk):
        s = jnp.einsum('bqhd,bkhd->bhqk', qblk, k, preferred_element_type=jnp.float32) * scale
        p = jax.nn.softmax(s, axis=-1)
        return jnp.einsum('bhqk,bkhd->bqhd', p.astype(v.dtype), v)

    o = lax.map(one_block, qb)
    return o.transpose(1, 0, 2, 3, 4).reshape(B, S, H * Dv)


def mla_mixer(q_lat, kv_lat, k_pe, cos, sin, qlat_g, kvlat_g, w_uq, w_ukv, q_norm_g, k_norm_g):
    B, S, _ = q_lat.shape
    q = (rmsnorm(q_lat, qlat_g) @ w_uq).reshape(B, S, MLA_HEADS, QK_HEAD)
    q = rmsnorm(q, q_norm_g)
    q = jnp.concatenate([q[..., :QK_NOPE], apply_rope(q[..., QK_NOPE:], cos, sin)], axis=-1)
    kv = (rmsnorm(kv_lat, kvlat_g) @ w_ukv).reshape(B, S, MLA_HEADS, QK_NOPE + V_HEAD)
    k_nope, v = kv[..., :QK_NOPE], kv[..., QK_NOPE:]
    k_pe_h = jnp.broadcast_to(k_pe[:, :, None, :], (B, S, MLA_HEADS, QK_ROPE))
    k = rmsnorm(jnp.concatenate([k_nope, k_pe_h], axis=-1), k_norm_g)
    k = jnp.concatenate([k[..., :QK_NOPE], apply_rope(k[..., QK_NOPE:], cos, sin)], axis=-1)
    return block_attention(q, k, v, QK_HEAD ** -0.5)


def sgu_mixer(uv, v_norm_g, w_s, b_s):
    B, S, _ = uv.shape
    uv = jax.nn.gelu(uv)
    u, v = uv[..., :SG_WIDTH], uv[..., SG_WIDTH:]
    v = rmsnorm(v, v_norm_g)
    vc = v.reshape(B, S // CHUNK, CHUNK, SG_GROUPS, SG_GDIM)
    mixed = jnp.einsum('gts,bcsgd->bctgd', w_s, vc) + b_s.T[None, None, :, :, None]
    return u * mixed.reshape(B, S, SG_WIDTH)


def memory_attention(q_mem, mem, mem_g, w_mem_kv, mq_g, mk_g):
    B, S, _ = q_mem.shape
    q = rmsnorm(q_mem.reshape(B, S, MEM_HEADS, MEM_HDIM), mq_g)
    kv = rmsnorm(mem, mem_g) @ w_mem_kv
    k = rmsnorm(kv[..., :MEM_WIDTH].reshape(B, N_MEM, MEM_HEADS, MEM_HDIM), mk_g)
    v = kv[..., MEM_WIDTH:].reshape(B, N_MEM, MEM_HEADS, MEM_HDIM)
    s = jnp.einsum('bqhd,bkhd->bhqk', q, k, preferred_element_type=jnp.float32) * (MEM_HDIM ** -0.5)
    p = jax.nn.softmax(s, axis=-1)
    o = jnp.einsum('bhqk,bkhd->bqhd', p.astype(v.dtype), v)
    return o.reshape(B, S, MEM_WIDTH)


def encoder_trunk(x, mem, ln_g, w_in_a, a_qlat_g, a_kvlat_g, a_w_uq, a_w_ukv, a_q_norm_g, a_k_norm_g,
                  w_in_b, b_v_norm_g, b_w_s, b_bias, mem_norm_g, w_mem_kv, mem_q_norm_g, mem_k_norm_g, w_out):
    S = x.shape[1]
    cos, sin = rope_tables(S)
    for i in range(DEPTH):
        h = rmsnorm(x, ln_g[i])
        j = i // N_MIXERS
        if i % N_MIXERS == 0:
            p = h @ w_in_a[j]
            q_lat, kv_lat, k_pe, q_mem, gate = jnp.split(
                p, [Q_LORA, Q_LORA + KV_LORA, Q_LORA + KV_LORA + QK_ROPE,
                    Q_LORA + KV_LORA + QK_ROPE + MEM_WIDTH], axis=-1)
            o_mix = mla_mixer(q_lat, kv_lat, k_pe, cos, sin, a_qlat_g[j], a_kvlat_g[j], a_w_uq[j],
                              a_w_ukv[j], a_q_norm_g[j], a_k_norm_g[j])
        else:
            p = h @ w_in_b[j]
            uv, q_mem, gate = jnp.split(p, [2 * SG_WIDTH, 2 * SG_WIDTH + MEM_WIDTH], axis=-1)
            o_mix = sgu_mixer(uv, b_v_norm_g[j], b_w_s[j], b_bias[j])
        o_mem = memory_attention(q_mem, mem, mem_norm_g[i], w_mem_kv[i], mem_q_norm_g[i], mem_k_norm_g[i])
        o = jnp.concatenate([o_mix, o_mem], axis=-1) * jax.nn.silu(gate)
        x = x + o @ w_out[i]
    return x


def setup_inputs(seed: int = 0) -> dict:
    key = jax.random.key(seed)
    ks = jax.random.split(key, 22)
    f = jnp.float32

    def nrm(k, shape, scale):
        return jax.random.normal(k, shape, f) * scale

    def gain(k, shape):
        return 1.0 + 0.02 * jax.random.normal(k, shape, f)

    return {
        'x_prompt': nrm(ks[0], (BATCH, SEQ, D_MODEL), 1.0),
        'x_sample': nrm(ks[1], (DEC_BATCH, DEC_SEQ, D_MODEL), 1.0),
        'mem_prompt': nrm(ks[2], (BATCH, N_MEM, D_MODEL), 1.0),
        'mem_sample': nrm(ks[3], (DEC_BATCH, N_MEM, D_MODEL), 1.0),
        'ln_g': gain(ks[4], (DEPTH, D_MODEL)),
        'w_in_a': nrm(ks[5], (N_A, D_MODEL, IN_A), D_MODEL ** -0.5),
        'a_qlat_g': gain(ks[6], (N_A, Q_LORA)),
        'a_kvlat_g': gain(ks[7], (N_A, KV_LORA)),
        'a_w_uq': nrm(ks[8], (N_A, Q_LORA, MLA_HEADS * QK_HEAD), Q_LORA ** -0.5),
        'a_w_ukv': nrm(ks[9], (N_A, KV_LORA, MLA_HEADS * (QK_NOPE + V_HEAD)), KV_LORA ** -0.5),
        'a_q_norm_g': gain(ks[10], (N_A, QK_HEAD)),
        'a_k_norm_g': gain(ks[11], (N_A, QK_HEAD)),
        'w_in_b': nrm(ks[12], (N_B, D_MODEL, IN_B), D_MODEL ** -0.5),
        'b_v_norm_g': gain(ks[13], (N_B, SG_WIDTH)),
        'b_w_s': nrm(ks[14], (N_B, SG_GROUPS, CHUNK, CHUNK), CHUNK ** -0.5),
        'b_bias': 1.0 + 0.1 * jax.random.normal(ks[15], (N_B, SG_GROUPS, CHUNK), f),
        'mem_norm_g': gain(ks[16], (DEPTH, D_MODEL)),
        'w_mem_kv': nrm(ks[17], (DEPTH, D_MODEL, 2 * MEM_WIDTH), D_MODEL ** -0.5),
        'mem_q_norm_g': gain(ks[18], (DEPTH, MEM_HDIM)),
        'mem_k_norm_g': gain(ks[19], (DEPTH, MEM_HDIM)),
        'w_out': nrm(ks[20], (DEPTH, BRANCH, D_MODEL), BRANCH ** -0.5),
    }


def reference(x_prompt, x_sample, mem_prompt, mem_sample, ln_g, w_in_a, a_qlat_g, a_kvlat_g, a_w_uq,
              a_w_ukv, a_q_norm_g, a_k_norm_g, w_in_b, b_v_norm_g, b_w_s, b_bias, mem_norm_g, w_mem_kv,
              mem_q_norm_g, mem_k_norm_g, w_out):
    y_prompt = encoder_trunk(x_prompt, mem_prompt, ln_g, w_in_a, a_qlat_g, a_kvlat_g, a_w_uq, a_w_ukv,
                             a_q_norm_g, a_k_norm_g, w_in_b, b_v_norm_g, b_w_s, b_bias, mem_norm_g,
                             w_mem_kv, mem_q_norm_g, mem_k_norm_g, w_out)
    y_sample = encoder_trunk(x_sample, mem_sample, ln_g, w_in_a, a_qlat_g, a_kvlat_g, a_w_uq, a_w_ukv,
                             a_q_norm_g, a_k_norm_g, w_in_b, b_v_norm_g, b_w_s, b_bias, mem_norm_g,
                             w_mem_kv, mem_q_norm_g, mem_k_norm_g, w_out)
    return (y_prompt, y_sample)
```

```python
import functools
import math

import jax
import jax.numpy as jnp
from jax import lax
from jax.experimental import pallas as pl
from jax.experimental.pallas import tpu as pltpu

D_MODEL = 1024
DEPTH = 4
EPS = 1e-6
N_MEM = 256
MEM_HEADS = 4
MEM_HDIM = 64
MEM_WIDTH = MEM_HEADS * MEM_HDIM
MLA_HEADS = 12
QK_NOPE = 64
QK_ROPE = 32
QK_HEAD = QK_NOPE + QK_ROPE
V_HEAD = 64
Q_LORA = 384
KV_LORA = 256
MLA_WIDTH = MLA_HEADS * V_HEAD
ROPE_BASE = 10000.0
CHUNK = 128
SG_GROUPS = 8
SG_WIDTH = 768
SG_GDIM = SG_WIDTH // SG_GROUPS

LANE = 128
SG_PAD = SG_GROUPS * LANE
HALF_ROPE = QK_ROPE // 2

ROW_TILE = 256
ATTN_TQ = 256
ATTN_TK = 512
VMEM_LIMIT = 56 * 1024 * 1024

BF16 = jnp.bfloat16
F32 = jnp.float32


def _rsqrt_mean(x, n):
    return lax.rsqrt(jnp.sum(x * x, axis=-1, keepdims=True) * (1.0 / n) + EPS)


def _silu(g):
    return g / (1.0 + jnp.exp(-g))


def _gelu_tanh(x):
    c = math.sqrt(2.0 / math.pi)
    return 0.5 * x * (1.0 + jnp.tanh(c * (x + 0.044715 * (x * x * x))))


def _dot(a, b):
    return jnp.dot(a, b, preferred_element_type=F32)


def _dot_nt(a, b):
    return lax.dot_general(a, b, (((1,), (1,)), ((), ())), preferred_element_type=F32)


def _mem_kv_kernel(mem_ref, g_ref, wk_ref, wv_ref, kg_ref, k_out, v_out):
    m = mem_ref[0]
    mn = (m * _rsqrt_mean(m, D_MODEL) * g_ref[0]).astype(BF16)
    kk = _dot(mn, wk_ref[0])
    vv = _dot(mn, wv_ref[0])
    lane = lax.broadcasted_iota(jnp.int32, (N_MEM, MEM_WIDTH), 1)
    for h in range(MEM_HEADS):
        kh = kk[:, h * LANE:(h + 1) * LANE]
        k_out[0, 0, h] = (kh * _rsqrt_mean(kh, MEM_HDIM) * kg_ref[0]).astype(BF16)
        sel = (lane >= h * MEM_HDIM) & (lane < (h + 1) * MEM_HDIM)
        v_out[0, 0, h] = jnp.where(sel, vv, 0.0).astype(BF16)


def _mem_kv(mem, mem_g, wk, wv, kg):
    B = mem.shape[0]
    return pl.pallas_call(
        _mem_kv_kernel,
        grid=(DEPTH, B),
        in_specs=[
            pl.BlockSpec((1, N_MEM, D_MODEL), lambda l, b: (b, 0, 0)),
            pl.BlockSpec((1, 1, D_MODEL), lambda l, b: (l, 0, 0)),
            pl.BlockSpec((1, D_MODEL, MEM_HEADS * LANE), lambda l, b: (l, 0, 0)),
            pl.BlockSpec((1, D_MODEL, MEM_WIDTH), lambda l, b: (l, 0, 0)),
            pl.BlockSpec((1, 1, LANE), lambda l, b: (l, 0, 0)),
        ],
        out_specs=[
            pl.BlockSpec((1, 1, MEM_HEADS, N_MEM, LANE), lambda l, b: (l, b, 0, 0, 0)),
            pl.BlockSpec((1, 1, MEM_HEADS, N_MEM, MEM_WIDTH), lambda l, b: (l, b, 0, 0, 0)),
        ],
        out_shape=[
            jax.ShapeDtypeStruct((DEPTH, B, MEM_HEADS, N_MEM, LANE), BF16),
            jax.ShapeDtypeStruct((DEPTH, B, MEM_HEADS, N_MEM, MEM_WIDTH), BF16),
        ],
        compiler_params=pltpu.CompilerParams(
            dimension_semantics=("arbitrary", "arbitrary"), vmem_limit_bytes=VMEM_LIMIT),
        name="mem_kv",
    )(mem, mem_g, wk, wv, kg)


def _memory_attention(qm, qg, kmem_ref, vmem_ref):
    o = None
    for h in range(MEM_HEADS):
        qh = qm[:, h * LANE:(h + 1) * LANE]
        qn = (qh * _rsqrt_mean(qh, MEM_HDIM) * qg).astype(BF16)
        s = _dot_nt(qn, kmem_ref[0, 0, h]) * (MEM_HDIM ** -0.5)
        e = jnp.exp(s - jnp.max(s, axis=-1, keepdims=True))
        p = (e / jnp.sum(e, axis=-1, keepdims=True)).astype(BF16)
        oh = _dot(p, vmem_ref[0, 0, h])
        o = oh if o is None else o + oh
    return o


def _pre_a_kernel(x_ref, ct_ref, st_ref, ln_ref, wlat_ref, wqm_ref, wgate_ref, qlg_ref, kvlg_ref,
                  wuq_ref, wuqs_ref, wk_ref, wv_ref, qg_ref, kg_ref, mqg_ref, kmem_ref, vmem_ref,
                  q_out, k_out, v_out, sg_out, om_out):
    x = x_ref[0]
    h = (x * _rsqrt_mean(x, D_MODEL) * ln_ref[0]).astype(BF16)

    lat = _dot(h, wlat_ref[0])
    q_lat = lat[:, :Q_LORA]
    kv_lat = lat[:, Q_LORA:Q_LORA + KV_LORA]
    kpe = lat[:, Q_LORA + KV_LORA:Q_LORA + KV_LORA + LANE]
    kpe_sw = lat[:, Q_LORA + KV_LORA + LANE:]

    ct = ct_ref[...]
    st = st_ref[...]

    qn = (q_lat * _rsqrt_mean(q_lat, Q_LORA) * qlg_ref[0]).astype(BF16)
    q_raw = _dot(qn, wuq_ref[0])
    q_sw = _dot(qn, wuqs_ref[0])
    q_c = ct * qg_ref[0, 0:1]
    q_s = st * qg_ref[0, 1:2]
    scale = QK_HEAD ** -0.5
    for hd in range(MLA_HEADS):
        qh = q_raw[:, hd * LANE:(hd + 1) * LANE]
        r = _rsqrt_mean(qh, QK_HEAD)
        qo = (qh * q_c + q_sw[:, hd * LANE:(hd + 1) * LANE] * q_s) * (r * scale)
        q_out[0, hd] = qo.astype(BF16)

    kvn = (kv_lat * _rsqrt_mean(kv_lat, KV_LORA) * kvlg_ref[0]).astype(BF16)
    k_nope = _dot(kvn, wk_ref[0])
    v_out[0] = _dot(kvn, wv_ref[0]).astype(BF16)
    k_cg = kg_ref[0, 0:1]
    kpe_rot = kpe * (ct * k_cg) + kpe_sw * (st * kg_ref[0, 1:2])
    pe_sq = jnp.sum(kpe * kpe, axis=-1, keepdims=True)
    for hd in range(MLA_HEADS):
        kh = k_nope[:, hd * LANE:(hd + 1) * LANE]
        r = lax.rsqrt((jnp.sum(kh * kh, axis=-1, keepdims=True) + pe_sq) * (1.0 / QK_HEAD) + EPS)
        k_out[0, hd] = ((kh * k_cg + kpe_rot) * r).astype(BF16)

    sg = _silu(_dot(h, wgate_ref[0]))
    sg_out[0] = sg[:, :MLA_WIDTH]
    o_mem = _memory_attention(_dot(h, wqm_ref[0]), mqg_ref[0], kmem_ref, vmem_ref)
    om_out[0] = (o_mem * sg[:, MLA_WIDTH:]).astype(BF16)


def _const_spec(shape, layer):
    nd = len(shape)
    return pl.BlockSpec((1,) + tuple(shape[1:]), lambda b, i: (layer,) + (0,) * (nd - 1))


def _pre_a(x, ct, st, w, j, layer, kmem, vmem):
    B, S, _ = x.shape
    tm = ROW_TILE
    weights = [w["ln_g"], w["a_wlat"], w["a_wqm"], w["a_wgate"], w["a_qlat_g"], w["a_kvlat_g"],
               w["a_wuq"], w["a_wuqs"], w["a_wk"], w["a_wv"], w["a_qg"], w["a_kg"], w["mq_g"]]
    wlayer = [layer, j, j, j, j, j, j, j, j, j, j, j, layer]
    in_specs = [
        pl.BlockSpec((1, tm, D_MODEL), lambda b, i: (b, i, 0)),
        pl.BlockSpec((tm, LANE), lambda b, i: (i, 0)),
        pl.BlockSpec((tm, LANE), lambda b, i: (i, 0)),
    ] + [_const_spec(a.shape, l) for a, l in zip(weights, wlayer)] + [
        pl.BlockSpec((1, 1, MEM_HEADS, N_MEM, LANE), lambda b, i: (layer, b, 0, 0, 0)),
        pl.BlockSpec((1, 1, MEM_HEADS, N_MEM, MEM_WIDTH), lambda b, i: (layer, b, 0, 0, 0)),
    ]
    out_specs = [
        pl.BlockSpec((1, MLA_HEADS, tm, LANE), lambda b, i: (b, 0, i, 0)),
        pl.BlockSpec((1, MLA_HEADS, tm, LANE), lambda b, i: (b, 0, i, 0)),
        pl.BlockSpec((1, tm, MLA_WIDTH), lambda b, i: (b, i, 0)),
        pl.BlockSpec((1, tm, MLA_WIDTH), lambda b, i: (b, i, 0)),
        pl.BlockSpec((1, tm, MEM_WIDTH), lambda b, i: (b, i, 0)),
    ]
    out_shape = [
        jax.ShapeDtypeStruct((B, MLA_HEADS, S, LANE), BF16),
        jax.ShapeDtypeStruct((B, MLA_HEADS, S, LANE), BF16),
        jax.ShapeDtypeStruct((B, S, MLA_WIDTH), BF16),
        jax.ShapeDtypeStruct((B, S, MLA_WIDTH), F32),
        jax.ShapeDtypeStruct((B, S, MEM_WIDTH), BF16),
    ]
    return pl.pallas_call(
        _pre_a_kernel,
        grid=(B, S // tm),
        in_specs=in_specs,
        out_specs=out_specs,
        out_shape=out_shape,
        compiler_params=pltpu.CompilerParams(
            dimension_semantics=("arbitrary", "arbitrary"), vmem_limit_bytes=VMEM_LIMIT),
        name="pre_a",
    )(x, ct, st, *weights, kmem, vmem)


def _attn_kernel(q_ref, k_ref, v_ref, o_ref, *, seq):
    tq, tk = ATTN_TQ, ATTN_TK
    outs = []
    for hh in range(2):
        q = q_ref[0, hh]

        def body(j, carry):
            m, l, acc = carry
            off = pl.multiple_of(j * tk, tk)
            k = k_ref[0, hh, pl.ds(off, tk), :]
            v = v_ref[0, pl.ds(off, tk), :]
            s = _dot_nt(q, k)
            m_new = jnp.maximum(m, jnp.max(s, axis=-1, keepdims=True))
            alpha = jnp.exp(m - m_new)
            p = jnp.exp(s - m_new)
            l = alpha * l + jnp.sum(p, axis=-1, keepdims=True)
            acc = alpha * acc + _dot(p.astype(BF16), v)
            return m_new, l, acc

        init = (jnp.full((tq, 1), -jnp.inf, F32), jnp.zeros((tq, 1), F32), jnp.zeros((tq, LANE), F32))
        m, l, acc = lax.fori_loop(0, seq // tk, body, init)
        outs.append(acc / l)
    lane = lax.broadcasted_iota(jnp.int32, (tq, LANE), 1)
    o_ref[0] = jnp.where(lane < V_HEAD, outs[0], outs[1])


def _attention(q, k, v):
    B, _, S, _ = q.shape
    tq = ATTN_TQ
    return pl.pallas_call(
        functools.partial(_attn_kernel, seq=S),
        grid=(B, MLA_HEADS // 2, S // tq),
        in_specs=[
            pl.BlockSpec((1, 2, tq, LANE), lambda b, hp, i: (b, hp, i, 0)),
            pl.BlockSpec((1, 2, S, LANE), lambda b, hp, i: (b, hp, 0, 0)),
            pl.BlockSpec((1, S, LANE), lambda b, hp, i: (b, 0, hp)),
        ],
        out_specs=pl.BlockSpec((1, tq, LANE), lambda b, hp, i: (b, i, hp)),
        out_shape=jax.ShapeDtypeStruct((B, S, MLA_WIDTH), F32),
        compiler_params=pltpu.CompilerParams(
            dimension_semantics=("arbitrary", "arbitrary", "arbitrary"), vmem_limit_bytes=VMEM_LIMIT),
        name="attn",
    )(q, k, v)


def _post_kernel(x_ref, om_ref, sg_ref, omem_ref, wmix_ref, wmem_ref, y_ref):
    o = (om_ref[0] * sg_ref[0]).astype(BF16)
    y_ref[0] = x_ref[0] + _dot(o, wmix_ref[0]) + _dot(omem_ref[0], wmem_ref[0])


def _post(x, o_mix, sg, o_mem, w, layer):
    B, S, _ = x.shape
    tm = ROW_TILE
    return pl.pallas_call(
        _post_kernel,
        grid=(B, S // tm),
        in_specs=[
            pl.BlockSpec((1, tm, D_MODEL), lambda b, i: (b, i, 0)),
            pl.BlockSpec((1, tm, MLA_WIDTH), lambda b, i: (b, i, 0)),
            pl.BlockSpec((1, tm, MLA_WIDTH), lambda b, i: (b, i, 0)),
            pl.BlockSpec((1, tm, MEM_WIDTH), lambda b, i: (b, i, 0)),
            _const_spec(w["wout_mix"].shape, layer),
            _const_spec(w["wout_mem"].shape, layer),
        ],
        out_specs=pl.BlockSpec((1, tm, D_MODEL), lambda b, i: (b, i, 0)),
        out_shape=jax.ShapeDtypeStruct((B, S, D_MODEL), F32),
        compiler_params=pltpu.CompilerParams(
            dimension_semantics=("arbitrary", "arbitrary"), vmem_limit_bytes=VMEM_LIMIT),
        name="post_a",
    )(x, o_mix, sg, o_mem, w["wout_mix"], w["wout_mem"])


def _layer_b_kernel(x_ref, ln_ref, wu_ref, wv_ref, wqm_ref, wgmix_ref, wgmem_ref, vg_ref, ws_ref,
                    bias_ref, mqg_ref, kmem_ref, vmem_ref, womix_ref, womem_ref, y_ref):
    x = x_ref[0]
    rows = x.shape[0]
    h = (x * _rsqrt_mean(x, D_MODEL) * ln_ref[0]).astype(BF16)
    u = _gelu_tanh(_dot(h, wu_ref[0]))
    v = _gelu_tanh(_dot(h, wv_ref[0]))
    vn = (v * _rsqrt_mean(v, SG_WIDTH) * vg_ref[0]).astype(BF16)
    sg_mix = _silu(_dot(h, wgmix_ref[0]))
    bias = bias_ref[0]
    chunks = []
    for c in range(rows // CHUNK):
        groups = []
        for g in range(SG_GROUPS):
            vc = vn[c * CHUNK:(c + 1) * CHUNK, g * LANE:(g + 1) * LANE]
            groups.append(_dot(ws_ref[0, g], vc))
        chunks.append(jnp.concatenate(groups, axis=1) + bias)
    mixed = jnp.concatenate(chunks, axis=0)
    o_mix = (u * mixed * sg_mix).astype(BF16)

    sg_mem = _silu(_dot(h, wgmem_ref[0]))
    o_mem = _memory_attention(_dot(h, wqm_ref[0]), mqg_ref[0], kmem_ref, vmem_ref)
    o_mem = (o_mem * sg_mem).astype(BF16)
    y_ref[0] = x + _dot(o_mix, womix_ref[0]) + _dot(o_mem, womem_ref[0])


def _layer_b(x, w, j, layer, kmem, vmem):
    B, S, _ = x.shape
    tm = ROW_TILE
    weights = [w["ln_g"], w["b_wu"], w["b_wv"], w["b_wqm"], w["b_wgmix"], w["b_wgmem"], w["b_vg"],
               w["b_ws"], w["b_bias"], w["mq_g"]]
    wlayer = [layer, j, j, j, j, j, j, j, j, layer]
    in_specs = [pl.BlockSpec((1, tm, D_MODEL), lambda b, i: (b, i, 0))]
    in_specs += [_const_spec(a.shape, l) for a, l in zip(weights, wlayer)]
    in_specs += [
        pl.BlockSpec((1, 1, MEM_HEADS, N_MEM, LANE), lambda b, i: (layer, b, 0, 0, 0)),
        pl.BlockSpec((1, 1, MEM_HEADS, N_MEM, MEM_WIDTH), lambda b, i: (layer, b, 0, 0, 0)),
        _const_spec(w["b_womix"].shape, j),
        _const_spec(w["wout_mem"].shape, layer),
    ]
    return pl.pallas_call(
        _layer_b_kernel,
        grid=(B, S // tm),
        in_specs=in_specs,
        out_specs=pl.BlockSpec((1, tm, D_MODEL), lambda b, i: (b, i, 0)),
        out_shape=jax.ShapeDtypeStruct((B, S, D_MODEL), F32),
        compiler_params=pltpu.CompilerParams(
            dimension_semantics=("arbitrary", "arbitrary"), vmem_limit_bytes=VMEM_LIMIT),
        name="layer_b",
    )(x, *weights, kmem, vmem, w["b_womix"], w["wout_mem"])


def _pad_heads(wm, n_heads, width):
    lead = wm.shape[:-1]
    wm = wm.reshape(lead + (n_heads, width))
    wm = jnp.pad(wm, [(0, 0)] * len(lead) + [(0, 0), (0, LANE - width)])
    return wm.reshape(lead + (n_heads * LANE,))


def _rope_partner(a):
    return jnp.concatenate([a[..., HALF_ROPE:], a[..., :HALF_ROPE]], axis=-1)


def _rope_block(a):
    return jnp.pad(a, [(0, 0)] * (a.ndim - 1) + [(QK_NOPE, LANE - QK_HEAD)])


def _prep_weights(ln_g, w_in_a, a_qlat_g, a_kvlat_g, a_w_uq, a_w_ukv, a_q_norm_g, a_k_norm_g, w_in_b,
                  b_v_norm_g, b_w_s, b_bias, mem_norm_g, w_mem_kv, mem_q_norm_g, mem_k_norm_g, w_out):
    w = {}
    na, nb = w_in_a.shape[0], w_in_b.shape[0]
    w["ln_g"] = ln_g[:, None, :]
    w["mem_g"] = mem_norm_g[:, None, :]

    o1, o2, o3, o4 = Q_LORA, Q_LORA + KV_LORA, Q_LORA + KV_LORA + QK_ROPE, Q_LORA + KV_LORA + QK_ROPE + MEM_WIDTH
    w_kpe = w_in_a[:, :, o2:o3]
    w["a_wlat"] = jnp.concatenate(
        [w_in_a[:, :, :o2], _rope_block(w_kpe), _rope_block(_rope_partner(w_kpe))], axis=-1).astype(BF16)
    w["a_wqm"] = _pad_heads(w_in_a[:, :, o3:o4], MEM_HEADS, MEM_HDIM).astype(BF16)
    w["a_wgate"] = w_in_a[:, :, o4:].astype(BF16)
    w["a_qlat_g"] = a_qlat_g[:, None, :]
    w["a_kvlat_g"] = a_kvlat_g[:, None, :]
    uq = a_w_uq.reshape(na, Q_LORA, MLA_HEADS, QK_HEAD)
    w["a_wuq"] = _pad_heads(a_w_uq, MLA_HEADS, QK_HEAD).astype(BF16)
    uq_sw = _rope_block(_rope_partner(uq[..., QK_NOPE:]))
    w["a_wuqs"] = uq_sw.reshape(na, Q_LORA, MLA_HEADS * LANE).astype(BF16)
    ukv = a_w_ukv.reshape(na, KV_LORA, MLA_HEADS, QK_NOPE + V_HEAD)
    w["a_wk"] = _pad_heads(ukv[..., :QK_NOPE].reshape(na, KV_LORA, MLA_HEADS * QK_NOPE),
                           MLA_HEADS, QK_NOPE).astype(BF16)
    w["a_wv"] = ukv[..., QK_NOPE:].reshape(na, KV_LORA, MLA_WIDTH).astype(BF16)

    def norm_gain_rows(g):
        g_c = jnp.pad(g, [(0, 0), (0, LANE - QK_HEAD)])
        g_s = _rope_block(_rope_partner(g[:, QK_NOPE:]))
        return jnp.stack([g_c, g_s], axis=1)

    w["a_qg"] = norm_gain_rows(a_q_norm_g)
    w["a_kg"] = norm_gain_rows(a_k_norm_g)

    p1, p2, p3 = SG_WIDTH, 2 * SG_WIDTH, 2 * SG_WIDTH + MEM_WIDTH
    w["b_wu"] = _pad_heads(w_in_b[:, :, :p1], SG_GROUPS, SG_GDIM).astype(BF16)
    w["b_wv"] = _pad_heads(w_in_b[:, :, p1:p2], SG_GROUPS, SG_GDIM).astype(BF16)
    w["b_wqm"] = _pad_heads(w_in_b[:, :, p2:p3], MEM_HEADS, MEM_HDIM).astype(BF16)
    w["b_wgmix"] = _pad_heads(w_in_b[:, :, p3:p3 + SG_WIDTH], SG_GROUPS, SG_GDIM).astype(BF16)
    w["b_wgmem"] = w_in_b[:, :, p3 + SG_WIDTH:].astype(BF16)
    w["b_vg"] = _pad_heads(b_v_norm_g, SG_GROUPS, SG_GDIM)[:, None, :]
    w["b_ws"] = b_w_s.astype(BF16)
    w["b_bias"] = jnp.repeat(jnp.swapaxes(b_bias, 1, 2), LANE, axis=2)

    w["mem_wk"] = _pad_heads(w_mem_kv[:, :, :MEM_WIDTH], MEM_HEADS, MEM_HDIM).astype(BF16)
    w["mem_wv"] = w_mem_kv[:, :, MEM_WIDTH:].astype(BF16)
    w["mq_g"] = jnp.pad(mem_q_norm_g, [(0, 0), (0, LANE - MEM_HDIM)])[:, None, :]
    w["mk_g"] = jnp.pad(mem_k_norm_g, [(0, 0), (0, LANE - MEM_HDIM)])[:, None, :]
    w["wout_mix"] = w_out[:, :MLA_WIDTH, :].astype(BF16)
    w["wout_mem"] = w_out[:, MLA_WIDTH:, :].astype(BF16)
    wo_b = w_out[1::2, :SG_WIDTH, :].reshape(nb, SG_GROUPS, SG_GDIM, D_MODEL)
    wo_b = jnp.pad(wo_b, [(0, 0), (0, 0), (0, LANE - SG_GDIM), (0, 0)])
    w["b_womix"] = wo_b.reshape(nb, SG_PAD, D_MODEL).astype(BF16)
    return w


def _rope_tables(seq):
    inv = 1.0 / (ROPE_BASE ** (jnp.arange(0, QK_ROPE, 2, dtype=F32) / QK_ROPE))
    ang = jnp.arange(seq, dtype=F32)[:, None] * inv[None, :]
    cos, sin = jnp.cos(ang), jnp.sin(ang)
    ct = jnp.concatenate([jnp.ones((seq, QK_NOPE), F32), cos, cos, jnp.zeros((seq, LANE - QK_HEAD), F32)], axis=1)
    st = jnp.concatenate([jnp.zeros((seq, QK_NOPE), F32), -sin, sin, jnp.zeros((seq, LANE - QK_HEAD), F32)], axis=1)
    return ct, st


def _trunk(x, mem, w):
    S = x.shape[1]
    ct, st = _rope_tables(S)
    kmem, vmem = _mem_kv(mem, w["mem_g"], w["mem_wk"], w["mem_wv"], w["mk_g"])
    for layer in range(DEPTH):
        j = layer // 2
        if layer % 2 == 0:
            q, k, v, sg, o_mem = _pre_a(x, ct, st, w, j, layer, kmem, vmem)
            o_mix = _attention(q, k, v)
            x = _post(x, o_mix, sg, o_mem, w, layer)
        else:
            x = _layer_b(x, w, j, layer, kmem, vmem)
    return x


def kernel(x_prompt, x_sample, mem_prompt, mem_sample, ln_g, w_in_a, a_qlat_g, a_kvlat_g, a_w_uq, a_w_ukv,
           a_q_norm_g, a_k_norm_g, w_in_b, b_v_norm_g, b_w_s, b_bias, mem_norm_g, w_mem_kv, mem_q_norm_g,
           mem_k_norm_g, w_out):
    w = _prep_weights(ln_g, w_in_a, a_qlat_g, a_kvlat_g, a_w_uq, a_w_ukv, a_q_norm_g, a_k_norm_g, w_in_b,
                      b_v_norm_g, b_w_s, b_bias, mem_norm_g, w_mem_kv, mem_q_norm_g, mem_k_norm_g, w_out)
    return _trunk(x_prompt, mem_prompt, w), _trunk(x_sample, mem_sample, w)
```

```python
import functools
import math

import jax
import jax.numpy as jnp
from jax import lax
from jax.experimental import pallas as pl
from jax.experimental.pallas import tpu as pltpu

D_MODEL = 1024
DEPTH = 4
EPS = 1e-6
N_MEM = 256
MEM_HEADS = 4
MEM_HDIM = 64
MEM_WIDTH = MEM_HEADS * MEM_HDIM
MLA_HEADS = 12
QK_NOPE = 64
QK_ROPE = 32
QK_HEAD = QK_NOPE + QK_ROPE
V_HEAD = 64
Q_LORA = 384
KV_LORA = 256
MLA_WIDTH = MLA_HEADS * V_HEAD
ROPE_BASE = 10000.0
CHUNK = 128
SG_GROUPS = 8
SG_WIDTH = 768
SG_GDIM = SG_WIDTH // SG_GROUPS

LANE = 128
SG_PAD = SG_GROUPS * LANE
HALF_ROPE = QK_ROPE // 2

ROW_TILE = 256
ATTN_TQ = 512
ATTN_TK = 1024
SAFE_TQ = 256
SAFE_TK = 512
VMEM_LIMIT = 56 * 1024 * 1024
LOG2E = 1.4426950408889634
MAX_FIXED_SHIFT = 40.0

BF16 = jnp.bfloat16
F32 = jnp.float32


def _rsqrt_mean(x, n):
    return lax.rsqrt(jnp.sum(x * x, axis=-1, keepdims=True) * (1.0 / n) + EPS)


def _silu(g):
    return g / (1.0 + jnp.exp(-g))


def _gelu_tanh(x):
    c = math.sqrt(2.0 / math.pi)
    return 0.5 * x * (1.0 + jnp.tanh(c * (x + 0.044715 * (x * x * x))))


def _dot(a, b):
    return jnp.dot(a, b, preferred_element_type=F32)


def _dot_nt(a, b):
    return lax.dot_general(a, b, (((1,), (1,)), ((), ())), preferred_element_type=F32)


def _mem_kv_kernel(mem_ref, g_ref, wk_ref, wv_ref, kg_ref, k_out, v_out):
    m = mem_ref[0]
    mn = (m * _rsqrt_mean(m, D_MODEL) * g_ref[0]).astype(BF16)
    kk = _dot(mn, wk_ref[0])
    vv = _dot(mn, wv_ref[0])
    lane = lax.broadcasted_iota(jnp.int32, (N_MEM, MEM_WIDTH), 1)
    for h in range(MEM_HEADS):
        kh = kk[:, h * LANE:(h + 1) * LANE]
        k_out[0, 0, h] = (kh * _rsqrt_mean(kh, MEM_HDIM) * kg_ref[0]).astype(BF16)
        sel = (lane >= h * MEM_HDIM) & (lane < (h + 1) * MEM_HDIM)
        v_out[0, 0, h] = jnp.where(sel, vv, 0.0).astype(BF16)


def _mem_kv(mem, mem_g, wk, wv, kg):
    B = mem.shape[0]
    return pl.pallas_call(
        _mem_kv_kernel,
        grid=(DEPTH, B),
        in_specs=[
            pl.BlockSpec((1, N_MEM, D_MODEL), lambda l, b: (b, 0, 0)),
            pl.BlockSpec((1, 1, D_MODEL), lambda l, b: (l, 0, 0)),
            pl.BlockSpec((1, D_MODEL, MEM_HEADS * LANE), lambda l, b: (l, 0, 0)),
            pl.BlockSpec((1, D_MODEL, MEM_WIDTH), lambda l, b: (l, 0, 0)),
            pl.BlockSpec((1, 1, LANE), lambda l, b: (l, 0, 0)),
        ],
        out_specs=[
            pl.BlockSpec((1, 1, MEM_HEADS, N_MEM, LANE), lambda l, b: (l, b, 0, 0, 0)),
            pl.BlockSpec((1, 1, MEM_HEADS, N_MEM, MEM_WIDTH), lambda l, b: (l, b, 0, 0, 0)),
        ],
        out_shape=[
            jax.ShapeDtypeStruct((DEPTH, B, MEM_HEADS, N_MEM, LANE), BF16),
            jax.ShapeDtypeStruct((DEPTH, B, MEM_HEADS, N_MEM, MEM_WIDTH), BF16),
        ],
        compiler_params=pltpu.CompilerParams(
            dimension_semantics=("arbitrary", "arbitrary"), vmem_limit_bytes=VMEM_LIMIT),
        name="mem_kv",
    )(mem, mem_g, wk, wv, kg)


def _memory_attention(qm, qg, kmem_ref, vmem_ref):
    o = None
    for h in range(MEM_HEADS):
        qh = qm[:, h * LANE:(h + 1) * LANE]
        qn = (qh * _rsqrt_mean(qh, MEM_HDIM) * qg).astype(BF16)
        s = _dot_nt(qn, kmem_ref[0, 0, h]) * (MEM_HDIM ** -0.5)
        e = jnp.exp(s - jnp.max(s, axis=-1, keepdims=True))
        p = (e / jnp.sum(e, axis=-1, keepdims=True)).astype(BF16)
        oh = _dot(p, vmem_ref[0, 0, h])
        o = oh if o is None else o + oh
    return o


def _pre_a_kernel(x_ref, ct_ref, st_ref, ln_ref, wlat_ref, wqm_ref, wgate_ref, qlg_ref, kvlg_ref,
                  wuq_ref, wuqs_ref, wk_ref, wv_ref, qg_ref, kg_ref, mqg_ref, kmem_ref, vmem_ref,
                  q_out, k_out, v_out, sg_out, om_out):
    x = x_ref[0]
    h = (x * _rsqrt_mean(x, D_MODEL) * ln_ref[0]).astype(BF16)

    lat = _dot(h, wlat_ref[0])
    q_lat = lat[:, :Q_LORA]
    kv_lat = lat[:, Q_LORA:Q_LORA + KV_LORA]
    kpe = lat[:, Q_LORA + KV_LORA:Q_LORA + KV_LORA + LANE]
    kpe_sw = lat[:, Q_LORA + KV_LORA + LANE:]

    ct = ct_ref[...]
    st = st_ref[...]

    qn = (q_lat * _rsqrt_mean(q_lat, Q_LORA) * qlg_ref[0]).astype(BF16)
    q_raw = _dot(qn, wuq_ref[0])
    q_sw = _dot(qn, wuqs_ref[0])
    q_c = ct * qg_ref[0, 0:1]
    q_s = st * qg_ref[0, 1:2]
    scale = QK_HEAD ** -0.5 * LOG2E
    for hd in range(MLA_HEADS):
        qh = q_raw[:, hd * LANE:(hd + 1) * LANE]
        r = _rsqrt_mean(qh, QK_HEAD)
        qo = (qh * q_c + q_sw[:, hd * LANE:(hd + 1) * LANE] * q_s) * (r * scale)
        q_out[0, hd] = qo.astype(BF16)

    kvn = (kv_lat * _rsqrt_mean(kv_lat, KV_LORA) * kvlg_ref[0]).astype(BF16)
    k_nope = _dot(kvn, wk_ref[0])
    v_out[0] = _dot(kvn, wv_ref[0]).astype(BF16)
    k_cg = kg_ref[0, 0:1]
    kpe_rot = kpe * (ct * k_cg) + kpe_sw * (st * kg_ref[0, 1:2])
    pe_sq = jnp.sum(kpe * kpe, axis=-1, keepdims=True)
    for hd in range(MLA_HEADS):
        kh = k_nope[:, hd * LANE:(hd + 1) * LANE]
        r = lax.rsqrt((jnp.sum(kh * kh, axis=-1, keepdims=True) + pe_sq) * (1.0 / QK_HEAD) + EPS)
        k_out[0, hd] = ((kh * k_cg + kpe_rot) * r).astype(BF16)

    sg = _silu(_dot(h, wgate_ref[0]))
    sg_out[0] = sg[:, :MLA_WIDTH]
    o_mem = _memory_attention(_dot(h, wqm_ref[0]), mqg_ref[0], kmem_ref, vmem_ref)
    om_out[0] = (o_mem * sg[:, MLA_WIDTH:]).astype(BF16)


def _const_spec(shape, layer):
    nd = len(shape)
    return pl.BlockSpec((1,) + tuple(shape[1:]), lambda b, i: (layer,) + (0,) * (nd - 1))


def _pre_a(x, ct, st, w, j, layer, kmem, vmem):
    B, S, _ = x.shape
    tm = ROW_TILE
    weights = [w["ln_g"], w["a_wlat"], w["a_wqm"], w["a_wgate"], w["a_qlat_g"], w["a_kvlat_g"],
               w["a_wuq"], w["a_wuqs"], w["a_wk"], w["a_wv"], w["a_qg"], w["a_kg"], w["mq_g"]]
    wlayer = [layer, j, j, j, j, j, j, j, j, j, j, j, layer]
    in_specs = [
        pl.BlockSpec((1, tm, D_MODEL), lambda b, i: (b, i, 0)),
        pl.BlockSpec((tm, LANE), lambda b, i: (i, 0)),
        pl.BlockSpec((tm, LANE), lambda b, i: (i, 0)),
    ] + [_const_spec(a.shape, l) for a, l in zip(weights, wlayer)] + [
        pl.BlockSpec((1, 1, MEM_HEADS, N_MEM, LANE), lambda b, i: (layer, b, 0, 0, 0)),
        pl.BlockSpec((1, 1, MEM_HEADS, N_MEM, MEM_WIDTH), lambda b, i: (layer, b, 0, 0, 0)),
    ]
    out_specs = [
        pl.BlockSpec((1, MLA_HEADS, tm, LANE), lambda b, i: (b, 0, i, 0)),
        pl.BlockSpec((1, MLA_HEADS, tm, LANE), lambda b, i: (b, 0, i, 0)),
        pl.BlockSpec((1, tm, MLA_WIDTH), lambda b, i: (b, i, 0)),
        pl.BlockSpec((1, tm, MLA_WIDTH), lambda b, i: (b, i, 0)),
        pl.BlockSpec((1, tm, MEM_WIDTH), lambda b, i: (b, i, 0)),
    ]
    out_shape = [
        jax.ShapeDtypeStruct((B, MLA_HEADS, S, LANE), BF16),
        jax.ShapeDtypeStruct((B, MLA_HEADS, S, LANE), BF16),
        jax.ShapeDtypeStruct((B, S, MLA_WIDTH), BF16),
        jax.ShapeDtypeStruct((B, S, MLA_WIDTH), F32),
        jax.ShapeDtypeStruct((B, S, MEM_WIDTH), BF16),
    ]
    return pl.pallas_call(
        _pre_a_kernel,
        grid=(B, S // tm),
        in_specs=in_specs,
        out_specs=out_specs,
        out_shape=out_shape,
        compiler_params=pltpu.CompilerParams(
            dimension_semantics=("arbitrary", "arbitrary"), vmem_limit_bytes=VMEM_LIMIT),
        name="pre_a",
    )(x, ct, st, *weights, kmem, vmem)


def _attn_epilogue(o0, o1, sg_ref, o_ref):
    lane = lax.broadcasted_iota(jnp.int32, o0.shape, 1)
    o_ref[0] = (jnp.where(lane < V_HEAD, o0, o1) * sg_ref[0]).astype(BF16)


def _attn_bounded_kernel(m2_ref, q_ref, k_ref, v_ref, sg_ref, o_ref, acc_ref, l_ref, *, seq):
    tk = ATTN_TK
    m2 = m2_ref[0]
    acc_ref[...] = jnp.zeros_like(acc_ref)
    l_ref[...] = jnp.zeros_like(l_ref)

    def body(j, carry):
        off = pl.multiple_of(j * tk, tk)
        v = v_ref[0, pl.ds(off, tk), :]
        for hh in range(2):
            k = k_ref[0, hh, pl.ds(off, tk), :]
            p = jnp.exp2(_dot_nt(q_ref[0, hh], k) - m2)
            lsum = p[:, :LANE]
            for c in range(1, tk // LANE):
                lsum = lsum + p[:, c * LANE:(c + 1) * LANE]
            l_ref[hh] += lsum
            acc_ref[hh] += _dot(p.astype(BF16), v)
        return carry

    lax.fori_loop(0, seq // tk, body, 0)
    o0 = acc_ref[0] / jnp.sum(l_ref[0], axis=-1, keepdims=True)
    o1 = acc_ref[1] / jnp.sum(l_ref[1], axis=-1, keepdims=True)
    _attn_epilogue(o0, o1, sg_ref, o_ref)


def _attn_online_kernel(q_ref, k_ref, v_ref, sg_ref, o_ref, *, seq):
    tq, tk = SAFE_TQ, SAFE_TK
    outs = []
    for hh in range(2):
        q = q_ref[0, hh]

        def body(j, carry):
            m, l, acc = carry
            off = pl.multiple_of(j * tk, tk)
            k = k_ref[0, hh, pl.ds(off, tk), :]
            v = v_ref[0, pl.ds(off, tk), :]
            s = _dot_nt(q, k)
            m_new = jnp.maximum(m, jnp.max(s, axis=-1, keepdims=True))
            alpha = jnp.exp2(m - m_new)
            p = jnp.exp2(s - m_new)
            l = alpha * l + jnp.sum(p, axis=-1, keepdims=True)
            acc = alpha * acc + _dot(p.astype(BF16), v)
            return m_new, l, acc

        init = (jnp.full((tq, 1), -jnp.inf, F32), jnp.zeros((tq, 1), F32), jnp.zeros((tq, LANE), F32))
        m, l, acc = lax.fori_loop(0, seq // tk, body, init)
        outs.append(acc / l)
    _attn_epilogue(outs[0], outs[1], sg_ref, o_ref)


def _attention_call(kernel_fn, tq, q, k, v, sg, extra_in, extra_specs, scratch, name):
    B, _, S, _ = q.shape
    return pl.pallas_call(
        functools.partial(kernel_fn, seq=S),
        grid=(B, MLA_HEADS // 2, S // tq),
        in_specs=extra_specs + [
            pl.BlockSpec((1, 2, tq, LANE), lambda b, hp, i: (b, hp, i, 0)),
            pl.BlockSpec((1, 2, S, LANE), lambda b, hp, i: (b, hp, 0, 0)),
            pl.BlockSpec((1, S, LANE), lambda b, hp, i: (b, 0, hp)),
            pl.BlockSpec((1, tq, LANE), lambda b, hp, i: (b, i, hp)),
        ],
        out_specs=pl.BlockSpec((1, tq, LANE), lambda b, hp, i: (b, i, hp)),
        out_shape=jax.ShapeDtypeStruct((B, S, MLA_WIDTH), BF16),
        scratch_shapes=scratch,
        compiler_params=pltpu.CompilerParams(
            dimension_semantics=("arbitrary", "arbitrary", "arbitrary"), vmem_limit_bytes=VMEM_LIMIT),
        name=name,
    )(*extra_in, q, k, v, sg)


def _attention(q, k, v, sg, m2):
    def bounded(ops):
        scratch = [pltpu.VMEM((2, ATTN_TQ, LANE), F32), pltpu.VMEM((2, ATTN_TQ, LANE), F32)]
        return _attention_call(_attn_bounded_kernel, ATTN_TQ, *ops, [m2],
                               [pl.BlockSpec(memory_space=pltpu.SMEM)], scratch, "attn")

    def online(ops):
        return _attention_call(_attn_online_kernel, SAFE_TQ, *ops, [], [], [], "attn_online")

    return lax.cond(m2[0] <= MAX_FIXED_SHIFT, bounded, online, (q, k, v, sg))


def _post_kernel(x_ref, og_ref, omem_ref, wmix_ref, wmem_ref, y_ref):
    y_ref[0] = x_ref[0] + _dot(og_ref[0], wmix_ref[0]) + _dot(omem_ref[0], wmem_ref[0])


def _post(x, o_gated, o_mem, w, layer):
    B, S, _ = x.shape
    tm = ROW_TILE
    return pl.pallas_call(
        _post_kernel,
        grid=(B, S // tm),
        in_specs=[
            pl.BlockSpec((1, tm, D_MODEL), lambda b, i: (b, i, 0)),
            pl.BlockSpec((1, tm, MLA_WIDTH), lambda b, i: (b, i, 0)),
            pl.BlockSpec((1, tm, MEM_WIDTH), lambda b, i: (b, i, 0)),
            _const_spec(w["wout_mix"].shape, layer),
            _const_spec(w["wout_mem"].shape, layer),
        ],
        out_specs=pl.BlockSpec((1, tm, D_MODEL), lambda b, i: (b, i, 0)),
        out_shape=jax.ShapeDtypeStruct((B, S, D_MODEL), F32),
        compiler_params=pltpu.CompilerParams(
            dimension_semantics=("arbitrary", "arbitrary"), vmem_limit_bytes=VMEM_LIMIT),
        name="post_a",
    )(x, o_gated, o_mem, w["wout_mix"], w["wout_mem"])


def _layer_b_kernel(x_ref, ln_ref, wu_ref, wv_ref, wqm_ref, wgmix_ref, wgmem_ref, vg_ref, ws_ref,
                    bias_ref, mqg_ref, kmem_ref, vmem_ref, womix_ref, womem_ref, y_ref):
    x = x_ref[0]
    rows = x.shape[0]
    h = (x * _rsqrt_mean(x, D_MODEL) * ln_ref[0]).astype(BF16)
    u = _gelu_tanh(_dot(h, wu_ref[0]))
    v = _gelu_tanh(_dot(h, wv_ref[0]))
    vn = (v * _rsqrt_mean(v, SG_WIDTH) * vg_ref[0]).astype(BF16)
    sg_mix = _silu(_dot(h, wgmix_ref[0]))
    bias = bias_ref[0]
    chunks = []
    for c in range(rows // CHUNK):
        groups = []
        for g in range(SG_GROUPS):
            vc = vn[c * CHUNK:(c + 1) * CHUNK, g * LANE:(g + 1) * LANE]
            groups.append(_dot(ws_ref[0, g], vc))
        chunks.append(jnp.concatenate(groups, axis=1) + bias)
    mixed = jnp.concatenate(chunks, axis=0)
    o_mix = (u * mixed * sg_mix).astype(BF16)

    sg_mem = _silu(_dot(h, wgmem_ref[0]))
    o_mem = _memory_attention(_dot(h, wqm_ref[0]), mqg_ref[0], kmem_ref, vmem_ref)
    o_mem = (o_mem * sg_mem).astype(BF16)
    y_ref[0] = x + _dot(o_mix, womix_ref[0]) + _dot(o_mem, womem_ref[0])


def _layer_b(x, w, j, layer, kmem, vmem):
    B, S, _ = x.shape
    tm = ROW_TILE
    weights = [w["ln_g"], w["b_wu"], w["b_wv"], w["b_wqm"], w["b_wgmix"], w["b_wgmem"], w["b_vg"],
               w["b_ws"], w["b_bias"], w["mq_g"]]
    wlayer = [layer, j, j, j, j, j, j, j, j, layer]
    in_specs = [pl.BlockSpec((1, tm, D_MODEL), lambda b, i: (b, i, 0))]
    in_specs += [_const_spec(a.shape, l) for a, l in zip(weights, wlayer)]
    in_specs += [
        pl.BlockSpec((1, 1, MEM_HEADS, N_MEM, LANE), lambda b, i: (layer, b, 0, 0, 0)),
        pl.BlockSpec((1, 1, MEM_HEADS, N_MEM, MEM_WIDTH), lambda b, i: (layer, b, 0, 0, 0)),
        _const_spec(w["b_womix"].shape, j),
        _const_spec(w["wout_mem"].shape, layer),
    ]
    return pl.pallas_call(
        _layer_b_kernel,
        grid=(B, S // tm),
        in_specs=in_specs,
        out_specs=pl.BlockSpec((1, tm, D_MODEL), lambda b, i: (b, i, 0)),
        out_shape=jax.ShapeDtypeStruct((B, S, D_MODEL), F32),
        compiler_params=pltpu.CompilerParams(
            dimension_semantics=("arbitrary", "arbitrary"), vmem_limit_bytes=VMEM_LIMIT),
        name="layer_b",
    )(x, *weights, kmem, vmem, w["b_womix"], w["wout_mem"])


def _pad_heads(wm, n_heads, width):
    lead = wm.shape[:-1]
    wm = wm.reshape(lead + (n_heads, width))
    wm = jnp.pad(wm, [(0, 0)] * len(lead) + [(0, 0), (0, LANE - width)])
    return wm.reshape(lead + (n_heads * LANE,))


def _rope_partner(a):
    return jnp.concatenate([a[..., HALF_ROPE:], a[..., :HALF_ROPE]], axis=-1)


def _rope_block(a):
    return jnp.pad(a, [(0, 0)] * (a.ndim - 1) + [(QK_NOPE, LANE - QK_HEAD)])


def _prep_weights(ln_g, w_in_a, a_qlat_g, a_kvlat_g, a_w_uq, a_w_ukv, a_q_norm_g, a_k_norm_g, w_in_b,
                  b_v_norm_g, b_w_s, b_bias, mem_norm_g, w_mem_kv, mem_q_norm_g, mem_k_norm_g, w_out):
    w = {}
    na, nb = w_in_a.shape[0], w_in_b.shape[0]
    w["ln_g"] = ln_g[:, None, :]
    w["mem_g"] = mem_norm_g[:, None, :]

    o1, o2, o3, o4 = Q_LORA, Q_LORA + KV_LORA, Q_LORA + KV_LORA + QK_ROPE, Q_LORA + KV_LORA + QK_ROPE + MEM_WIDTH
    w_kpe = w_in_a[:, :, o2:o3]
    w["a_wlat"] = jnp.concatenate(
        [w_in_a[:, :, :o2], _rope_block(w_kpe), _rope_block(_rope_partner(w_kpe))], axis=-1).astype(BF16)
    w["a_wqm"] = _pad_heads(w_in_a[:, :, o3:o4], MEM_HEADS, MEM_HDIM).astype(BF16)
    w["a_wgate"] = w_in_a[:, :, o4:].astype(BF16)
    w["a_qlat_g"] = a_qlat_g[:, None, :]
    w["a_kvlat_g"] = a_kvlat_g[:, None, :]
    uq = a_w_uq.reshape(na, Q_LORA, MLA_HEADS, QK_HEAD)
    w["a_wuq"] = _pad_heads(a_w_uq, MLA_HEADS, QK_HEAD).astype(BF16)
    uq_sw = _rope_block(_rope_partner(uq[..., QK_NOPE:]))
    w["a_wuqs"] = uq_sw.reshape(na, Q_LORA, MLA_HEADS * LANE).astype(BF16)
    ukv = a_w_ukv.reshape(na, KV_LORA, MLA_HEADS, QK_NOPE + V_HEAD)
    w["a_wk"] = _pad_heads(ukv[..., :QK_NOPE].reshape(na, KV_LORA, MLA_HEADS * QK_NOPE),
                           MLA_HEADS, QK_NOPE).astype(BF16)
    w["a_wv"] = ukv[..., QK_NOPE:].reshape(na, KV_LORA, MLA_WIDTH).astype(BF16)

    def norm_gain_rows(g):
        g_c = jnp.pad(g, [(0, 0), (0, LANE - QK_HEAD)])
        g_s = _rope_block(_rope_partner(g[:, QK_NOPE:]))
        return jnp.stack([g_c, g_s], axis=1)

    w["a_qg"] = norm_gain_rows(a_q_norm_g)
    w["a_kg"] = norm_gain_rows(a_k_norm_g)
    g_max = jnp.max(jnp.abs(a_q_norm_g), axis=1) * jnp.max(jnp.abs(a_k_norm_g), axis=1)
    w["a_m2"] = (QK_HEAD ** 0.5 * LOG2E * g_max)[:, None]

    p1, p2, p3 = SG_WIDTH, 2 * SG_WIDTH, 2 * SG_WIDTH + MEM_WIDTH
    w["b_wu"] = _pad_heads(w_in_b[:, :, :p1], SG_GROUPS, SG_GDIM).astype(BF16)
    w["b_wv"] = _pad_heads(w_in_b[:, :, p1:p2], SG_GROUPS, SG_GDIM).astype(BF16)
    w["b_wqm"] = _pad_heads(w_in_b[:, :, p2:p3], MEM_HEADS, MEM_HDIM).astype(BF16)
    w["b_wgmix"] = _pad_heads(w_in_b[:, :, p3:p3 + SG_WIDTH], SG_GROUPS, SG_GDIM).astype(BF16)
    w["b_wgmem"] = w_in_b[:, :, p3 + SG_WIDTH:].astype(BF16)
    w["b_vg"] = _pad_heads(b_v_norm_g, SG_GROUPS, SG_GDIM)[:, None, :]
    w["b_ws"] = b_w_s.astype(BF16)
    w["b_bias"] = jnp.repeat(jnp.swapaxes(b_bias, 1, 2), LANE, axis=2)

    w["mem_wk"] = _pad_heads(w_mem_kv[:, :, :MEM_WIDTH], MEM_HEADS, MEM_HDIM).astype(BF16)
    w["mem_wv"] = w_mem_kv[:, :, MEM_WIDTH:].astype(BF16)
    w["mq_g"] = jnp.pad(mem_q_norm_g, [(0, 0), (0, LANE - MEM_HDIM)])[:, None, :]
    w["mk_g"] = jnp.pad(mem_k_norm_g, [(0, 0), (0, LANE - MEM_HDIM)])[:, None, :]
    w["wout_mix"] = w_out[:, :MLA_WIDTH, :].astype(BF16)
    w["wout_mem"] = w_out[:, MLA_WIDTH:, :].astype(BF16)
    wo_b = w_out[1::2, :SG_WIDTH, :].reshape(nb, SG_GROUPS, SG_GDIM, D_MODEL)
    wo_b = jnp.pad(wo_b, [(0, 0), (0, 0), (0, LANE - SG_GDIM), (0, 0)])
    w["b_womix"] = wo_b.reshape(nb, SG_PAD, D_MODEL).astype(BF16)
    return w


def _rope_tables(seq):
    inv = 1.0 / (ROPE_BASE ** (jnp.arange(0, QK_ROPE, 2, dtype=F32) / QK_ROPE))
    ang = jnp.arange(seq, dtype=F32)[:, None] * inv[None, :]
    cos, sin = jnp.cos(ang), jnp.sin(ang)
    ct = jnp.concatenate([jnp.ones((seq, QK_NOPE), F32), cos, cos, jnp.zeros((seq, LANE - QK_HEAD), F32)], axis=1)
    st = jnp.concatenate([jnp.zeros((seq, QK_NOPE), F32), -sin, sin, jnp.zeros((seq, LANE - QK_HEAD), F32)], axis=1)
    return ct, st


def _trunk(x, mem, w):
    S = x.shape[1]
    ct, st = _rope_tables(S)
    kmem, vmem = _mem_kv(mem, w["mem_g"], w["mem_wk"], w["mem_wv"], w["mk_g"])
    for layer in range(DEPTH):
        j = layer // 2
        if layer % 2 == 0:
            q, k, v, sg, o_mem = _pre_a(x, ct, st, w, j, layer, kmem, vmem)
            o_gated = _attention(q, k, v, sg, w["a_m2"][j])
            x = _post(x, o_gated, o_mem, w, layer)
        else:
            x = _layer_b(x, w, j, layer, kmem, vmem)
    return x


def kernel(x_prompt, x_sample, mem_prompt, mem_sample, ln_g, w_in_a, a_qlat_g, a_kvlat_g, a_w_uq, a_w_ukv,
           a_q_norm_g, a_k_norm_g, w_in_b, b_v_norm_g, b_w_s, b_bias, mem_norm_g, w_mem_kv, mem_q_norm_g,
           mem_k_norm_g, w_out):
    w = _prep_weights(ln_g, w_in_a, a_qlat_g, a_kvlat_g, a_w_uq, a_w_ukv, a_q_norm_g, a_k_norm_g, w_in_b,
                      b_v_norm_g, b_w_s, b_bias, mem_norm_g, w_mem_kv, mem_q_norm_g, mem_k_norm_g, w_out)
    return _trunk(x_prompt, mem_prompt, w), _trunk(x_sample, mem_sample, w)
```

```python
import functools
import math

import jax
import jax.numpy as jnp
from jax import lax
from jax.experimental import pallas as pl
from jax.experimental.pallas import tpu as pltpu

D_MODEL = 1024
DEPTH = 4
EPS = 1e-6
N_MEM = 256
MEM_HEADS = 4
MEM_HDIM = 64
MEM_WIDTH = MEM_HEADS * MEM_HDIM
MLA_HEADS = 12
QK_NOPE = 64
QK_ROPE = 32
QK_HEAD = QK_NOPE + QK_ROPE
V_HEAD = 64
Q_LORA = 384
KV_LORA = 256
MLA_WIDTH = MLA_HEADS * V_HEAD
ROPE_BASE = 10000.0
CHUNK = 128
SG_GROUPS = 8
SG_WIDTH = 768
SG_GDIM = SG_WIDTH // SG_GROUPS
BRANCH = MLA_WIDTH + MEM_WIDTH

LANE = 128
HALF_LANE = LANE // 2
HALF_ROPE = QK_ROPE // 2

ROW_TILE = 512
ATTN_TQ = 512
ATTN_TK = 2048
SAFE_TQ = 256
SAFE_TK = 512
VMEM_LIMIT = 56 * 1024 * 1024
LOG2E = 1.4426950408889634
MAX_FIXED_SHIFT = 40.0

BF16 = jnp.bfloat16
F32 = jnp.float32


def _rsqrt_mean(x, n):
    return lax.rsqrt(jnp.sum(x * x, axis=-1, keepdims=True) * (1.0 / n) + EPS)


def _silu(g):
    return g / (1.0 + jnp.exp(-g))


def _gelu_tanh(x):
    c = math.sqrt(2.0 / math.pi)
    return 0.5 * x * (1.0 + jnp.tanh(c * (x + 0.044715 * (x * x * x))))


def _dot(a, b):
    return jnp.dot(a, b, preferred_element_type=F32)


def _dot_nt(a, b):
    return lax.dot_general(a, b, (((1,), (1,)), ((), ())), preferred_element_type=F32)


def _low_half(shape):
    return lax.broadcasted_iota(jnp.int32, shape, 1) < HALF_LANE


def _norm_heads64(x, g):
    out = []
    for a in range(MEM_WIDTH // LANE):
        blk = x[:, a * LANE:(a + 1) * LANE]
        lo = _low_half(blk.shape)
        sq = blk * blk
        s_lo = jnp.sum(jnp.where(lo, sq, 0.0), axis=-1, keepdims=True)
        s_hi = jnp.sum(jnp.where(lo, 0.0, sq), axis=-1, keepdims=True)
        r = jnp.where(lo, lax.rsqrt(s_lo * (1.0 / MEM_HDIM) + EPS), lax.rsqrt(s_hi * (1.0 / MEM_HDIM) + EPS))
        out.append(blk * r * g[:, a * LANE:(a + 1) * LANE])
    return out


def _mem_kv_kernel(mem_ref, g_ref, wk_ref, wv_ref, kg_ref, k_out, v_out):
    m = mem_ref[0]
    mn = (m * _rsqrt_mean(m, D_MODEL) * g_ref[0]).astype(BF16)
    kn = _norm_heads64(_dot(mn, wk_ref[0]), kg_ref[0])
    k_out[0, 0] = jnp.concatenate(kn, axis=1).astype(BF16)
    vv = _dot(mn, wv_ref[0])
    lane = lax.broadcasted_iota(jnp.int32, (N_MEM, MEM_WIDTH), 1)
    for h in range(MEM_HEADS):
        vh = jnp.where(lane >= h * MEM_HDIM, jnp.where(lane < (h + 1) * MEM_HDIM, vv, 0.0), 0.0)
        v_out[0, 0, h] = vh.astype(BF16)


def _mem_kv(mem, mem_g, wk, wv, kg):
    B = mem.shape[0]
    return pl.pallas_call(
        _mem_kv_kernel,
        grid=(DEPTH, B),
        in_specs=[
            pl.BlockSpec((1, N_MEM, D_MODEL), lambda l, b: (b, 0, 0)),
            pl.BlockSpec((1, 1, D_MODEL), lambda l, b: (l, 0, 0)),
            pl.BlockSpec((1, D_MODEL, MEM_WIDTH), lambda l, b: (l, 0, 0)),
            pl.BlockSpec((1, D_MODEL, MEM_WIDTH), lambda l, b: (l, 0, 0)),
            pl.BlockSpec((1, 1, MEM_WIDTH), lambda l, b: (l, 0, 0)),
        ],
        out_specs=[
            pl.BlockSpec((1, 1, N_MEM, MEM_WIDTH), lambda l, b: (l, b, 0, 0)),
            pl.BlockSpec((1, 1, MEM_HEADS, N_MEM, MEM_WIDTH), lambda l, b: (l, b, 0, 0, 0)),
        ],
        out_shape=[
            jax.ShapeDtypeStruct((DEPTH, B, N_MEM, MEM_WIDTH), BF16),
            jax.ShapeDtypeStruct((DEPTH, B, MEM_HEADS, N_MEM, MEM_WIDTH), BF16),
        ],
        compiler_params=pltpu.CompilerParams(
            dimension_semantics=("arbitrary", "arbitrary"), vmem_limit_bytes=VMEM_LIMIT),
        name="mem_kv",
    )(mem, mem_g, wk, wv, kg)


def _mem_specs(layer):
    return [
        pl.BlockSpec((1, 1, N_MEM, MEM_WIDTH), lambda b, i: (layer, b, 0, 0)),
        pl.BlockSpec((1, 1, MEM_HEADS, N_MEM, MEM_WIDTH), lambda b, i: (layer, b, 0, 0, 0)),
    ]


def _memory_attention(qm, qg, kmem_ref, vmem_ref):
    qn = _norm_heads64(qm, qg)
    o = None
    for h in range(MEM_HEADS):
        a, upper = divmod(h, 2)
        lo = _low_half(qn[a].shape)
        qh = jnp.where(lo, 0.0, qn[a]) if upper else jnp.where(lo, qn[a], 0.0)
        qh = (qh * (MEM_HDIM ** -0.5)).astype(BF16)
        s = _dot_nt(qh, kmem_ref[0, 0, :, a * LANE:(a + 1) * LANE])
        e = jnp.exp(s - jnp.max(s, axis=-1, keepdims=True))
        p = (e / jnp.sum(e, axis=-1, keepdims=True)).astype(BF16)
        oh = _dot(p, vmem_ref[0, 0, h])
        o = oh if o is None else o + oh
    return o


def _pre_a_kernel(x_ref, ct_ref, st_ref, ln_ref, wlat_ref, wqm_ref, wgate_ref, qlg_ref, kvlg_ref,
                  wuq_ref, wuqs_ref, wk_ref, wv_ref, qg_ref, kg_ref, mqg_ref, kmem_ref, vmem_ref,
                  q_out, k_out, v_out, sg_out, om_out):
    x = x_ref[0]
    h = (x * _rsqrt_mean(x, D_MODEL) * ln_ref[0]).astype(BF16)

    lat = _dot(h, wlat_ref[0])
    q_lat = lat[:, :Q_LORA]
    kv_lat = lat[:, Q_LORA:Q_LORA + KV_LORA]
    kpe = lat[:, Q_LORA + KV_LORA:]

    ct = ct_ref[...]
    st = st_ref[...]

    qn = (q_lat * _rsqrt_mean(q_lat, Q_LORA) * qlg_ref[0]).astype(BF16)
    q_raw = _dot(qn, wuq_ref[0])
    q_sw = _dot(qn, wuqs_ref[0])
    q_c = ct * qg_ref[0, 0:1]
    q_s = st * qg_ref[0, 1:2]
    scale = QK_HEAD ** -0.5 * LOG2E
    for hd in range(MLA_HEADS):
        qh = q_raw[:, hd * LANE:(hd + 1) * LANE]
        r = _rsqrt_mean(qh, QK_HEAD)
        qo = (qh * q_c + q_sw[:, hd * LANE:(hd + 1) * LANE] * q_s) * (r * scale)
        q_out[0, hd] = qo.astype(BF16)

    kvn = (kv_lat * _rsqrt_mean(kv_lat, KV_LORA) * kvlg_ref[0]).astype(BF16)
    k_nope = _dot(kvn, wk_ref[0])
    v_out[0] = _dot(kvn, wv_ref[0]).astype(BF16)
    k_cg = kg_ref[0, 0:1]
    kpe_rot = kpe * (ct * k_cg) + pltpu.roll(kpe, HALF_LANE, 1) * (st * kg_ref[0, 1:2])
    pe_sq = jnp.sum(kpe * kpe, axis=-1, keepdims=True)
    for hd in range(MLA_HEADS):
        kh = k_nope[:, hd * LANE:(hd + 1) * LANE]
        r = lax.rsqrt((jnp.sum(kh * kh, axis=-1, keepdims=True) + pe_sq) * (1.0 / QK_HEAD) + EPS)
        k_out[0, hd] = ((kh * k_cg + kpe_rot) * r).astype(BF16)

    sg = _silu(_dot(h, wgate_ref[0]))
    sg_out[0] = sg[:, :MLA_WIDTH]
    o_mem = _memory_attention(_dot(h, wqm_ref[0]), mqg_ref[0], kmem_ref, vmem_ref)
    om_out[0] = (o_mem * sg[:, MLA_WIDTH:]).astype(BF16)


def _const_spec(shape, layer):
    nd = len(shape)
    return pl.BlockSpec((1,) + tuple(shape[1:]), lambda b, i: (layer,) + (0,) * (nd - 1))


def _pre_a(x, ct, st, w, j, layer, kmem, vmem):
    B, S, _ = x.shape
    tm = ROW_TILE
    weights = [w["ln_g"], w["a_wlat"], w["a_wqm"], w["a_wgate"], w["a_qlat_g"], w["a_kvlat_g"],
               w["a_wuq"], w["a_wuqs"], w["a_wk"], w["a_wv"], w["a_qg"], w["a_kg"], w["mq_g"]]
    wlayer = [layer, j, j, j, j, j, j, j, j, j, j, j, layer]
    in_specs = [
        pl.BlockSpec((1, tm, D_MODEL), lambda b, i: (b, i, 0)),
        pl.BlockSpec((tm, LANE), lambda b, i: (i, 0)),
        pl.BlockSpec((tm, LANE), lambda b, i: (i, 0)),
    ] + [_const_spec(a.shape, l) for a, l in zip(weights, wlayer)] + _mem_specs(layer)
    out_specs = [
        pl.BlockSpec((1, MLA_HEADS, tm, LANE), lambda b, i: (b, 0, i, 0)),
        pl.BlockSpec((1, MLA_HEADS, tm, LANE), lambda b, i: (b, 0, i, 0)),
        pl.BlockSpec((1, tm, MLA_WIDTH), lambda b, i: (b, i, 0)),
        pl.BlockSpec((1, tm, MLA_WIDTH), lambda b, i: (b, i, 0)),
        pl.BlockSpec((1, tm, MEM_WIDTH), lambda b, i: (b, i, 0)),
    ]
    out_shape = [
        jax.ShapeDtypeStruct((B, MLA_HEADS, S, LANE), BF16),
        jax.ShapeDtypeStruct((B, MLA_HEADS, S, LANE), BF16),
        jax.ShapeDtypeStruct((B, S, MLA_WIDTH), BF16),
        jax.ShapeDtypeStruct((B, S, MLA_WIDTH), F32),
        jax.ShapeDtypeStruct((B, S, MEM_WIDTH), BF16),
    ]
    return pl.pallas_call(
        _pre_a_kernel,
        grid=(B, S // tm),
        in_specs=in_specs,
        out_specs=out_specs,
        out_shape=out_shape,
        compiler_params=pltpu.CompilerParams(
            dimension_semantics=("arbitrary", "arbitrary"), vmem_limit_bytes=VMEM_LIMIT),
        name="pre_a",
    )(x, ct, st, *weights, kmem, vmem)


def _attn_epilogue(o0, o1, sg_ref, o_ref):
    o_ref[0] = (jnp.where(_low_half(o0.shape), o0, o1) * sg_ref[0]).astype(BF16)


def _attn_bounded_kernel(m2_ref, q_ref, k_ref, v_ref, sg_ref, o_ref, acc_ref, l_ref, *, seq):
    tk = ATTN_TK
    m2 = m2_ref[0]
    acc_ref[...] = jnp.zeros_like(acc_ref)
    l_ref[...] = jnp.zeros_like(l_ref)

    def body(j, carry):
        off = pl.multiple_of(j * tk, tk)
        v = v_ref[0, pl.ds(off, tk), :]
        for hh in range(2):
            k = k_ref[0, hh, pl.ds(off, tk), :]
            p = jnp.exp2(_dot_nt(q_ref[0, hh], k) - m2)
            lsum = p[:, :LANE]
            for c in range(1, tk // LANE):
                lsum = lsum + p[:, c * LANE:(c + 1) * LANE]
            l_ref[hh] += lsum
            acc_ref[hh] += _dot(p.astype(BF16), v)
        return carry

    lax.fori_loop(0, seq // tk, body, 0)
    o0 = acc_ref[0] / jnp.sum(l_ref[0], axis=-1, keepdims=True)
    o1 = acc_ref[1] / jnp.sum(l_ref[1], axis=-1, keepdims=True)
    _attn_epilogue(o0, o1, sg_ref, o_ref)


def _attn_online_kernel(q_ref, k_ref, v_ref, sg_ref, o_ref, *, seq):
    tq, tk = SAFE_TQ, SAFE_TK
    outs = []
    for hh in range(2):
        q = q_ref[0, hh]

        def body(j, carry):
            m, l, acc = carry
            off = pl.multiple_of(j * tk, tk)
            k = k_ref[0, hh, pl.ds(off, tk), :]
            v = v_ref[0, pl.ds(off, tk), :]
            s = _dot_nt(q, k)
            m_new = jnp.maximum(m, jnp.max(s, axis=-1, keepdims=True))
            alpha = jnp.exp2(m - m_new)
            p = jnp.exp2(s - m_new)
            l = alpha * l + jnp.sum(p, axis=-1, keepdims=True)
            acc = alpha * acc + _dot(p.astype(BF16), v)
            return m_new, l, acc

        init = (jnp.full((tq, 1), -jnp.inf, F32), jnp.zeros((tq, 1), F32), jnp.zeros((tq, LANE), F32))
        m, l, acc = lax.fori_loop(0, seq // tk, body, init)
        outs.append(acc / l)
    _attn_epilogue(outs[0], outs[1], sg_ref, o_ref)


def _attention_call(kernel_fn, tq, q, k, v, sg, extra_in, extra_specs, scratch, name):
    B, _, S, _ = q.shape
    return pl.pallas_call(
        functools.partial(kernel_fn, seq=S),
        grid=(B, MLA_HEADS // 2, S // tq),
        in_specs=extra_specs + [
            pl.BlockSpec((1, 2, tq, LANE), lambda b, hp, i: (b, hp, i, 0)),
            pl.BlockSpec((1, 2, S, LANE), lambda b, hp, i: (b, hp, 0, 0)),
            pl.BlockSpec((1, S, LANE), lambda b, hp, i: (b, 0, hp)),
            pl.BlockSpec((1, tq, LANE), lambda b, hp, i: (b, i, hp)),
        ],
        out_specs=pl.BlockSpec((1, tq, LANE), lambda b, hp, i: (b, i, hp)),
        out_shape=jax.ShapeDtypeStruct((B, S, MLA_WIDTH), BF16),
        scratch_shapes=scratch,
        compiler_params=pltpu.CompilerParams(
            dimension_semantics=("arbitrary", "arbitrary", "arbitrary"), vmem_limit_bytes=VMEM_LIMIT),
        name=name,
    )(*extra_in, q, k, v, sg)


def _attention(q, k, v, sg, m2):
    def bounded(ops):
        scratch = [pltpu.VMEM((2, ATTN_TQ, LANE), F32), pltpu.VMEM((2, ATTN_TQ, LANE), F32)]
        return _attention_call(_attn_bounded_kernel, ATTN_TQ, *ops, [m2],
                               [pl.BlockSpec(memory_space=pltpu.SMEM)], scratch, "attn")

    def online(ops):
        return _attention_call(_attn_online_kernel, SAFE_TQ, *ops, [], [], [], "attn_online")

    return lax.cond(m2[0] <= MAX_FIXED_SHIFT, bounded, online, (q, k, v, sg))


def _out_proj(x, o_mix, o_mem, wout_ref):
    return x + _dot(o_mix, wout_ref[0, :MLA_WIDTH, :]) + _dot(o_mem, wout_ref[0, MLA_WIDTH:, :])


def _post_kernel(x_ref, og_ref, omem_ref, wout_ref, y_ref):
    y_ref[0] = _out_proj(x_ref[0], og_ref[0], omem_ref[0], wout_ref)


def _post(x, o_gated, o_mem, w, layer):
    B, S, _ = x.shape
    tm = ROW_TILE
    return pl.pallas_call(
        _post_kernel,
        grid=(B, S // tm),
        in_specs=[
            pl.BlockSpec((1, tm, D_MODEL), lambda b, i: (b, i, 0)),
            pl.BlockSpec((1, tm, MLA_WIDTH), lambda b, i: (b, i, 0)),
            pl.BlockSpec((1, tm, MEM_WIDTH), lambda b, i: (b, i, 0)),
            _const_spec(w["wout"].shape, layer),
        ],
        out_specs=pl.BlockSpec((1, tm, D_MODEL), lambda b, i: (b, i, 0)),
        out_shape=jax.ShapeDtypeStruct((B, S, D_MODEL), F32),
        compiler_params=pltpu.CompilerParams(
            dimension_semantics=("arbitrary", "arbitrary"), vmem_limit_bytes=VMEM_LIMIT),
        name="post_a",
    )(x, o_gated, o_mem, w["wout"])


def _spatial_mix(vn, ws_ref, bias):
    rows = vn.shape[0]
    n_blk = SG_WIDTH // LANE
    masked = []
    for g in range(SG_GROUPS):
        lo, hi = g * SG_GDIM, (g + 1) * SG_GDIM
        b0, b1 = lo // LANE, (hi - 1) // LANE
        blk = vn[:, b0 * LANE:(b1 + 1) * LANE]
        col = lax.broadcasted_iota(jnp.int32, blk.shape, 1) + b0 * LANE
        masked.append((b0, b1, jnp.where(col >= lo, jnp.where(col < hi, blk, 0.0), 0.0).astype(BF16)))
    chunks = []
    for c in range(rows // CHUNK):
        parts = [[] for _ in range(n_blk)]
        for g, (b0, b1, mv) in enumerate(masked):
            res = _dot(ws_ref[0, g], mv[c * CHUNK:(c + 1) * CHUNK])
            for b in range(b0, b1 + 1):
                parts[b].append(res[:, (b - b0) * LANE:(b - b0 + 1) * LANE])
        chunks.append(jnp.concatenate([functools.reduce(lambda a, b: a + b, p) for p in parts], axis=1) + bias)
    return jnp.concatenate(chunks, axis=0)


def _layer_b_kernel(x_ref, ln_ref, win_ref, vg_ref, ws_ref, bias_ref, mqg_ref, kmem_ref, vmem_ref,
                    wout_ref, y_ref):
    x = x_ref[0]
    h = (x * _rsqrt_mean(x, D_MODEL) * ln_ref[0]).astype(BF16)
    u = _gelu_tanh(_dot(h, win_ref[0, :, :SG_WIDTH]))
    v = _gelu_tanh(_dot(h, win_ref[0, :, SG_WIDTH:2 * SG_WIDTH]))
    vn = v * _rsqrt_mean(v, SG_WIDTH) * vg_ref[0]
    sg = _silu(_dot(h, win_ref[0, :, 2 * SG_WIDTH + MEM_WIDTH:]))
    o_mix = (u * _spatial_mix(vn, ws_ref, bias_ref[0]) * sg[:, :SG_WIDTH]).astype(BF16)
    qm = _dot(h, win_ref[0, :, 2 * SG_WIDTH:2 * SG_WIDTH + MEM_WIDTH])
    o_mem = (_memory_attention(qm, mqg_ref[0], kmem_ref, vmem_ref) * sg[:, SG_WIDTH:]).astype(BF16)
    y_ref[0] = _out_proj(x, o_mix, o_mem, wout_ref)


def _layer_b(x, w, j, layer, kmem, vmem):
    B, S, _ = x.shape
    tm = ROW_TILE
    weights = [w["ln_g"], w["b_win"], w["b_vg"], w["b_ws"], w["b_bias"], w["mq_g"]]
    wlayer = [layer, j, j, j, j, layer]
    in_specs = [pl.BlockSpec((1, tm, D_MODEL), lambda b, i: (b, i, 0))]
    in_specs += [_const_spec(a.shape, l) for a, l in zip(weights, wlayer)]
    in_specs += _mem_specs(layer) + [_const_spec(w["wout"].shape, layer)]
    return pl.pallas_call(
        _layer_b_kernel,
        grid=(B, S // tm),
        in_specs=in_specs,
        out_specs=pl.BlockSpec((1, tm, D_MODEL), lambda b, i: (b, i, 0)),
        out_shape=jax.ShapeDtypeStruct((B, S, D_MODEL), F32),
        compiler_params=pltpu.CompilerParams(
            dimension_semantics=("arbitrary", "arbitrary"), vmem_limit_bytes=VMEM_LIMIT),
        name="layer_b",
    )(x, *weights, kmem, vmem, w["wout"])


def _zeros_like_last(a, n):
    return jnp.zeros(a.shape[:-1] + (n,), a.dtype)


def _head_lanes(nope, rope):
    ref = nope if nope is not None else rope
    z = lambda n: _zeros_like_last(ref, n)
    n0, n1 = (nope[..., :48], nope[..., 48:]) if nope is not None else (z(48), z(QK_NOPE - 48))
    r0, r1 = (rope[..., :HALF_ROPE], rope[..., HALF_ROPE:]) if rope is not None else (z(HALF_ROPE), z(HALF_ROPE))
    return jnp.concatenate([r0, n0, r1, n1, z(LANE - QK_HEAD)], axis=-1)


def _prep_weights(ln_g, w_in_a, a_qlat_g, a_kvlat_g, a_w_uq, a_w_ukv, a_q_norm_g, a_k_norm_g, w_in_b,
                  b_v_norm_g, b_w_s, b_bias, mem_norm_g, w_mem_kv, mem_q_norm_g, mem_k_norm_g, w_out):
    w = {}
    na = w_in_a.shape[0]
    w["ln_g"] = ln_g[:, None, :]
    w["mem_g"] = mem_norm_g[:, None, :]

    o2, o3, o4 = Q_LORA + KV_LORA, Q_LORA + KV_LORA + QK_ROPE, Q_LORA + KV_LORA + QK_ROPE + MEM_WIDTH
    w["a_wlat"] = jnp.concatenate([w_in_a[:, :, :o2], _head_lanes(None, w_in_a[:, :, o2:o3])], axis=-1).astype(BF16)
    w["a_wqm"] = w_in_a[:, :, o3:o4].astype(BF16)
    w["a_wgate"] = w_in_a[:, :, o4:].astype(BF16)
    w["a_qlat_g"] = a_qlat_g[:, None, :]
    w["a_kvlat_g"] = a_kvlat_g[:, None, :]
    uq = a_w_uq.reshape(na, Q_LORA, MLA_HEADS, QK_HEAD)
    w["a_wuq"] = _head_lanes(uq[..., :QK_NOPE], uq[..., QK_NOPE:]).reshape(na, Q_LORA, MLA_HEADS * LANE).astype(BF16)
    uq_partner = jnp.concatenate([uq[..., QK_NOPE + HALF_ROPE:], uq[..., QK_NOPE:QK_NOPE + HALF_ROPE]], axis=-1)
    w["a_wuqs"] = _head_lanes(None, uq_partner).reshape(na, Q_LORA, MLA_HEADS * LANE).astype(BF16)
    ukv = a_w_ukv.reshape(na, KV_LORA, MLA_HEADS, QK_NOPE + V_HEAD)
    w["a_wk"] = _head_lanes(ukv[..., :QK_NOPE], None).reshape(na, KV_LORA, MLA_HEADS * LANE).astype(BF16)
    w["a_wv"] = ukv[..., QK_NOPE:].reshape(na, KV_LORA, MLA_WIDTH).astype(BF16)

    def norm_gain_rows(g):
        g_c = _head_lanes(g[:, :QK_NOPE], g[:, QK_NOPE:])
        return jnp.stack([g_c, jnp.roll(g_c, HALF_LANE, axis=-1)], axis=1)

    w["a_qg"] = norm_gain_rows(a_q_norm_g)
    w["a_kg"] = norm_gain_rows(a_k_norm_g)
    g_max = jnp.max(jnp.abs(a_q_norm_g), axis=1) * jnp.max(jnp.abs(a_k_norm_g), axis=1)
    w["a_m2"] = (QK_HEAD ** 0.5 * LOG2E * g_max)[:, None]

    w["b_win"] = w_in_b.astype(BF16)
    w["b_vg"] = b_v_norm_g[:, None, :]
    w["b_ws"] = b_w_s.astype(BF16)
    w["b_bias"] = jnp.repeat(jnp.swapaxes(b_bias, 1, 2), SG_GDIM, axis=2)

    w["mem_wk"] = w_mem_kv[:, :, :MEM_WIDTH].astype(BF16)
    w["mem_wv"] = w_mem_kv[:, :, MEM_WIDTH:].astype(BF16)
    w["mq_g"] = jnp.tile(mem_q_norm_g, (1, MEM_HEADS))[:, None, :]
    w["mk_g"] = jnp.tile(mem_k_norm_g, (1, MEM_HEADS))[:, None, :]
    w["wout"] = w_out.astype(BF16)
    return w


def _rope_tables(seq):
    inv = 1.0 / (ROPE_BASE ** (jnp.arange(0, QK_ROPE, 2, dtype=F32) / QK_ROPE))
    ang = jnp.arange(seq, dtype=F32)[:, None] * inv[None, :]
    cos, sin = jnp.cos(ang), jnp.sin(ang)
    ct = _head_lanes(jnp.ones((seq, QK_NOPE), F32), jnp.concatenate([cos, cos], axis=1))
    st = _head_lanes(None, jnp.concatenate([-sin, sin], axis=1))
    return ct, st


def _trunk(x, mem, w):
    S = x.shape[1]
    ct, st = _rope_tables(S)
    kmem, vmem = _mem_kv(mem, w["mem_g"], w["mem_wk"], w["mem_wv"], w["mk_g"])
    for layer in range(DEPTH):
        j = layer // 2
        if layer % 2 == 0:
            q, k, v, sg, o_mem = _pre_a(x, ct, st, w, j, layer, kmem, vmem)
            o_gated = _attention(q, k, v, sg, w["a_m2"][j])
            x = _post(x, o_gated, o_mem, w, layer)
        else:
            x = _layer_b(x, w, j, layer, kmem, vmem)
    return x


def kernel(x_prompt, x_sample, mem_prompt, mem_sample, ln_g, w_in_a, a_qlat_g, a_kvlat_g, a_w_uq, a_w_ukv,
           a_q_norm_g, a_k_norm_g, w_in_b, b_v_norm_g, b_w_s, b_bias, mem_norm_g, w_mem_kv, mem_q_norm_g,
           mem_k_norm_g, w_out):
    w = _prep_weights(ln_g, w_in_a, a_qlat_g, a_kvlat_g, a_w_uq, a_w_ukv, a_q_norm_g, a_k_norm_g, w_in_b,
                      b_v_norm_g, b_w_s, b_bias, mem_norm_g, w_mem_kv, mem_q_norm_g, mem_k_norm_g, w_out)
    return _trunk(x_prompt, mem_prompt, w), _trunk(x_sample, mem_sample, w)
```

```python
import functools
import math

import jax
import jax.numpy as jnp
from jax import lax
from jax.experimental import pallas as pl
from jax.experimental.pallas import tpu as pltpu

D_MODEL = 1024
DEPTH = 4
EPS = 1e-6
N_MEM = 256
MEM_HEADS = 4
MEM_HDIM = 64
MEM_WIDTH = MEM_HEADS * MEM_HDIM
MLA_HEADS = 12
QK_NOPE = 64
QK_ROPE = 32
QK_HEAD = QK_NOPE + QK_ROPE
V_HEAD = 64
Q_LORA = 384
KV_LORA = 256
MLA_WIDTH = MLA_HEADS * V_HEAD
ROPE_BASE = 10000.0
CHUNK = 128
SG_GROUPS = 8
SG_WIDTH = 768
SG_GDIM = SG_WIDTH // SG_GROUPS
BRANCH = MLA_WIDTH + MEM_WIDTH

LANE = 128
HALF_LANE = LANE // 2
HALF_ROPE = QK_ROPE // 2

ROW_TILE = 512
ATTN_TQ = 512
ATTN_TK = 2048
SAFE_TQ = 256
VMEM_LIMIT = 56 * 1024 * 1024
LOG2E = 1.4426950408889634
MAX_FIXED_SHIFT = 40.0

BF16 = jnp.bfloat16
F32 = jnp.float32


def _rsqrt_mean(x, n):
    return lax.rsqrt(jnp.sum(x * x, axis=-1, keepdims=True) * (1.0 / n) + EPS)


def _silu(g):
    return g / (1.0 + jnp.exp(-g))


def _gelu_tanh(x):
    c = math.sqrt(2.0 / math.pi)
    return 0.5 * x * (1.0 + jnp.tanh(c * (x + 0.044715 * (x * x * x))))


def _dot(a, b):
    return jnp.dot(a, b, preferred_element_type=F32)


def _dot_nt(a, b):
    return lax.dot_general(a, b, (((1,), (1,)), ((), ())), preferred_element_type=F32)


def _low_half(shape):
    return lax.broadcasted_iota(jnp.int32, shape, 1) < HALF_LANE


def _norm_heads64(x, g):
    out = []
    for a in range(MEM_WIDTH // LANE):
        blk = x[:, a * LANE:(a + 1) * LANE]
        lo = _low_half(blk.shape)
        sq = blk * blk
        s_lo = jnp.sum(jnp.where(lo, sq, 0.0), axis=-1, keepdims=True)
        s_hi = jnp.sum(jnp.where(lo, 0.0, sq), axis=-1, keepdims=True)
        r = jnp.where(lo, lax.rsqrt(s_lo * (1.0 / MEM_HDIM) + EPS), lax.rsqrt(s_hi * (1.0 / MEM_HDIM) + EPS))
        out.append(blk * r * g[:, a * LANE:(a + 1) * LANE])
    return out


def _mem_kv_kernel(mem_ref, g_ref, wk_ref, wv_ref, kg_ref, k_out, v_out):
    m = mem_ref[0]
    mn = (m * _rsqrt_mean(m, D_MODEL) * g_ref[0]).astype(BF16)
    kn = _norm_heads64(_dot(mn, wk_ref[0]), kg_ref[0])
    k_out[0, 0] = jnp.concatenate(kn, axis=1).astype(BF16)
    vv = _dot(mn, wv_ref[0])
    lane = lax.broadcasted_iota(jnp.int32, (N_MEM, MEM_WIDTH), 1)
    for h in range(MEM_HEADS):
        vh = jnp.where(lane >= h * MEM_HDIM, jnp.where(lane < (h + 1) * MEM_HDIM, vv, 0.0), 0.0)
        v_out[0, 0, h] = vh.astype(BF16)


def _mem_kv(mem, mem_g, wk, wv, kg):
    B = mem.shape[0]
    return pl.pallas_call(
        _mem_kv_kernel,
        grid=(DEPTH, B),
        in_specs=[
            pl.BlockSpec((1, N_MEM, D_MODEL), lambda l, b: (b, 0, 0)),
            pl.BlockSpec((1, 1, D_MODEL), lambda l, b: (l, 0, 0)),
            pl.BlockSpec((1, D_MODEL, MEM_WIDTH), lambda l, b: (l, 0, 0)),
            pl.BlockSpec((1, D_MODEL, MEM_WIDTH), lambda l, b: (l, 0, 0)),
            pl.BlockSpec((1, 1, MEM_WIDTH), lambda l, b: (l, 0, 0)),
        ],
        out_specs=[
            pl.BlockSpec((1, 1, N_MEM, MEM_WIDTH), lambda l, b: (l, b, 0, 0)),
            pl.BlockSpec((1, 1, MEM_HEADS, N_MEM, MEM_WIDTH), lambda l, b: (l, b, 0, 0, 0)),
        ],
        out_shape=[
            jax.ShapeDtypeStruct((DEPTH, B, N_MEM, MEM_WIDTH), BF16),
            jax.ShapeDtypeStruct((DEPTH, B, MEM_HEADS, N_MEM, MEM_WIDTH), BF16),
        ],
        compiler_params=pltpu.CompilerParams(
            dimension_semantics=("arbitrary", "arbitrary"), vmem_limit_bytes=VMEM_LIMIT),
        name="mem_kv",
    )(mem, mem_g, wk, wv, kg)


def _mem_specs(layer):
    return [
        pl.BlockSpec((1, 1, N_MEM, MEM_WIDTH), lambda b, i: (layer, b, 0, 0)),
        pl.BlockSpec((1, 1, MEM_HEADS, N_MEM, MEM_WIDTH), lambda b, i: (layer, b, 0, 0, 0)),
    ]


def _memory_attention(qm, qg, kmem_ref, vmem_ref):
    qn = _norm_heads64(qm, qg)
    o = None
    for h in range(MEM_HEADS):
        a, upper = divmod(h, 2)
        lo = _low_half(qn[a].shape)
        qh = jnp.where(lo, 0.0, qn[a]) if upper else jnp.where(lo, qn[a], 0.0)
        qh = (qh * (MEM_HDIM ** -0.5)).astype(BF16)
        s = _dot_nt(qh, kmem_ref[0, 0, :, a * LANE:(a + 1) * LANE])
        e = jnp.exp(s - jnp.max(s, axis=-1, keepdims=True))
        p = (e / jnp.sum(e, axis=-1, keepdims=True)).astype(BF16)
        oh = _dot(p, vmem_ref[0, 0, h])
        o = oh if o is None else o + oh
    return o


def _pre_a_kernel(x_ref, ct_ref, st_ref, ln_ref, wlat_ref, wqm_ref, wgate_ref, qlg_ref, kvlg_ref,
                  wuq_ref, wuqs_ref, wk_ref, wvt_ref, qg_ref, kg_ref, mqg_ref, kmem_ref, vmem_ref,
                  q_out, k_out, vt_out, sg_out, om_out):
    x = x_ref[0]
    h = (x * _rsqrt_mean(x, D_MODEL) * ln_ref[0]).astype(BF16)

    lat = _dot(h, wlat_ref[0])
    q_lat = lat[:, :Q_LORA]
    kv_lat = lat[:, Q_LORA:Q_LORA + KV_LORA]
    kpe = lat[:, Q_LORA + KV_LORA:]

    ct = ct_ref[...]
    st = st_ref[...]

    qn = (q_lat * _rsqrt_mean(q_lat, Q_LORA) * qlg_ref[0]).astype(BF16)
    q_raw = _dot(qn, wuq_ref[0])
    q_sw = _dot(qn, wuqs_ref[0])
    q_c = ct * qg_ref[0, 0:1]
    q_s = st * qg_ref[0, 1:2]
    scale = QK_HEAD ** -0.5 * LOG2E
    for hd in range(MLA_HEADS):
        qh = q_raw[:, hd * LANE:(hd + 1) * LANE]
        r = _rsqrt_mean(qh, QK_HEAD)
        qo = (qh * q_c + q_sw[:, hd * LANE:(hd + 1) * LANE] * q_s) * (r * scale)
        q_out[0, hd] = qo.astype(BF16)

    kvn = (kv_lat * _rsqrt_mean(kv_lat, KV_LORA) * kvlg_ref[0]).astype(BF16)
    k_nope = _dot(kvn, wk_ref[0])
    vt_out[0, 0] = _dot_nt(wvt_ref[0], kvn).astype(BF16)
    k_cg = kg_ref[0, 0:1]
    kpe_rot = kpe * (ct * k_cg) + pltpu.roll(kpe, HALF_LANE, 1) * (st * kg_ref[0, 1:2])
    pe_sq = jnp.sum(kpe * kpe, axis=-1, keepdims=True)
    for hd in range(MLA_HEADS):
        kh = k_nope[:, hd * LANE:(hd + 1) * LANE]
        r = lax.rsqrt((jnp.sum(kh * kh, axis=-1, keepdims=True) + pe_sq) * (1.0 / QK_HEAD) + EPS)
        k_out[0, hd] = ((kh * k_cg + kpe_rot) * r).astype(BF16)

    sg = _silu(_dot(h, wgate_ref[0]))
    sg_out[0] = sg[:, :MLA_WIDTH]
    o_mem = _memory_attention(_dot(h, wqm_ref[0]), mqg_ref[0], kmem_ref, vmem_ref)
    om_out[0] = (o_mem * sg[:, MLA_WIDTH:]).astype(BF16)


def _const_spec(shape, layer):
    nd = len(shape)
    return pl.BlockSpec((1,) + tuple(shape[1:]), lambda b, i: (layer,) + (0,) * (nd - 1))


def _pre_a(x, ct, st, w, j, layer, kmem, vmem):
    B, S, _ = x.shape
    tm = ROW_TILE
    kv_per_chunk = ATTN_TK // tm
    weights = [w["ln_g"], w["a_wlat"], w["a_wqm"], w["a_wgate"], w["a_qlat_g"], w["a_kvlat_g"],
               w["a_wuq"], w["a_wuqs"], w["a_wk"], w["a_wvt"], w["a_qg"], w["a_kg"], w["mq_g"]]
    wlayer = [layer, j, j, j, j, j, j, j, j, j, j, j, layer]
    in_specs = [
        pl.BlockSpec((1, tm, D_MODEL), lambda b, i: (b, i, 0)),
        pl.BlockSpec((tm, LANE), lambda b, i: (i, 0)),
        pl.BlockSpec((tm, LANE), lambda b, i: (i, 0)),
    ] + [_const_spec(a.shape, l) for a, l in zip(weights, wlayer)] + _mem_specs(layer)
    out_specs = [
        pl.BlockSpec((1, MLA_HEADS, tm, LANE), lambda b, i: (b, 0, i, 0)),
        pl.BlockSpec((1, MLA_HEADS, tm, LANE), lambda b, i: (b, 0, i, 0)),
        pl.BlockSpec((1, 1, MLA_WIDTH, tm), lambda b, i: (b, i // kv_per_chunk, 0, i % kv_per_chunk)),
        pl.BlockSpec((1, tm, MLA_WIDTH), lambda b, i: (b, i, 0)),
        pl.BlockSpec((1, tm, MEM_WIDTH), lambda b, i: (b, i, 0)),
    ]
    out_shape = [
        jax.ShapeDtypeStruct((B, MLA_HEADS, S, LANE), BF16),
        jax.ShapeDtypeStruct((B, MLA_HEADS, S, LANE), BF16),
        jax.ShapeDtypeStruct((B, S // ATTN_TK, MLA_WIDTH, ATTN_TK), BF16),
        jax.ShapeDtypeStruct((B, S, MLA_WIDTH), F32),
        jax.ShapeDtypeStruct((B, S, MEM_WIDTH), BF16),
    ]
    return pl.pallas_call(
        _pre_a_kernel,
        grid=(B, S // tm),
        in_specs=in_specs,
        out_specs=out_specs,
        out_shape=out_shape,
        compiler_params=pltpu.CompilerParams(
            dimension_semantics=("arbitrary", "arbitrary"), vmem_limit_bytes=VMEM_LIMIT),
        name="pre_a",
    )(x, ct, st, *weights, kmem, vmem)


def _attn_bounded_kernel(m2_ref, q_ref, k_ref, vt_ref, sg_ref, o_ref, acc_ref, l_ref, *, seq):
    tq, tk = ATTN_TQ, ATTN_TK
    m2 = m2_ref[0]
    acc_ref[...] = jnp.zeros_like(acc_ref)
    l_ref[...] = jnp.zeros_like(l_ref)

    def body(j, carry):
        for hh in range(2):
            k = k_ref[0, hh, pl.ds(pl.multiple_of(j * tk, tk), tk), :]
            pt = jnp.exp2(_dot_nt(k, q_ref[0, hh]) - m2)
            l_ref[hh] += jnp.sum(pt.reshape(tk // 8, 8, tq), axis=0)
            acc_ref[hh] += _dot(vt_ref[0, j, hh * V_HEAD:(hh + 1) * V_HEAD, :], pt.astype(BF16))
        return carry

    lax.fori_loop(0, seq // tk, body, 0)
    ot = [acc_ref[hh] / jnp.sum(l_ref[hh], axis=0, keepdims=True) for hh in range(2)]
    o = jnp.concatenate(ot, axis=0).T
    o_ref[0] = (o * sg_ref[0]).astype(BF16)


def _attn_online_kernel(q_ref, k_ref, vt_ref, sg_ref, o_ref, *, seq):
    tq, tk = SAFE_TQ, ATTN_TK
    outs = []
    for hh in range(2):
        q = q_ref[0, hh]

        def body(j, carry):
            m, l, acc = carry
            k = k_ref[0, hh, pl.ds(pl.multiple_of(j * tk, tk), tk), :]
            s = _dot_nt(q, k)
            m_new = jnp.maximum(m, jnp.max(s, axis=-1, keepdims=True))
            alpha = jnp.exp2(m - m_new)
            p = jnp.exp2(s - m_new)
            l = alpha * l + jnp.sum(p, axis=-1, keepdims=True)
            acc = alpha * acc + _dot_nt(p.astype(BF16), vt_ref[0, j])
            return m_new, l, acc

        init = (jnp.full((tq, 1), -jnp.inf, F32), jnp.zeros((tq, 1), F32), jnp.zeros((tq, LANE), F32))
        m, l, acc = lax.fori_loop(0, seq // tk, body, init)
        outs.append(acc / l)
    o_ref[0] = (jnp.where(_low_half(outs[0].shape), outs[0], outs[1]) * sg_ref[0]).astype(BF16)


def _attention_call(kernel_fn, tq, q, k, vt, sg, extra_in, extra_specs, scratch, name):
    B, _, S, _ = q.shape
    return pl.pallas_call(
        functools.partial(kernel_fn, seq=S),
        grid=(B, MLA_HEADS // 2, S // tq),
        in_specs=extra_specs + [
            pl.BlockSpec((1, 2, tq, LANE), lambda b, hp, i: (b, hp, i, 0)),
            pl.BlockSpec((1, 2, S, LANE), lambda b, hp, i: (b, hp, 0, 0)),
            pl.BlockSpec((1, S // ATTN_TK, LANE, ATTN_TK), lambda b, hp, i: (b, 0, hp, 0)),
            pl.BlockSpec((1, tq, LANE), lambda b, hp, i: (b, i, hp)),
        ],
        out_specs=pl.BlockSpec((1, tq, LANE), lambda b, hp, i: (b, i, hp)),
        out_shape=jax.ShapeDtypeStruct((B, S, MLA_WIDTH), BF16),
        scratch_shapes=scratch,
        compiler_params=pltpu.CompilerParams(
            dimension_semantics=("arbitrary", "arbitrary", "arbitrary"), vmem_limit_bytes=VMEM_LIMIT),
        name=name,
    )(*extra_in, q, k, vt, sg)


def _attention(q, k, vt, sg, m2):
    def bounded(ops):
        scratch = [pltpu.VMEM((2, V_HEAD, ATTN_TQ), F32), pltpu.VMEM((2, 8, ATTN_TQ), F32)]
        return _attention_call(_attn_bounded_kernel, ATTN_TQ, *ops, [m2],
                               [pl.BlockSpec(memory_space=pltpu.SMEM)], scratch, "attn")

    def online(ops):
        return _attention_call(_attn_online_kernel, SAFE_TQ, *ops, [], [], [], "attn_online")

    return lax.cond(m2[0] <= MAX_FIXED_SHIFT, bounded, online, (q, k, vt, sg))


def _out_proj(x, o_mix, o_mem, wout_ref):
    return x + _dot(o_mix, wout_ref[0, :MLA_WIDTH, :]) + _dot(o_mem, wout_ref[0, MLA_WIDTH:, :])


def _post_kernel(x_ref, og_ref, omem_ref, wout_ref, y_ref):
    y_ref[0] = _out_proj(x_ref[0], og_ref[0], omem_ref[0], wout_ref)


def _post(x, o_gated, o_mem, w, layer):
    B, S, _ = x.shape
    tm = ROW_TILE
    return pl.pallas_call(
        _post_kernel,
        grid=(B, S // tm),
        in_specs=[
            pl.BlockSpec((1, tm, D_MODEL), lambda b, i: (b, i, 0)),
            pl.BlockSpec((1, tm, MLA_WIDTH), lambda b, i: (b, i, 0)),
            pl.BlockSpec((1, tm, MEM_WIDTH), lambda b, i: (b, i, 0)),
            _const_spec(w["wout"].shape, layer),
        ],
        out_specs=pl.BlockSpec((1, tm, D_MODEL), lambda b, i: (b, i, 0)),
        out_shape=jax.ShapeDtypeStruct((B, S, D_MODEL), F32),
        compiler_params=pltpu.CompilerParams(
            dimension_semantics=("arbitrary", "arbitrary"), vmem_limit_bytes=VMEM_LIMIT),
        name="post_a",
    )(x, o_gated, o_mem, w["wout"])


def _spatial_mix(vn, ws_ref, bias):
    rows = vn.shape[0]
    n_blk = SG_WIDTH // LANE
    masked = []
    for g in range(SG_GROUPS):
        lo, hi = g * SG_GDIM, (g + 1) * SG_GDIM
        b0, b1 = lo // LANE, (hi - 1) // LANE
        blk = vn[:, b0 * LANE:(b1 + 1) * LANE]
        col = lax.broadcasted_iota(jnp.int32, blk.shape, 1) + b0 * LANE
        masked.append((b0, b1, jnp.where(col >= lo, jnp.where(col < hi, blk, 0.0), 0.0).astype(BF16)))
    chunks = []
    for c in range(rows // CHUNK):
        parts = [[] for _ in range(n_blk)]
        for g, (b0, b1, mv) in enumerate(masked):
            res = _dot(ws_ref[0, g], mv[c * CHUNK:(c + 1) * CHUNK])
            for b in range(b0, b1 + 1):
                parts[b].append(res[:, (b - b0) * LANE:(b - b0 + 1) * LANE])
        chunks.append(jnp.concatenate([functools.reduce(lambda a, b: a + b, p) for p in parts], axis=1) + bias)
    return jnp.concatenate(chunks, axis=0)


def _layer_b_kernel(x_ref, ln_ref, win_ref, vg_ref, ws_ref, bias_ref, mqg_ref, kmem_ref, vmem_ref,
                    wout_ref, y_ref):
    x = x_ref[0]
    h = (x * _rsqrt_mean(x, D_MODEL) * ln_ref[0]).astype(BF16)
    u = _gelu_tanh(_dot(h, win_ref[0, :, :SG_WIDTH]))
    v = _gelu_tanh(_dot(h, win_ref[0, :, SG_WIDTH:2 * SG_WIDTH]))
    vn = v * _rsqrt_mean(v, SG_WIDTH) * vg_ref[0]
    sg = _silu(_dot(h, win_ref[0, :, 2 * SG_WIDTH + MEM_WIDTH:]))
    o_mix = (u * _spatial_mix(vn, ws_ref, bias_ref[0]) * sg[:, :SG_WIDTH]).astype(BF16)
    qm = _dot(h, win_ref[0, :, 2 * SG_WIDTH:2 * SG_WIDTH + MEM_WIDTH])
    o_mem = (_memory_attention(qm, mqg_ref[0], kmem_ref, vmem_ref) * sg[:, SG_WIDTH:]).astype(BF16)
    y_ref[0] = _out_proj(x, o_mix, o_mem, wout_ref)


def _layer_b(x, w, j, layer, kmem, vmem):
    B, S, _ = x.shape
    tm = ROW_TILE
    weights = [w["ln_g"], w["b_win"], w["b_vg"], w["b_ws"], w["b_bias"], w["mq_g"]]
    wlayer = [layer, j, j, j, j, layer]
    in_specs = [pl.BlockSpec((1, tm, D_MODEL), lambda b, i: (b, i, 0))]
    in_specs += [_const_spec(a.shape, l) for a, l in zip(weights, wlayer)]
    in_specs += _mem_specs(layer) + [_const_spec(w["wout"].shape, layer)]
    return pl.pallas_call(
        _layer_b_kernel,
        grid=(B, S // tm),
        in_specs=in_specs,
        out_specs=pl.BlockSpec((1, tm, D_MODEL), lambda b, i: (b, i, 0)),
        out_shape=jax.ShapeDtypeStruct((B, S, D_MODEL), F32),
        compiler_params=pltpu.CompilerParams(
            dimension_semantics=("arbitrary", "arbitrary"), vmem_limit_bytes=VMEM_LIMIT),
        name="layer_b",
    )(x, *weights, kmem, vmem, w["wout"])


def _zeros_like_last(a, n):
    return jnp.zeros(a.shape[:-1] + (n,), a.dtype)


def _head_lanes(nope, rope):
    ref = nope if nope is not None else rope
    z = lambda n: _zeros_like_last(ref, n)
    n0, n1 = (nope[..., :48], nope[..., 48:]) if nope is not None else (z(48), z(QK_NOPE - 48))
    r0, r1 = (rope[..., :HALF_ROPE], rope[..., HALF_ROPE:]) if rope is not None else (z(HALF_ROPE), z(HALF_ROPE))
    return jnp.concatenate([r0, n0, r1, n1, z(LANE - QK_HEAD)], axis=-1)


def _prep_weights(ln_g, w_in_a, a_qlat_g, a_kvlat_g, a_w_uq, a_w_ukv, a_q_norm_g, a_k_norm_g, w_in_b,
                  b_v_norm_g, b_w_s, b_bias, mem_norm_g, w_mem_kv, mem_q_norm_g, mem_k_norm_g, w_out):
    w = {}
    na = w_in_a.shape[0]
    w["ln_g"] = ln_g[:, None, :]
    w["mem_g"] = mem_norm_g[:, None, :]

    o2, o3, o4 = Q_LORA + KV_LORA, Q_LORA + KV_LORA + QK_ROPE, Q_LORA + KV_LORA + QK_ROPE + MEM_WIDTH
    w["a_wlat"] = jnp.concatenate([w_in_a[:, :, :o2], _head_lanes(None, w_in_a[:, :, o2:o3])], axis=-1).astype(BF16)
    w["a_wqm"] = w_in_a[:, :, o3:o4].astype(BF16)
    w["a_wgate"] = w_in_a[:, :, o4:].astype(BF16)
    w["a_qlat_g"] = a_qlat_g[:, None, :]
    w["a_kvlat_g"] = a_kvlat_g[:, None, :]
    uq = a_w_uq.reshape(na, Q_LORA, MLA_HEADS, QK_HEAD)
    w["a_wuq"] = _head_lanes(uq[..., :QK_NOPE], uq[..., QK_NOPE:]).reshape(na, Q_LORA, MLA_HEADS * LANE).astype(BF16)
    uq_partner = jnp.concatenate([uq[..., QK_NOPE + HALF_ROPE:], uq[..., QK_NOPE:QK_NOPE + HALF_ROPE]], axis=-1)
    w["a_wuqs"] = _head_lanes(None, uq_partner).reshape(na, Q_LORA, MLA_HEADS * LANE).astype(BF16)
    ukv = a_w_ukv.reshape(na, KV_LORA, MLA_HEADS, QK_NOPE + V_HEAD)
    w["a_wk"] = _head_lanes(ukv[..., :QK_NOPE], None).reshape(na, KV_LORA, MLA_HEADS * LANE).astype(BF16)
    w["a_wvt"] = jnp.swapaxes(ukv[..., QK_NOPE:].reshape(na, KV_LORA, MLA_WIDTH), 1, 2).astype(BF16)

    def norm_gain_rows(g):
        g_c = _head_lanes(g[:, :QK_NOPE], g[:, QK_NOPE:])
        return jnp.stack([g_c, jnp.roll(g_c, HALF_LANE, axis=-1)], axis=1)

    w["a_qg"] = norm_gain_rows(a_q_norm_g)
    w["a_kg"] = norm_gain_rows(a_k_norm_g)
    g_max = jnp.max(jnp.abs(a_q_norm_g), axis=1) * jnp.max(jnp.abs(a_k_norm_g), axis=1)
    w["a_m2"] = (QK_HEAD ** 0.5 * LOG2E * g_max)[:, None]

    w["b_win"] = w_in_b.astype(BF16)
    w["b_vg"] = b_v_norm_g[:, None, :]
    w["b_ws"] = b_w_s.astype(BF16)
    w["b_bias"] = jnp.repeat(jnp.swapaxes(b_bias, 1, 2), SG_GDIM, axis=2)

    w["mem_wk"] = w_mem_kv[:, :, :MEM_WIDTH].astype(BF16)
    w["mem_wv"] = w_mem_kv[:, :, MEM_WIDTH:].astype(BF16)
    w["mq_g"] = jnp.tile(mem_q_norm_g, (1, MEM_HEADS))[:, None, :]
    w["mk_g"] = jnp.tile(mem_k_norm_g, (1, MEM_HEADS))[:, None, :]
    w["wout"] = w_out.astype(BF16)
    return w


def _rope_tables(seq):
    inv = 1.0 / (ROPE_BASE ** (jnp.arange(0, QK_ROPE, 2, dtype=F32) / QK_ROPE))
    ang = jnp.arange(seq, dtype=F32)[:, None] * inv[None, :]
    cos, sin = jnp.cos(ang), jnp.sin(ang)
    ct = _head_lanes(jnp.ones((seq, QK_NOPE), F32), jnp.concatenate([cos, cos], axis=1))
    st = _head_lanes(None, jnp.concatenate([-sin, sin], axis=1))
    return ct, st


def _trunk(x, mem, w):
    S = x.shape[1]
    ct, st = _rope_tables(S)
    kmem, vmem = _mem_kv(mem, w["mem_g"], w["mem_wk"], w["mem_wv"], w["mk_g"])
    for layer in range(DEPTH):
        j = layer // 2
        if layer % 2 == 0:
            q, k, vt, sg, o_mem = _pre_a(x, ct, st, w, j, layer, kmem, vmem)
            o_gated = _attention(q, k, vt, sg, w["a_m2"][j])
            x = _post(x, o_gated, o_mem, w, layer)
        else:
            x = _layer_b(x, w, j, layer, kmem, vmem)
    return x


def kernel(x_prompt, x_sample, mem_prompt, mem_sample, ln_g, w_in_a, a_qlat_g, a_kvlat_g, a_w_uq, a_w_ukv,
           a_q_norm_g, a_k_norm_g, w_in_b, b_v_norm_g, b_w_s, b_bias, mem_norm_g, w_mem_kv, mem_q_norm_g,
           mem_k_norm_g, w_out):
    w = _prep_weights(ln_g, w_in_a, a_qlat_g, a_kvlat_g, a_w_uq, a_w_ukv, a_q_norm_g, a_k_norm_g, w_in_b,
                      b_v_norm_g, b_w_s, b_bias, mem_norm_g, w_mem_kv, mem_q_norm_g, mem_k_norm_g, w_out)
    return _trunk(x_prompt, mem_prompt, w), _trunk(x_sample, mem_sample, w)
```

```python
import functools
import math

import jax
import jax.numpy as jnp
from jax import lax
from jax.experimental import pallas as pl
from jax.experimental.pallas import tpu as pltpu

D_MODEL = 1024
DEPTH = 4
EPS = 1e-6
N_MEM = 256
MEM_HEADS = 4
MEM_HDIM = 64
MEM_WIDTH = MEM_HEADS * MEM_HDIM
MLA_HEADS = 12
QK_NOPE = 64
QK_ROPE = 32
QK_HEAD = QK_NOPE + QK_ROPE
V_HEAD = 64
Q_LORA = 384
KV_LORA = 256
MLA_WIDTH = MLA_HEADS * V_HEAD
ROPE_BASE = 10000.0
CHUNK = 128
SG_GROUPS = 8
SG_WIDTH = 768
SG_GDIM = SG_WIDTH // SG_GROUPS
BRANCH = MLA_WIDTH + MEM_WIDTH

LANE = 128
HALF_LANE = LANE // 2
HALF_ROPE = QK_ROPE // 2

ROW_TILE = 512
ATTN_TQ = 512
ATTN_TK = 2048
SAFE_TQ = 256
VMEM_LIMIT = 56 * 1024 * 1024
LOG2E = 1.4426950408889634
MAX_FIXED_SHIFT = 40.0

BF16 = jnp.bfloat16
F32 = jnp.float32


def _rsqrt_mean(x, n):
    return lax.rsqrt(jnp.sum(x * x, axis=-1, keepdims=True) * (1.0 / n) + EPS)


def _silu(g):
    return g / (1.0 + jnp.exp(-g))


def _gelu_tanh(x):
    c = math.sqrt(2.0 / math.pi)
    return 0.5 * x * (1.0 + jnp.tanh(c * (x + 0.044715 * (x * x * x))))


def _dot(a, b):
    return jnp.dot(a, b, preferred_element_type=F32)


def _dot_nt(a, b):
    return lax.dot_general(a, b, (((1,), (1,)), ((), ())), preferred_element_type=F32)


def _low_half(shape):
    return lax.broadcasted_iota(jnp.int32, shape, 1) < HALF_LANE


def _norm_heads64(x, g):
    out = []
    for a in range(MEM_WIDTH // LANE):
        blk = x[:, a * LANE:(a + 1) * LANE]
        lo = _low_half(blk.shape)
        sq = blk * blk
        s_lo = jnp.sum(jnp.where(lo, sq, 0.0), axis=-1, keepdims=True)
        s_hi = jnp.sum(jnp.where(lo, 0.0, sq), axis=-1, keepdims=True)
        r = jnp.where(lo, lax.rsqrt(s_lo * (1.0 / MEM_HDIM) + EPS), lax.rsqrt(s_hi * (1.0 / MEM_HDIM) + EPS))
        out.append(blk * r * g[:, a * LANE:(a + 1) * LANE])
    return out


def _mem_kv_kernel(mem_ref, g_ref, wk_ref, wv_ref, kg_ref, k_out, v_out):
    m = mem_ref[0]
    mn = (m * _rsqrt_mean(m, D_MODEL) * g_ref[0]).astype(BF16)
    kn = _norm_heads64(_dot(mn, wk_ref[0]), kg_ref[0])
    k_out[0, 0] = jnp.concatenate(kn, axis=1).astype(BF16)
    vv = _dot(mn, wv_ref[0])
    lane = lax.broadcasted_iota(jnp.int32, (N_MEM, MEM_WIDTH), 1)
    for h in range(MEM_HEADS):
        vh = jnp.where(lane >= h * MEM_HDIM, jnp.where(lane < (h + 1) * MEM_HDIM, vv, 0.0), 0.0)
        v_out[0, 0, h] = vh.astype(BF16)


def _mem_kv(mem, mem_g, wk, wv, kg):
    B = mem.shape[0]
    return pl.pallas_call(
        _mem_kv_kernel,
        grid=(DEPTH, B),
        in_specs=[
            pl.BlockSpec((1, N_MEM, D_MODEL), lambda l, b: (b, 0, 0)),
            pl.BlockSpec((1, 1, D_MODEL), lambda l, b: (l, 0, 0)),
            pl.BlockSpec((1, D_MODEL, MEM_WIDTH), lambda l, b: (l, 0, 0)),
            pl.BlockSpec((1, D_MODEL, MEM_WIDTH), lambda l, b: (l, 0, 0)),
            pl.BlockSpec((1, 1, MEM_WIDTH), lambda l, b: (l, 0, 0)),
        ],
        out_specs=[
            pl.BlockSpec((1, 1, N_MEM, MEM_WIDTH), lambda l, b: (l, b, 0, 0)),
            pl.BlockSpec((1, 1, MEM_HEADS, N_MEM, MEM_WIDTH), lambda l, b: (l, b, 0, 0, 0)),
        ],
        out_shape=[
            jax.ShapeDtypeStruct((DEPTH, B, N_MEM, MEM_WIDTH), BF16),
            jax.ShapeDtypeStruct((DEPTH, B, MEM_HEADS, N_MEM, MEM_WIDTH), BF16),
        ],
        compiler_params=pltpu.CompilerParams(
            dimension_semantics=("arbitrary", "arbitrary"), vmem_limit_bytes=VMEM_LIMIT),
        name="mem_kv",
    )(mem, mem_g, wk, wv, kg)


def _mem_specs(layer):
    return [
        pl.BlockSpec((1, 1, N_MEM, MEM_WIDTH), lambda b, i: (layer, b, 0, 0)),
        pl.BlockSpec((1, 1, MEM_HEADS, N_MEM, MEM_WIDTH), lambda b, i: (layer, b, 0, 0, 0)),
    ]


def _memory_attention(qm, qg, kmem_ref, vmem_ref):
    qn = _norm_heads64(qm, qg)
    o = None
    for h in range(MEM_HEADS):
        a, upper = divmod(h, 2)
        lo = _low_half(qn[a].shape)
        qh = jnp.where(lo, 0.0, qn[a]) if upper else jnp.where(lo, qn[a], 0.0)
        qh = (qh * (MEM_HDIM ** -0.5)).astype(BF16)
        s = _dot_nt(qh, kmem_ref[0, 0, :, a * LANE:(a + 1) * LANE])
        e = jnp.exp(s - jnp.max(s, axis=-1, keepdims=True))
        p = (e / jnp.sum(e, axis=-1, keepdims=True)).astype(BF16)
        oh = _dot(p, vmem_ref[0, 0, h])
        o = oh if o is None else o + oh
    return o


def _pre_a_kernel(x_ref, ct_ref, st_ref, ln_ref, wlat_ref, wqm_ref, wgate_ref, qlg_ref, kvlg_ref,
                  wuq_ref, wuqs_ref, wk_ref, wvt_ref, qg_ref, kg_ref, mqg_ref, kmem_ref, vmem_ref,
                  q_out, k_out, vt_out, sg_out, om_out):
    x = x_ref[0]
    h = (x * _rsqrt_mean(x, D_MODEL) * ln_ref[0]).astype(BF16)

    lat = _dot(h, wlat_ref[0])
    q_lat = lat[:, :Q_LORA]
    kv_lat = lat[:, Q_LORA:Q_LORA + KV_LORA]
    kpe = lat[:, Q_LORA + KV_LORA:]

    ct = ct_ref[...]
    st = st_ref[...]

    qn = (q_lat * _rsqrt_mean(q_lat, Q_LORA) * qlg_ref[0]).astype(BF16)
    q_raw = _dot(qn, wuq_ref[0])
    q_sw = _dot(qn, wuqs_ref[0])
    q_c = ct * qg_ref[0, 0:1]
    q_s = st * qg_ref[0, 1:2]
    scale = QK_HEAD ** -0.5 * LOG2E
    for hd in range(MLA_HEADS):
        qh = q_raw[:, hd * LANE:(hd + 1) * LANE]
        r = _rsqrt_mean(qh, QK_HEAD)
        qo = (qh * q_c + q_sw[:, hd * LANE:(hd + 1) * LANE] * q_s) * (r * scale)
        q_out[0, hd] = qo.astype(BF16)

    kvn = (kv_lat * _rsqrt_mean(kv_lat, KV_LORA) * kvlg_ref[0]).astype(BF16)
    k_nope = _dot(kvn, wk_ref[0])
    vt_out[0, 0] = _dot_nt(wvt_ref[0], kvn).astype(BF16)
    k_cg = kg_ref[0, 0:1]
    kpe_rot = kpe * (ct * k_cg) + pltpu.roll(kpe, HALF_LANE, 1) * (st * kg_ref[0, 1:2])
    pe_sq = jnp.sum(kpe * kpe, axis=-1, keepdims=True)
    for hd in range(MLA_HEADS):
        kh = k_nope[:, hd * LANE:(hd + 1) * LANE]
        r = lax.rsqrt((jnp.sum(kh * kh, axis=-1, keepdims=True) + pe_sq) * (1.0 / QK_HEAD) + EPS)
        k_out[0, hd] = ((kh * k_cg + kpe_rot) * r).astype(BF16)

    sg = _silu(_dot(h, wgate_ref[0]))
    sg_out[0] = sg[:, :MLA_WIDTH]
    o_mem = _memory_attention(_dot(h, wqm_ref[0]), mqg_ref[0], kmem_ref, vmem_ref)
    om_out[0] = (o_mem * sg[:, MLA_WIDTH:]).astype(BF16)


def _const_spec(shape, layer):
    nd = len(shape)
    return pl.BlockSpec((1,) + tuple(shape[1:]), lambda b, i: (layer,) + (0,) * (nd - 1))


def _pre_a(x, ct, st, w, j, layer, kmem, vmem):
    B, S, _ = x.shape
    tm = ROW_TILE
    kv_per_chunk = ATTN_TK // tm
    weights = [w["ln_g"], w["a_wlat"], w["a_wqm"], w["a_wgate"], w["a_qlat_g"], w["a_kvlat_g"],
               w["a_wuq"], w["a_wuqs"], w["a_wk"], w["a_wvt"], w["a_qg"], w["a_kg"], w["mq_g"]]
    wlayer = [layer, j, j, j, j, j, j, j, j, j, j, j, layer]
    in_specs = [
        pl.BlockSpec((1, tm, D_MODEL), lambda b, i: (b, i, 0)),
        pl.BlockSpec((tm, LANE), lambda b, i: (i, 0)),
        pl.BlockSpec((tm, LANE), lambda b, i: (i, 0)),
    ] + [_const_spec(a.shape, l) for a, l in zip(weights, wlayer)] + _mem_specs(layer)
    out_specs = [
        pl.BlockSpec((1, MLA_HEADS, tm, LANE), lambda b, i: (b, 0, i, 0)),
        pl.BlockSpec((1, MLA_HEADS, tm, LANE), lambda b, i: (b, 0, i, 0)),
        pl.BlockSpec((1, 1, MLA_WIDTH, tm), lambda b, i: (b, i // kv_per_chunk, 0, i % kv_per_chunk)),
        pl.BlockSpec((1, tm, MLA_WIDTH), lambda b, i: (b, i, 0)),
        pl.BlockSpec((1, tm, MEM_WIDTH), lambda b, i: (b, i, 0)),
    ]
    out_shape = [
        jax.ShapeDtypeStruct((B, MLA_HEADS, S, LANE), BF16),
        jax.ShapeDtypeStruct((B, MLA_HEADS, S, LANE), BF16),
        jax.ShapeDtypeStruct((B, S // ATTN_TK, MLA_WIDTH, ATTN_TK), BF16),
        jax.ShapeDtypeStruct((B, S, MLA_WIDTH), F32),
        jax.ShapeDtypeStruct((B, S, MEM_WIDTH), BF16),
    ]
    return pl.pallas_call(
        _pre_a_kernel,
        grid=(B, S // tm),
        in_specs=in_specs,
        out_specs=out_specs,
        out_shape=out_shape,
        compiler_params=pltpu.CompilerParams(
            dimension_semantics=("arbitrary", "arbitrary"), vmem_limit_bytes=VMEM_LIMIT),
        name="pre_a",
    )(x, ct, st, *weights, kmem, vmem)


def _attn_bounded_kernel(m2_ref, q_ref, k_ref, vt_ref, sg_ref, o_ref, acc_ref, l_ref, *, seq):
    tq, tk = ATTN_TQ, ATTN_TK
    m2 = m2_ref[0]
    acc_ref[...] = jnp.zeros_like(acc_ref)
    l_ref[...] = jnp.zeros_like(l_ref)

    def body(j, carry):
        for hh in range(2):
            k = k_ref[0, hh, pl.ds(pl.multiple_of(j * tk, tk), tk), :]
            pt = jnp.exp2(_dot_nt(k, q_ref[0, hh]) - m2)
            l_ref[hh] += jnp.sum(pt.reshape(tk // 8, 8, tq), axis=0)
            acc_ref[hh] += _dot(vt_ref[0, j, hh * V_HEAD:(hh + 1) * V_HEAD, :], pt.astype(BF16))
        return carry

    lax.fori_loop(0, seq // tk, body, 0)
    ot = [acc_ref[hh] / jnp.sum(l_ref[hh], axis=0, keepdims=True) for hh in range(2)]
    o = jnp.concatenate(ot, axis=0).T
    o_ref[0] = (o * sg_ref[0]).astype(BF16)


def _attn_online_kernel(q_ref, k_ref, vt_ref, sg_ref, o_ref, *, seq):
    tq, tk = SAFE_TQ, ATTN_TK
    outs = []
    for hh in range(2):
        q = q_ref[0, hh]

        def body(j, carry):
            m, l, acc = carry
            k = k_ref[0, hh, pl.ds(pl.multiple_of(j * tk, tk), tk), :]
            s = _dot_nt(q, k)
            m_new = jnp.maximum(m, jnp.max(s, axis=-1, keepdims=True))
            alpha = jnp.exp2(m - m_new)
            p = jnp.exp2(s - m_new)
            l = alpha * l + jnp.sum(p, axis=-1, keepdims=True)
            acc = alpha * acc + _dot_nt(p.astype(BF16), vt_ref[0, j])
            return m_new, l, acc

        init = (jnp.full((tq, 1), -jnp.inf, F32), jnp.zeros((tq, 1), F32), jnp.zeros((tq, LANE), F32))
        m, l, acc = lax.fori_loop(0, seq // tk, body, init)
        outs.append(acc / l)
    o_ref[0] = (jnp.where(_low_half(outs[0].shape), outs[0], outs[1]) * sg_ref[0]).astype(BF16)


def _attention_call(kernel_fn, tq, q, k, vt, sg, extra_in, extra_specs, scratch, name):
    B, _, S, _ = q.shape
    return pl.pallas_call(
        functools.partial(kernel_fn, seq=S),
        grid=(B, MLA_HEADS // 2, S // tq),
        in_specs=extra_specs + [
            pl.BlockSpec((1, 2, tq, LANE), lambda b, hp, i: (b, hp, i, 0)),
            pl.BlockSpec((1, 2, S, LANE), lambda b, hp, i: (b, hp, 0, 0)),
            pl.BlockSpec((1, S // ATTN_TK, LANE, ATTN_TK), lambda b, hp, i: (b, 0, hp, 0)),
            pl.BlockSpec((1, tq, LANE), lambda b, hp, i: (b, i, hp)),
        ],
        out_specs=pl.BlockSpec((1, tq, LANE), lambda b, hp, i: (b, i, hp)),
        out_shape=jax.ShapeDtypeStruct((B, S, MLA_WIDTH), BF16),
        scratch_shapes=scratch,
        compiler_params=pltpu.CompilerParams(
            dimension_semantics=("arbitrary", "arbitrary", "arbitrary"), vmem_limit_bytes=VMEM_LIMIT),
        name=name,
    )(*extra_in, q, k, vt, sg)


def _attention(q, k, vt, sg, m2):
    def bounded(ops):
        scratch = [pltpu.VMEM((2, V_HEAD, ATTN_TQ), F32), pltpu.VMEM((2, 8, ATTN_TQ), F32)]
        return _attention_call(_attn_bounded_kernel, ATTN_TQ, *ops, [m2],
                               [pl.BlockSpec(memory_space=pltpu.SMEM)], scratch, "attn")

    def online(ops):
        return _attention_call(_attn_online_kernel, SAFE_TQ, *ops, [], [], [], "attn_online")

    return lax.cond(m2[0] <= MAX_FIXED_SHIFT, bounded, online, (q, k, vt, sg))


def _out_proj(x, o_mix, o_mem, wout_ref):
    return x + _dot(o_mix, wout_ref[0, :MLA_WIDTH, :]) + _dot(o_mem, wout_ref[0, MLA_WIDTH:, :])


def _spatial_mix(vn, ws_ref, bias):
    rows = vn.shape[0]
    n_blk = SG_WIDTH // LANE
    masked = []
    for g in range(SG_GROUPS):
        lo, hi = g * SG_GDIM, (g + 1) * SG_GDIM
        b0, b1 = lo // LANE, (hi - 1) // LANE
        blk = vn[:, b0 * LANE:(b1 + 1) * LANE]
        col = lax.broadcasted_iota(jnp.int32, blk.shape, 1) + b0 * LANE
        masked.append((b0, b1, jnp.where(col >= lo, jnp.where(col < hi, blk, 0.0), 0.0).astype(BF16)))
    chunks = []
    for c in range(rows // CHUNK):
        parts = [[] for _ in range(n_blk)]
        for g, (b0, b1, mv) in enumerate(masked):
            res = _dot(ws_ref[0, g], mv[c * CHUNK:(c + 1) * CHUNK])
            for b in range(b0, b1 + 1):
                parts[b].append(res[:, (b - b0) * LANE:(b - b0 + 1) * LANE])
        chunks.append(jnp.concatenate([functools.reduce(lambda a, b: a + b, p) for p in parts], axis=1) + bias)
    return jnp.concatenate(chunks, axis=0)


def _post_b_kernel(x_ref, og_ref, omem_ref, wout_a_ref, ln_ref, win_ref, vg_ref, ws_ref, bias_ref, mqg_ref,
                   kmem_ref, vmem_ref, wout_ref, y_ref):
    y_ref[0] = _out_proj(x_ref[0], og_ref[0], omem_ref[0], wout_a_ref)
    x = y_ref[0]
    h = (x * _rsqrt_mean(x, D_MODEL) * ln_ref[0]).astype(BF16)
    u = _gelu_tanh(_dot(h, win_ref[0, :, :SG_WIDTH]))
    v = _gelu_tanh(_dot(h, win_ref[0, :, SG_WIDTH:2 * SG_WIDTH]))
    vn = v * _rsqrt_mean(v, SG_WIDTH) * vg_ref[0]
    sg = _silu(_dot(h, win_ref[0, :, 2 * SG_WIDTH + MEM_WIDTH:]))
    o_mix = (u * _spatial_mix(vn, ws_ref, bias_ref[0]) * sg[:, :SG_WIDTH]).astype(BF16)
    qm = _dot(h, win_ref[0, :, 2 * SG_WIDTH:2 * SG_WIDTH + MEM_WIDTH])
    o_mem = (_memory_attention(qm, mqg_ref[0], kmem_ref, vmem_ref) * sg[:, SG_WIDTH:]).astype(BF16)
    y_ref[0] = _out_proj(x, o_mix, o_mem, wout_ref)


def _post_b(x, o_gated, o_mem, w, j, layer, kmem, vmem):
    B, S, _ = x.shape
    tm = ROW_TILE
    weights = [w["wout_a"], w["ln_g"], w["b_win"], w["b_vg"], w["b_ws"], w["b_bias"], w["mq_g"]]
    wlayer = [j, layer, j, j, j, j, layer]
    in_specs = [
        pl.BlockSpec((1, tm, D_MODEL), lambda b, i: (b, i, 0)),
        pl.BlockSpec((1, tm, MLA_WIDTH), lambda b, i: (b, i, 0)),
        pl.BlockSpec((1, tm, MEM_WIDTH), lambda b, i: (b, i, 0)),
    ]
    in_specs += [_const_spec(a.shape, l) for a, l in zip(weights, wlayer)]
    in_specs += _mem_specs(layer) + [_const_spec(w["wout_b"].shape, j)]
    return pl.pallas_call(
        _post_b_kernel,
        grid=(B, S // tm),
        in_specs=in_specs,
        out_specs=pl.BlockSpec((1, tm, D_MODEL), lambda b, i: (b, i, 0)),
        out_shape=jax.ShapeDtypeStruct((B, S, D_MODEL), F32),
        compiler_params=pltpu.CompilerParams(
            dimension_semantics=("arbitrary", "arbitrary"), vmem_limit_bytes=VMEM_LIMIT),
        name="post_b",
    )(x, o_gated, o_mem, *weights, kmem, vmem, w["wout_b"])


def _zeros_like_last(a, n):
    return jnp.zeros(a.shape[:-1] + (n,), a.dtype)


def _head_lanes(nope, rope):
    ref = nope if nope is not None else rope
    z = lambda n: _zeros_like_last(ref, n)
    n0, n1 = (nope[..., :48], nope[..., 48:]) if nope is not None else (z(48), z(QK_NOPE - 48))
    r0, r1 = (rope[..., :HALF_ROPE], rope[..., HALF_ROPE:]) if rope is not None else (z(HALF_ROPE), z(HALF_ROPE))
    return jnp.concatenate([r0, n0, r1, n1, z(LANE - QK_HEAD)], axis=-1)


def _prep_weights(ln_g, w_in_a, a_qlat_g, a_kvlat_g, a_w_uq, a_w_ukv, a_q_norm_g, a_k_norm_g, w_in_b,
                  b_v_norm_g, b_w_s, b_bias, mem_norm_g, w_mem_kv, mem_q_norm_g, mem_k_norm_g, w_out):
    w = {}
    na = w_in_a.shape[0]
    w["ln_g"] = ln_g[:, None, :]
    w["mem_g"] = mem_norm_g[:, None, :]

    o2, o3, o4 = Q_LORA + KV_LORA, Q_LORA + KV_LORA + QK_ROPE, Q_LORA + KV_LORA + QK_ROPE + MEM_WIDTH
    w["a_wlat"] = jnp.concatenate([w_in_a[:, :, :o2], _head_lanes(None, w_in_a[:, :, o2:o3])], axis=-1).astype(BF16)
    w["a_wqm"] = w_in_a[:, :, o3:o4].astype(BF16)
    w["a_wgate"] = w_in_a[:, :, o4:].astype(BF16)
    w["a_qlat_g"] = a_qlat_g[:, None, :]
    w["a_kvlat_g"] = a_kvlat_g[:, None, :]
    uq = a_w_uq.reshape(na, Q_LORA, MLA_HEADS, QK_HEAD)
    w["a_wuq"] = _head_lanes(uq[..., :QK_NOPE], uq[..., QK_NOPE:]).reshape(na, Q_LORA, MLA_HEADS * LANE).astype(BF16)
    uq_partner = jnp.concatenate([uq[..., QK_NOPE + HALF_ROPE:], uq[..., QK_NOPE:QK_NOPE + HALF_ROPE]], axis=-1)
    w["a_wuqs"] = _head_lanes(None, uq_partner).reshape(na, Q_LORA, MLA_HEADS * LANE).astype(BF16)
    ukv = a_w_ukv.reshape(na, KV_LORA, MLA_HEADS, QK_NOPE + V_HEAD)
    w["a_wk"] = _head_lanes(ukv[..., :QK_NOPE], None).reshape(na, KV_LORA, MLA_HEADS * LANE).astype(BF16)
    w["a_wvt"] = jnp.swapaxes(ukv[..., QK_NOPE:].reshape(na, KV_LORA, MLA_WIDTH), 1, 2).astype(BF16)

    def norm_gain_rows(g):
        g_c = _head_lanes(g[:, :QK_NOPE], g[:, QK_NOPE:])
        return jnp.stack([g_c, jnp.roll(g_c, HALF_LANE, axis=-1)], axis=1)

    w["a_qg"] = norm_gain_rows(a_q_norm_g)
    w["a_kg"] = norm_gain_rows(a_k_norm_g)
    g_max = jnp.max(jnp.abs(a_q_norm_g), axis=1) * jnp.max(jnp.abs(a_k_norm_g), axis=1)
    w["a_m2"] = (QK_HEAD ** 0.5 * LOG2E * g_max)[:, None]

    w["b_win"] = w_in_b.astype(BF16)
    w["b_vg"] = b_v_norm_g[:, None, :]
    w["b_ws"] = b_w_s.astype(BF16)
    w["b_bias"] = jnp.repeat(jnp.swapaxes(b_bias, 1, 2), SG_GDIM, axis=2)

    w["mem_wk"] = w_mem_kv[:, :, :MEM_WIDTH].astype(BF16)
    w["mem_wv"] = w_mem_kv[:, :, MEM_WIDTH:].astype(BF16)
    w["mq_g"] = jnp.tile(mem_q_norm_g, (1, MEM_HEADS))[:, None, :]
    w["mk_g"] = jnp.tile(mem_k_norm_g, (1, MEM_HEADS))[:, None, :]
    w["wout_a"] = w_out[0::2].astype(BF16)
    w["wout_b"] = w_out[1::2].astype(BF16)
    return w


def _rope_tables(seq):
    inv = 1.0 / (ROPE_BASE ** (jnp.arange(0, QK_ROPE, 2, dtype=F32) / QK_ROPE))
    ang = jnp.arange(seq, dtype=F32)[:, None] * inv[None, :]
    cos, sin = jnp.cos(ang), jnp.sin(ang)
    ct = _head_lanes(jnp.ones((seq, QK_NOPE), F32), jnp.concatenate([cos, cos], axis=1))
    st = _head_lanes(None, jnp.concatenate([-sin, sin], axis=1))
    return ct, st


def _trunk(x, mem, w):
    S = x.shape[1]
    ct, st = _rope_tables(S)
    kmem, vmem = _mem_kv(mem, w["mem_g"], w["mem_wk"], w["mem_wv"], w["mk_g"])
    for j in range(DEPTH // 2):
        q, k, vt, sg, o_mem = _pre_a(x, ct, st, w, j, 2 * j, kmem, vmem)
        o_gated = _attention(q, k, vt, sg, w["a_m2"][j])
        x = _post_b(x, o_gated, o_mem, w, j, 2 * j + 1, kmem, vmem)
    return x


def kernel(x_prompt, x_sample, mem_prompt, mem_sample, ln_g, w_in_a, a_qlat_g, a_kvlat_g, a_w_uq, a_w_ukv,
           a_q_norm_g, a_k_norm_g, w_in_b, b_v_norm_g, b_w_s, b_bias, mem_norm_g, w_mem_kv, mem_q_norm_g,
           mem_k_norm_g, w_out):
    w = _prep_weights(ln_g, w_in_a, a_qlat_g, a_kvlat_g, a_w_uq, a_w_ukv, a_q_norm_g, a_k_norm_g, w_in_b,
                      b_v_norm_g, b_w_s, b_bias, mem_norm_g, w_mem_kv, mem_q_norm_g, mem_k_norm_g, w_out)
    return _trunk(x_prompt, mem_prompt, w), _trunk(x_sample, mem_sample, w)
```

```python
import functools
import math

import jax
import jax.numpy as jnp
from jax import lax
from jax.experimental import pallas as pl
from jax.experimental.pallas import tpu as pltpu

D_MODEL = 1024
DEPTH = 4
EPS = 1e-6
N_MEM = 256
MEM_HEADS = 4
MEM_HDIM = 64
MEM_WIDTH = MEM_HEADS * MEM_HDIM
MLA_HEADS = 12
QK_NOPE = 64
QK_ROPE = 32
QK_HEAD = QK_NOPE + QK_ROPE
V_HEAD = 64
Q_LORA = 384
KV_LORA = 256
MLA_WIDTH = MLA_HEADS * V_HEAD
ROPE_BASE = 10000.0
CHUNK = 128
SG_GROUPS = 8
SG_WIDTH = 768
SG_GDIM = SG_WIDTH // SG_GROUPS
BRANCH = MLA_WIDTH + MEM_WIDTH

LANE = 128
HALF_LANE = LANE // 2
HALF_ROPE = QK_ROPE // 2

ROW_TILE = 512
ATTN_TQ = 512
ATTN_TK = 2048
SAFE_TQ = 256
VMEM_LIMIT = 56 * 1024 * 1024
LOG2E = 1.4426950408889634
MAX_FIXED_SHIFT = 40.0

BF16 = jnp.bfloat16
F32 = jnp.float32


def _rsqrt_mean(x, n):
    return lax.rsqrt(jnp.sum(x * x, axis=-1, keepdims=True) * (1.0 / n) + EPS)


def _silu(g):
    return g / (1.0 + jnp.exp(-g))


def _gelu_tanh(x):
    c = math.sqrt(2.0 / math.pi)
    return 0.5 * x * (1.0 + jnp.tanh(c * (x + 0.044715 * (x * x * x))))


def _dot(a, b):
    return jnp.dot(a, b, preferred_element_type=F32)


def _dot_nt(a, b):
    return lax.dot_general(a, b, (((1,), (1,)), ((), ())), preferred_element_type=F32)


def _low_half(shape):
    return lax.broadcasted_iota(jnp.int32, shape, 1) < HALF_LANE


def _norm_heads64(x, g):
    out = []
    for a in range(MEM_WIDTH // LANE):
        blk = x[:, a * LANE:(a + 1) * LANE]
        lo = _low_half(blk.shape)
        sq = blk * blk
        s_lo = jnp.sum(jnp.where(lo, sq, 0.0), axis=-1, keepdims=True)
        s_hi = jnp.sum(jnp.where(lo, 0.0, sq), axis=-1, keepdims=True)
        r = jnp.where(lo, lax.rsqrt(s_lo * (1.0 / MEM_HDIM) + EPS), lax.rsqrt(s_hi * (1.0 / MEM_HDIM) + EPS))
        out.append(blk * r * g[:, a * LANE:(a + 1) * LANE])
    return out


def _mem_kv_kernel(mem_ref, g_ref, wk_ref, wv_ref, kg_ref, k_out, v_out):
    m = mem_ref[0]
    mn = (m * _rsqrt_mean(m, D_MODEL) * g_ref[0]).astype(BF16)
    kn = _norm_heads64(_dot(mn, wk_ref[0]), kg_ref[0])
    k_out[0, 0] = jnp.concatenate(kn, axis=1).astype(BF16)
    vv = _dot(mn, wv_ref[0])
    lane = lax.broadcasted_iota(jnp.int32, (N_MEM, MEM_WIDTH), 1)
    for h in range(MEM_HEADS):
        vh = jnp.where(lane >= h * MEM_HDIM, jnp.where(lane < (h + 1) * MEM_HDIM, vv, 0.0), 0.0)
        v_out[0, 0, h] = vh.astype(BF16)


def _mem_kv(mem, mem_g, wk, wv, kg):
    B = mem.shape[0]
    return pl.pallas_call(
        _mem_kv_kernel,
        grid=(DEPTH, B),
        in_specs=[
            pl.BlockSpec((1, N_MEM, D_MODEL), lambda l, b: (b, 0, 0)),
            pl.BlockSpec((1, 1, D_MODEL), lambda l, b: (l, 0, 0)),
            pl.BlockSpec((1, D_MODEL, MEM_WIDTH), lambda l, b: (l, 0, 0)),
            pl.BlockSpec((1, D_MODEL, MEM_WIDTH), lambda l, b: (l, 0, 0)),
            pl.BlockSpec((1, 1, MEM_WIDTH), lambda l, b: (l, 0, 0)),
        ],
        out_specs=[
            pl.BlockSpec((1, 1, N_MEM, MEM_WIDTH), lambda l, b: (l, b, 0, 0)),
            pl.BlockSpec((1, 1, MEM_HEADS, N_MEM, MEM_WIDTH), lambda l, b: (l, b, 0, 0, 0)),
        ],
        out_shape=[
            jax.ShapeDtypeStruct((DEPTH, B, N_MEM, MEM_WIDTH), BF16),
            jax.ShapeDtypeStruct((DEPTH, B, MEM_HEADS, N_MEM, MEM_WIDTH), BF16),
        ],
        compiler_params=pltpu.CompilerParams(
            dimension_semantics=("arbitrary", "arbitrary"), vmem_limit_bytes=VMEM_LIMIT),
        name="mem_kv",
    )(mem, mem_g, wk, wv, kg)


def _mem_specs(layer):
    return [
        pl.BlockSpec((1, 1, N_MEM, MEM_WIDTH), lambda b, i: (layer, b, 0, 0)),
        pl.BlockSpec((1, 1, MEM_HEADS, N_MEM, MEM_WIDTH), lambda b, i: (layer, b, 0, 0, 0)),
    ]


def _memory_attention(qm, qg, kmem_ref, vmem_ref):
    qn = _norm_heads64(qm, qg)
    o = None
    for h in range(MEM_HEADS):
        a, upper = divmod(h, 2)
        lo = _low_half(qn[a].shape)
        qh = jnp.where(lo, 0.0, qn[a]) if upper else jnp.where(lo, qn[a], 0.0)
        qh = (qh * (MEM_HDIM ** -0.5)).astype(BF16)
        s = _dot_nt(qh, kmem_ref[0, 0, :, a * LANE:(a + 1) * LANE])
        e = jnp.exp(s - jnp.max(s, axis=-1, keepdims=True))
        p = (e / jnp.sum(e, axis=-1, keepdims=True)).astype(BF16)
        oh = _dot(p, vmem_ref[0, 0, h])
        o = oh if o is None else o + oh
    return o


def _pre_a_kernel(x_ref, ct_ref, st_ref, ln_ref, wlat_ref, wqm_ref, wgate_ref, qlg_ref, kvlg_ref,
                  wuq_ref, wuqs_ref, wk_ref, wvt_ref, qg_ref, kg_ref, mqg_ref, kmem_ref, vmem_ref,
                  q_out, k_out, vt_out, sg_out, om_out):
    x = x_ref[0]
    h = (x * _rsqrt_mean(x, D_MODEL) * ln_ref[0]).astype(BF16)

    lat = _dot(h, wlat_ref[0])
    q_lat = lat[:, :Q_LORA]
    kv_lat = lat[:, Q_LORA:Q_LORA + KV_LORA]
    kpe = lat[:, Q_LORA + KV_LORA:]

    ct = ct_ref[...]
    st = st_ref[...]

    qn = (q_lat * _rsqrt_mean(q_lat, Q_LORA) * qlg_ref[0]).astype(BF16)
    q_raw = _dot(qn, wuq_ref[0])
    q_sw = _dot(qn, wuqs_ref[0])
    q_c = ct * qg_ref[0, 0:1]
    q_s = st * qg_ref[0, 1:2]
    scale = QK_HEAD ** -0.5 * LOG2E
    for hd in range(MLA_HEADS):
        qh = q_raw[:, hd * LANE:(hd + 1) * LANE]
        r = _rsqrt_mean(qh, QK_HEAD)
        qo = (qh * q_c + q_sw[:, hd * LANE:(hd + 1) * LANE] * q_s) * (r * scale)
        q_out[0, hd] = qo.astype(BF16)

    kvn = (kv_lat * _rsqrt_mean(kv_lat, KV_LORA) * kvlg_ref[0]).astype(BF16)
    k_nope = _dot(kvn, wk_ref[0])
    vt_out[0, 0] = _dot_nt(wvt_ref[0], kvn).astype(BF16)
    k_cg = kg_ref[0, 0:1]
    kpe_rot = kpe * (ct * k_cg) + pltpu.roll(kpe, HALF_LANE, 1) * (st * kg_ref[0, 1:2])
    pe_sq = jnp.sum(kpe * kpe, axis=-1, keepdims=True)
    for hd in range(MLA_HEADS):
        kh = k_nope[:, hd * LANE:(hd + 1) * LANE]
        r = lax.rsqrt((jnp.sum(kh * kh, axis=-1, keepdims=True) + pe_sq) * (1.0 / QK_HEAD) + EPS)
        k_out[0, hd] = ((kh * k_cg + kpe_rot) * r).astype(BF16)

    sg = _silu(_dot(h, wgate_ref[0]))
    sg_out[0] = sg[:, :MLA_WIDTH]
    o_mem = _memory_attention(_dot(h, wqm_ref[0]), mqg_ref[0], kmem_ref, vmem_ref)
    om_out[0] = (o_mem * sg[:, MLA_WIDTH:]).astype(BF16)


def _const_spec(shape, layer):
    nd = len(shape)
    return pl.BlockSpec((1,) + tuple(shape[1:]), lambda b, i: (layer,) + (0,) * (nd - 1))


def _pre_a(x, ct, st, w, j, layer, kmem, vmem):
    B, S, _ = x.shape
    tm = ROW_TILE
    kv_per_chunk = ATTN_TK // tm
    weights = [w["ln_g"], w["a_wlat"], w["a_wqm"], w["a_wgate"], w["a_qlat_g"], w["a_kvlat_g"],
               w["a_wuq"], w["a_wuqs"], w["a_wk"], w["a_wvt"], w["a_qg"], w["a_kg"], w["mq_g"]]
    wlayer = [layer, j, j, j, j, j, j, j, j, j, j, j, layer]
    in_specs = [
        pl.BlockSpec((1, tm, D_MODEL), lambda b, i: (b, i, 0)),
        pl.BlockSpec((tm, LANE), lambda b, i: (i, 0)),
        pl.BlockSpec((tm, LANE), lambda b, i: (i, 0)),
    ] + [_const_spec(a.shape, l) for a, l in zip(weights, wlayer)] + _mem_specs(layer)
    out_specs = [
        pl.BlockSpec((1, MLA_HEADS, tm, LANE), lambda b, i: (b, 0, i, 0)),
        pl.BlockSpec((1, MLA_HEADS, tm, LANE), lambda b, i: (b, 0, i, 0)),
        pl.BlockSpec((1, 1, MLA_WIDTH, tm), lambda b, i: (b, i // kv_per_chunk, 0, i % kv_per_chunk)),
        pl.BlockSpec((1, tm, MLA_WIDTH), lambda b, i: (b, i, 0)),
        pl.BlockSpec((1, tm, MEM_WIDTH), lambda b, i: (b, i, 0)),
    ]
    out_shape = [
        jax.ShapeDtypeStruct((B, MLA_HEADS, S, LANE), BF16),
        jax.ShapeDtypeStruct((B, MLA_HEADS, S, LANE), BF16),
        jax.ShapeDtypeStruct((B, S // ATTN_TK, MLA_WIDTH, ATTN_TK), BF16),
        jax.ShapeDtypeStruct((B, S, MLA_WIDTH), F32),
        jax.ShapeDtypeStruct((B, S, MEM_WIDTH), BF16),
    ]
    return pl.pallas_call(
        _pre_a_kernel,
        grid=(B, S // tm),
        in_specs=in_specs,
        out_specs=out_specs,
        out_shape=out_shape,
        compiler_params=pltpu.CompilerParams(
            dimension_semantics=("arbitrary", "arbitrary"), vmem_limit_bytes=VMEM_LIMIT),
        name="pre_a",
    )(x, ct, st, *weights, kmem, vmem)


def _attn_bounded_kernel(m2_ref, q_ref, k_ref, vt_ref, sg_ref, o_ref, acc_ref, l_ref, *, seq):
    tq, tk = ATTN_TQ, ATTN_TK
    m2 = m2_ref[0]
    acc_ref[...] = jnp.zeros_like(acc_ref)
    l_ref[...] = jnp.zeros_like(l_ref)

    def body(j, carry):
        for hh in range(2):
            k = k_ref[0, hh, pl.ds(pl.multiple_of(j * tk, tk), tk), :]
            pt = jnp.exp2(_dot_nt(k, q_ref[0, hh]) - m2)
            l_ref[hh] += jnp.sum(pt.reshape(tk // 8, 8, tq), axis=0)
            acc_ref[hh] += _dot(vt_ref[0, j, hh * V_HEAD:(hh + 1) * V_HEAD, :], pt.astype(BF16))
        return carry

    lax.fori_loop(0, seq // tk, body, 0, unroll=2)
    ot = [acc_ref[hh] / jnp.sum(l_ref[hh], axis=0, keepdims=True) for hh in range(2)]
    o = jnp.concatenate(ot, axis=0).T
    o_ref[0] = (o * sg_ref[0]).astype(BF16)


def _attn_online_kernel(q_ref, k_ref, vt_ref, sg_ref, o_ref, *, seq):
    tq, tk = SAFE_TQ, ATTN_TK
    outs = []
    for hh in range(2):
        q = q_ref[0, hh]

        def body(j, carry):
            m, l, acc = carry
            k = k_ref[0, hh, pl.ds(pl.multiple_of(j * tk, tk), tk), :]
            s = _dot_nt(q, k)
            m_new = jnp.maximum(m, jnp.max(s, axis=-1, keepdims=True))
            alpha = jnp.exp2(m - m_new)
            p = jnp.exp2(s - m_new)
            l = alpha * l + jnp.sum(p, axis=-1, keepdims=True)
            acc = alpha * acc + _dot_nt(p.astype(BF16), vt_ref[0, j])
            return m_new, l, acc

        init = (jnp.full((tq, 1), -jnp.inf, F32), jnp.zeros((tq, 1), F32), jnp.zeros((tq, LANE), F32))
        m, l, acc = lax.fori_loop(0, seq // tk, body, init)
        outs.append(acc / l)
    o_ref[0] = (jnp.where(_low_half(outs[0].shape), outs[0], outs[1]) * sg_ref[0]).astype(BF16)


def _attention_call(kernel_fn, tq, q, k, vt, sg, extra_in, extra_specs, scratch, name):
    B, _, S, _ = q.shape
    return pl.pallas_call(
        functools.partial(kernel_fn, seq=S),
        grid=(B, MLA_HEADS // 2, S // tq),
        in_specs=extra_specs + [
            pl.BlockSpec((1, 2, tq, LANE), lambda b, hp, i: (b, hp, i, 0)),
            pl.BlockSpec((1, 2, S, LANE), lambda b, hp, i: (b, hp, 0, 0)),
            pl.BlockSpec((1, S // ATTN_TK, LANE, ATTN_TK), lambda b, hp, i: (b, 0, hp, 0)),
            pl.BlockSpec((1, tq, LANE), lambda b, hp, i: (b, i, hp)),
        ],
        out_specs=pl.BlockSpec((1, tq, LANE), lambda b, hp, i: (b, i, hp)),
        out_shape=jax.ShapeDtypeStruct((B, S, MLA_WIDTH), BF16),
        scratch_shapes=scratch,
        compiler_params=pltpu.CompilerParams(
            dimension_semantics=("arbitrary", "arbitrary", "arbitrary"), vmem_limit_bytes=VMEM_LIMIT),
        name=name,
    )(*extra_in, q, k, vt, sg)


def _attention(q, k, vt, sg, m2):
    def bounded(ops):
        scratch = [pltpu.VMEM((2, V_HEAD, ATTN_TQ), F32), pltpu.VMEM((2, 8, ATTN_TQ), F32)]
        return _attention_call(_attn_bounded_kernel, ATTN_TQ, *ops, [m2],
                               [pl.BlockSpec(memory_space=pltpu.SMEM)], scratch, "attn")

    def online(ops):
        return _attention_call(_attn_online_kernel, SAFE_TQ, *ops, [], [], [], "attn_online")

    return lax.cond(m2[0] <= MAX_FIXED_SHIFT, bounded, online, (q, k, vt, sg))


def _out_proj(x, o_mix, o_mem, wout_ref):
    return x + _dot(o_mix, wout_ref[0, :MLA_WIDTH, :]) + _dot(o_mem, wout_ref[0, MLA_WIDTH:, :])


def _spatial_mix(vn, ws_ref, bias):
    rows = vn.shape[0]
    n_blk = SG_WIDTH // LANE
    masked = []
    for g in range(SG_GROUPS):
        lo, hi = g * SG_GDIM, (g + 1) * SG_GDIM
        b0, b1 = lo // LANE, (hi - 1) // LANE
        blk = vn[:, b0 * LANE:(b1 + 1) * LANE]
        col = lax.broadcasted_iota(jnp.int32, blk.shape, 1) + b0 * LANE
        masked.append((b0, b1, jnp.where(col >= lo, jnp.where(col < hi, blk, 0.0), 0.0).astype(BF16)))
    chunks = []
    for c in range(rows // CHUNK):
        parts = [[] for _ in range(n_blk)]
        for g, (b0, b1, mv) in enumerate(masked):
            res = _dot(ws_ref[0, g], mv[c * CHUNK:(c + 1) * CHUNK])
            for b in range(b0, b1 + 1):
                parts[b].append(res[:, (b - b0) * LANE:(b - b0 + 1) * LANE])
        chunks.append(jnp.concatenate([functools.reduce(lambda a, b: a + b, p) for p in parts], axis=1) + bias)
    return jnp.concatenate(chunks, axis=0)


def _post_b_kernel(x_ref, og_ref, omem_ref, wout_a_ref, ln_ref, win_ref, vg_ref, ws_ref, bias_ref, mqg_ref,
                   kmem_ref, vmem_ref, wout_ref, y_ref):
    y_ref[0] = _out_proj(x_ref[0], og_ref[0], omem_ref[0], wout_a_ref)
    x = y_ref[0]
    h = (x * _rsqrt_mean(x, D_MODEL) * ln_ref[0]).astype(BF16)
    u = _gelu_tanh(_dot(h, win_ref[0, :, :SG_WIDTH]))
    v = _gelu_tanh(_dot(h, win_ref[0, :, SG_WIDTH:2 * SG_WIDTH]))
    vn = v * _rsqrt_mean(v, SG_WIDTH) * vg_ref[0]
    sg = _silu(_dot(h, win_ref[0, :, 2 * SG_WIDTH + MEM_WIDTH:]))
    o_mix = (u * _spatial_mix(vn, ws_ref, bias_ref[0]) * sg[:, :SG_WIDTH]).astype(BF16)
    qm = _dot(h, win_ref[0, :, 2 * SG_WIDTH:2 * SG_WIDTH + MEM_WIDTH])
    o_mem = (_memory_attention(qm, mqg_ref[0], kmem_ref, vmem_ref) * sg[:, SG_WIDTH:]).astype(BF16)
    y_ref[0] = _out_proj(x, o_mix, o_mem, wout_ref)


def _post_b(x, o_gated, o_mem, w, j, layer, kmem, vmem):
    B, S, _ = x.shape
    tm = ROW_TILE
    weights = [w["wout_a"], w["ln_g"], w["b_win"], w["b_vg"], w["b_ws"], w["b_bias"], w["mq_g"]]
    wlayer = [j, layer, j, j, j, j, layer]
    in_specs = [
        pl.BlockSpec((1, tm, D_MODEL), lambda b, i: (b, i, 0)),
        pl.BlockSpec((1, tm, MLA_WIDTH), lambda b, i: (b, i, 0)),
        pl.BlockSpec((1, tm, MEM_WIDTH), lambda b, i: (b, i, 0)),
    ]
    in_specs += [_const_spec(a.shape, l) for a, l in zip(weights, wlayer)]
    in_specs += _mem_specs(layer) + [_const_spec(w["wout_b"].shape, j)]
    return pl.pallas_call(
        _post_b_kernel,
        grid=(B, S // tm),
        in_specs=in_specs,
        out_specs=pl.BlockSpec((1, tm, D_MODEL), lambda b, i: (b, i, 0)),
        out_shape=jax.ShapeDtypeStruct((B, S, D_MODEL), F32),
        compiler_params=pltpu.CompilerParams(
            dimension_semantics=("arbitrary", "arbitrary"), vmem_limit_bytes=VMEM_LIMIT),
        name="post_b",
    )(x, o_gated, o_mem, *weights, kmem, vmem, w["wout_b"])


def _zeros_like_last(a, n):
    return jnp.zeros(a.shape[:-1] + (n,), a.dtype)


def _head_lanes(nope, rope):
    ref = nope if nope is not None else rope
    z = lambda n: _zeros_like_last(ref, n)
    n0, n1 = (nope[..., :48], nope[..., 48:]) if nope is not None else (z(48), z(QK_NOPE - 48))
    r0, r1 = (rope[..., :HALF_ROPE], rope[..., HALF_ROPE:]) if rope is not None else (z(HALF_ROPE), z(HALF_ROPE))
    return jnp.concatenate([r0, n0, r1, n1, z(LANE - QK_HEAD)], axis=-1)


def _prep_weights(ln_g, w_in_a, a_qlat_g, a_kvlat_g, a_w_uq, a_w_ukv, a_q_norm_g, a_k_norm_g, w_in_b,
                  b_v_norm_g, b_w_s, b_bias, mem_norm_g, w_mem_kv, mem_q_norm_g, mem_k_norm_g, w_out):
    w = {}
    na = w_in_a.shape[0]
    w["ln_g"] = ln_g[:, None, :]
    w["mem_g"] = mem_norm_g[:, None, :]

    o2, o3, o4 = Q_LORA + KV_LORA, Q_LORA + KV_LORA + QK_ROPE, Q_LORA + KV_LORA + QK_ROPE + MEM_WIDTH
    w["a_wlat"] = jnp.concatenate([w_in_a[:, :, :o2], _head_lanes(None, w_in_a[:, :, o2:o3])], axis=-1).astype(BF16)
    w["a_wqm"] = w_in_a[:, :, o3:o4].astype(BF16)
    w["a_wgate"] = w_in_a[:, :, o4:].astype(BF16)
    w["a_qlat_g"] = a_qlat_g[:, None, :]
    w["a_kvlat_g"] = a_kvlat_g[:, None, :]
    uq = a_w_uq.reshape(na, Q_LORA, MLA_HEADS, QK_HEAD)
    w["a_wuq"] = _head_lanes(uq[..., :QK_NOPE], uq[..., QK_NOPE:]).reshape(na, Q_LORA, MLA_HEADS * LANE).astype(BF16)
    uq_partner = jnp.concatenate([uq[..., QK_NOPE + HALF_ROPE:], uq[..., QK_NOPE:QK_NOPE + HALF_ROPE]], axis=-1)
    w["a_wuqs"] = _head_lanes(None, uq_partner).reshape(na, Q_LORA, MLA_HEADS * LANE).astype(BF16)
    ukv = a_w_ukv.reshape(na, KV_LORA, MLA_HEADS, QK_NOPE + V_HEAD)
    w["a_wk"] = _head_lanes(ukv[..., :QK_NOPE], None).reshape(na, KV_LORA, MLA_HEADS * LANE).astype(BF16)
    w["a_wvt"] = jnp.swapaxes(ukv[..., QK_NOPE:].reshape(na, KV_LORA, MLA_WIDTH), 1, 2).astype(BF16)

    def norm_gain_rows(g):
        g_c = _head_lanes(g[:, :QK_NOPE], g[:, QK_NOPE:])
        return jnp.stack([g_c, jnp.roll(g_c, HALF_LANE, axis=-1)], axis=1)

    w["a_qg"] = norm_gain_rows(a_q_norm_g)
    w["a_kg"] = norm_gain_rows(a_k_norm_g)
    g_max = jnp.max(jnp.abs(a_q_norm_g), axis=1) * jnp.max(jnp.abs(a_k_norm_g), axis=1)
    w["a_m2"] = (QK_HEAD ** 0.5 * LOG2E * g_max)[:, None]

    w["b_win"] = w_in_b.astype(BF16)
    w["b_vg"] = b_v_norm_g[:, None, :]
    w["b_ws"] = b_w_s.astype(BF16)
    w["b_bias"] = jnp.repeat(jnp.swapaxes(b_bias, 1, 2), SG_GDIM, axis=2)

    w["mem_wk"] = w_mem_kv[:, :, :MEM_WIDTH].astype(BF16)
    w["mem_wv"] = w_mem_kv[:, :, MEM_WIDTH:].astype(BF16)
    w["mq_g"] = jnp.tile(mem_q_norm_g, (1, MEM_HEADS))[:, None, :]
    w["mk_g"] = jnp.tile(mem_k_norm_g, (1, MEM_HEADS))[:, None, :]
    w["wout_a"] = w_out[0::2].astype(BF16)
    w["wout_b"] = w_out[1::2].astype(BF16)
    return w


def _rope_tables(seq):
    inv = 1.0 / (ROPE_BASE ** (jnp.arange(0, QK_ROPE, 2, dtype=F32) / QK_ROPE))
    ang = jnp.arange(seq, dtype=F32)[:, None] * inv[None, :]
    cos, sin = jnp.cos(ang), jnp.sin(ang)
    ct = _head_lanes(jnp.ones((seq, QK_NOPE), F32), jnp.concatenate([cos, cos], axis=1))
    st = _head_lanes(None, jnp.concatenate([-sin, sin], axis=1))
    return ct, st


def _trunk(x, mem, w):
    S = x.shape[1]
    ct, st = _rope_tables(S)
    kmem, vmem = _mem_kv(mem, w["mem_g"], w["mem_wk"], w["mem_wv"], w["mk_g"])
    for j in range(DEPTH // 2):
        q, k, vt, sg, o_mem = _pre_a(x, ct, st, w, j, 2 * j, kmem, vmem)
        o_gated = _attention(q, k, vt, sg, w["a_m2"][j])
        x = _post_b(x, o_gated, o_mem, w, j, 2 * j + 1, kmem, vmem)
    return x


def kernel(x_prompt, x_sample, mem_prompt, mem_sample, ln_g, w_in_a, a_qlat_g, a_kvlat_g, a_w_uq, a_w_ukv,
           a_q_norm_g, a_k_norm_g, w_in_b, b_v_norm_g, b_w_s, b_bias, mem_norm_g, w_mem_kv, mem_q_norm_g,
           mem_k_norm_g, w_out):
    w = _prep_weights(ln_g, w_in_a, a_qlat_g, a_kvlat_g, a_w_uq, a_w_ukv, a_q_norm_g, a_k_norm_g, w_in_b,
                      b_v_norm_g, b_w_s, b_bias, mem_norm_g, w_mem_kv, mem_q_norm_g, mem_k_norm_g, w_out)
    return _trunk(x_prompt, mem_prompt, w), _trunk(x_sample, mem_sample, w)
```

```python
import functools
import math

import jax
import jax.numpy as jnp
from jax import lax
from jax.experimental import pallas as pl
from jax.experimental.pallas import tpu as pltpu

D_MODEL = 1024
DEPTH = 4
EPS = 1e-6
N_MEM = 256
MEM_HEADS = 4
MEM_HDIM = 64
MEM_WIDTH = MEM_HEADS * MEM_HDIM
MLA_HEADS = 12
QK_NOPE = 64
QK_ROPE = 32
QK_HEAD = QK_NOPE + QK_ROPE
V_HEAD = 64
Q_LORA = 384
KV_LORA = 256
MLA_WIDTH = MLA_HEADS * V_HEAD
ROPE_BASE = 10000.0
CHUNK = 128
SG_GROUPS = 8
SG_WIDTH = 768
SG_GDIM = SG_WIDTH // SG_GROUPS
BRANCH = MLA_WIDTH + MEM_WIDTH

LANE = 128
HALF_LANE = LANE // 2
HALF_ROPE = QK_ROPE // 2

ROW_TILE = 512
ATTN_TQ = 512
ATTN_TK = 2048
SAFE_TQ = 256
VMEM_LIMIT = 56 * 1024 * 1024
LOG2E = 1.4426950408889634
MAX_FIXED_SHIFT = 40.0

BF16 = jnp.bfloat16
F32 = jnp.float32


def _rsqrt_mean(x, n):
    return lax.rsqrt(jnp.sum(x * x, axis=-1, keepdims=True) * (1.0 / n) + EPS)


def _silu(g):
    return g / (1.0 + jnp.exp(-g))


def _gelu_tanh(x):
    c = math.sqrt(2.0 / math.pi)
    return 0.5 * x * (1.0 + jnp.tanh(c * (x + 0.044715 * (x * x * x))))


def _dot(a, b):
    return jnp.dot(a, b, preferred_element_type=F32)


def _dot_nt(a, b):
    return lax.dot_general(a, b, (((1,), (1,)), ((), ())), preferred_element_type=F32)


def _low_half(shape):
    return lax.broadcasted_iota(jnp.int32, shape, 1) < HALF_LANE


def _norm_heads64(x, g):
    out = []
    for a in range(MEM_WIDTH // LANE):
        blk = x[:, a * LANE:(a + 1) * LANE]
        lo = _low_half(blk.shape)
        sq = blk * blk
        s_lo = jnp.sum(jnp.where(lo, sq, 0.0), axis=-1, keepdims=True)
        s_hi = jnp.sum(jnp.where(lo, 0.0, sq), axis=-1, keepdims=True)
        r = jnp.where(lo, lax.rsqrt(s_lo * (1.0 / MEM_HDIM) + EPS), lax.rsqrt(s_hi * (1.0 / MEM_HDIM) + EPS))
        out.append(blk * r * g[:, a * LANE:(a + 1) * LANE])
    return out


def _mem_kv_kernel(mem_ref, g_ref, wk_ref, wv_ref, kg_ref, k_out, v_out):
    m = mem_ref[0]
    mn = (m * _rsqrt_mean(m, D_MODEL) * g_ref[0]).astype(BF16)
    kn = _norm_heads64(_dot(mn, wk_ref[0]), kg_ref[0])
    k_out[0, 0] = jnp.concatenate(kn, axis=1).astype(BF16)
    vv = _dot(mn, wv_ref[0])
    lane = lax.broadcasted_iota(jnp.int32, (N_MEM, MEM_WIDTH), 1)
    for h in range(MEM_HEADS):
        vh = jnp.where(lane >= h * MEM_HDIM, jnp.where(lane < (h + 1) * MEM_HDIM, vv, 0.0), 0.0)
        v_out[0, 0, h] = vh.astype(BF16)


def _mem_kv(mem, mem_g, wk, wv, kg):
    B = mem.shape[0]
    return pl.pallas_call(
        _mem_kv_kernel,
        grid=(DEPTH, B),
        in_specs=[
            pl.BlockSpec((1, N_MEM, D_MODEL), lambda l, b: (b, 0, 0)),
            pl.BlockSpec((1, 1, D_MODEL), lambda l, b: (l, 0, 0)),
            pl.BlockSpec((1, D_MODEL, MEM_WIDTH), lambda l, b: (l, 0, 0)),
            pl.BlockSpec((1, D_MODEL, MEM_WIDTH), lambda l, b: (l, 0, 0)),
            pl.BlockSpec((1, 1, MEM_WIDTH), lambda l, b: (l, 0, 0)),
        ],
        out_specs=[
            pl.BlockSpec((1, 1, N_MEM, MEM_WIDTH), lambda l, b: (l, b, 0, 0)),
            pl.BlockSpec((1, 1, MEM_HEADS, N_MEM, MEM_WIDTH), lambda l, b: (l, b, 0, 0, 0)),
        ],
        out_shape=[
            jax.ShapeDtypeStruct((DEPTH, B, N_MEM, MEM_WIDTH), BF16),
            jax.ShapeDtypeStruct((DEPTH, B, MEM_HEADS, N_MEM, MEM_WIDTH), BF16),
        ],
        compiler_params=pltpu.CompilerParams(
            dimension_semantics=("arbitrary", "arbitrary"), vmem_limit_bytes=VMEM_LIMIT),
        name="mem_kv",
    )(mem, mem_g, wk, wv, kg)


def _mem_specs(layer):
    return [
        pl.BlockSpec((1, 1, N_MEM, MEM_WIDTH), lambda b, i: (layer, b, 0, 0)),
        pl.BlockSpec((1, 1, MEM_HEADS, N_MEM, MEM_WIDTH), lambda b, i: (layer, b, 0, 0, 0)),
    ]


def _memory_attention(qm, qg, kmem_ref, vmem_ref, shift):
    qn = _norm_heads64(qm, qg)
    o = None
    inv_l = []
    for h in range(MEM_HEADS):
        a, upper = divmod(h, 2)
        lo = _low_half(qn[a].shape)
        qh = jnp.where(lo, 0.0, qn[a]) if upper else jnp.where(lo, qn[a], 0.0)
        qh = (qh * (MEM_HDIM ** -0.5 * LOG2E)).astype(BF16)
        s = _dot_nt(qh, kmem_ref[0, 0, :, a * LANE:(a + 1) * LANE])
        if shift is None:
            e = jnp.exp2(s - jnp.max(s, axis=-1, keepdims=True))
            p = (e / jnp.sum(e, axis=-1, keepdims=True)).astype(BF16)
        else:
            e = jnp.exp2(s - shift)
            inv_l.append(1.0 / jnp.sum(e[:, :LANE] + e[:, LANE:], axis=-1, keepdims=True))
            p = e.astype(BF16)
        oh = _dot(p, vmem_ref[0, 0, h])
        o = oh if o is None else o + oh
    if shift is not None:
        lane = lax.broadcasted_iota(jnp.int32, o.shape, 1)
        scale = inv_l[MEM_HEADS - 1]
        for h in range(MEM_HEADS - 2, -1, -1):
            scale = jnp.where(lane < (h + 1) * MEM_HDIM, inv_l[h], scale)
        o = o * scale
    return o


def _mem_shift(mm2_ref, layer, bounded):
    return mm2_ref[layer] if bounded else None


def _pre_a_kernel(mm2_ref, x_ref, ct_ref, st_ref, ln_ref, wlat_ref, wqm_ref, wgate_ref, qlg_ref, kvlg_ref,
                  wuq_ref, wuqs_ref, wk_ref, wvt_ref, qg_ref, kg_ref, mqg_ref, kmem_ref, vmem_ref,
                  q_out, k_out, vt_out, sg_out, om_out, *, layer, bounded):
    x = x_ref[0]
    h = (x * _rsqrt_mean(x, D_MODEL) * ln_ref[0]).astype(BF16)

    lat = _dot(h, wlat_ref[0])
    q_lat = lat[:, :Q_LORA]
    kv_lat = lat[:, Q_LORA:Q_LORA + KV_LORA]
    kpe = lat[:, Q_LORA + KV_LORA:]

    ct = ct_ref[...]
    st = st_ref[...]

    qn = (q_lat * _rsqrt_mean(q_lat, Q_LORA) * qlg_ref[0]).astype(BF16)
    q_raw = _dot(qn, wuq_ref[0])
    q_sw = _dot(qn, wuqs_ref[0])
    q_c = ct * qg_ref[0, 0:1]
    q_s = st * qg_ref[0, 1:2]
    scale = QK_HEAD ** -0.5 * LOG2E
    for hd in range(MLA_HEADS):
        qh = q_raw[:, hd * LANE:(hd + 1) * LANE]
        r = _rsqrt_mean(qh, QK_HEAD)
        qo = (qh * q_c + q_sw[:, hd * LANE:(hd + 1) * LANE] * q_s) * (r * scale)
        q_out[0, hd] = qo.astype(BF16)

    kvn = (kv_lat * _rsqrt_mean(kv_lat, KV_LORA) * kvlg_ref[0]).astype(BF16)
    k_nope = _dot(kvn, wk_ref[0])
    vt_out[0, 0] = _dot_nt(wvt_ref[0], kvn).astype(BF16)
    k_cg = kg_ref[0, 0:1]
    kpe_rot = kpe * (ct * k_cg) + pltpu.roll(kpe, HALF_LANE, 1) * (st * kg_ref[0, 1:2])
    pe_sq = jnp.sum(kpe * kpe, axis=-1, keepdims=True)
    for hd in range(MLA_HEADS):
        kh = k_nope[:, hd * LANE:(hd + 1) * LANE]
        r = lax.rsqrt((jnp.sum(kh * kh, axis=-1, keepdims=True) + pe_sq) * (1.0 / QK_HEAD) + EPS)
        k_out[0, hd] = ((kh * k_cg + kpe_rot) * r).astype(BF16)

    sg = _silu(_dot(h, wgate_ref[0]))
    sg_out[0] = sg[:, :MLA_WIDTH]
    o_mem = _memory_attention(_dot(h, wqm_ref[0]), mqg_ref[0], kmem_ref, vmem_ref,
                              _mem_shift(mm2_ref, layer, bounded))
    om_out[0] = (o_mem * sg[:, MLA_WIDTH:]).astype(BF16)


def _const_spec(shape, layer):
    nd = len(shape)
    return pl.BlockSpec((1,) + tuple(shape[1:]), lambda b, i: (layer,) + (0,) * (nd - 1))


def _pre_a(x, ct, st, w, j, layer, kmem, vmem, bounded):
    B, S, _ = x.shape
    tm = ROW_TILE
    kv_per_chunk = ATTN_TK // tm
    weights = [w["ln_g"], w["a_wlat"], w["a_wqm"], w["a_wgate"], w["a_qlat_g"], w["a_kvlat_g"],
               w["a_wuq"], w["a_wuqs"], w["a_wk"], w["a_wvt"], w["a_qg"], w["a_kg"], w["mq_g"]]
    wlayer = [layer, j, j, j, j, j, j, j, j, j, j, j, layer]
    in_specs = [
        pl.BlockSpec(memory_space=pltpu.SMEM),
        pl.BlockSpec((1, tm, D_MODEL), lambda b, i: (b, i, 0)),
        pl.BlockSpec((tm, LANE), lambda b, i: (i, 0)),
        pl.BlockSpec((tm, LANE), lambda b, i: (i, 0)),
    ] + [_const_spec(a.shape, l) for a, l in zip(weights, wlayer)] + _mem_specs(layer)
    out_specs = [
        pl.BlockSpec((1, MLA_HEADS, tm, LANE), lambda b, i: (b, 0, i, 0)),
        pl.BlockSpec((1, MLA_HEADS, tm, LANE), lambda b, i: (b, 0, i, 0)),
        pl.BlockSpec((1, 1, MLA_WIDTH, tm), lambda b, i: (b, i // kv_per_chunk, 0, i % kv_per_chunk)),
        pl.BlockSpec((1, tm, MLA_WIDTH), lambda b, i: (b, i, 0)),
        pl.BlockSpec((1, tm, MEM_WIDTH), lambda b, i: (b, i, 0)),
    ]
    out_shape = [
        jax.ShapeDtypeStruct((B, MLA_HEADS, S, LANE), BF16),
        jax.ShapeDtypeStruct((B, MLA_HEADS, S, LANE), BF16),
        jax.ShapeDtypeStruct((B, S // ATTN_TK, MLA_WIDTH, ATTN_TK), BF16),
        jax.ShapeDtypeStruct((B, S, MLA_WIDTH), F32),
        jax.ShapeDtypeStruct((B, S, MEM_WIDTH), BF16),
    ]
    return pl.pallas_call(
        functools.partial(_pre_a_kernel, layer=layer, bounded=bounded),
        grid=(B, S // tm),
        in_specs=in_specs,
        out_specs=out_specs,
        out_shape=out_shape,
        compiler_params=pltpu.CompilerParams(
            dimension_semantics=("arbitrary", "arbitrary"), vmem_limit_bytes=VMEM_LIMIT),
        name="pre_a",
    )(w["mem_m2"], x, ct, st, *weights, kmem, vmem)


def _attn_bounded_kernel(m2_ref, q_ref, k_ref, vt_ref, sg_ref, o_ref, acc_ref, l_ref, *, seq):
    tq, tk = ATTN_TQ, ATTN_TK
    m2 = m2_ref[0]
    acc_ref[...] = jnp.zeros_like(acc_ref)
    l_ref[...] = jnp.zeros_like(l_ref)

    def body(j, carry):
        for hh in range(2):
            k = k_ref[0, hh, pl.ds(pl.multiple_of(j * tk, tk), tk), :]
            pt = jnp.exp2(_dot_nt(k, q_ref[0, hh]) - m2)
            l_ref[hh] += jnp.sum(pt.reshape(tk // 8, 8, tq), axis=0)
            acc_ref[hh] += _dot(vt_ref[0, j, hh * V_HEAD:(hh + 1) * V_HEAD, :], pt.astype(BF16))
        return carry

    lax.fori_loop(0, seq // tk, body, 0, unroll=2)
    ot = [acc_ref[hh] / jnp.sum(l_ref[hh], axis=0, keepdims=True) for hh in range(2)]
    o = jnp.concatenate(ot, axis=0).T
    o_ref[0] = (o * sg_ref[0]).astype(BF16)


def _attn_online_kernel(q_ref, k_ref, vt_ref, sg_ref, o_ref, *, seq):
    tq, tk = SAFE_TQ, ATTN_TK
    outs = []
    for hh in range(2):
        q = q_ref[0, hh]

        def body(j, carry):
            m, l, acc = carry
            k = k_ref[0, hh, pl.ds(pl.multiple_of(j * tk, tk), tk), :]
            s = _dot_nt(q, k)
            m_new = jnp.maximum(m, jnp.max(s, axis=-1, keepdims=True))
            alpha = jnp.exp2(m - m_new)
            p = jnp.exp2(s - m_new)
            l = alpha * l + jnp.sum(p, axis=-1, keepdims=True)
            acc = alpha * acc + _dot_nt(p.astype(BF16), vt_ref[0, j])
            return m_new, l, acc

        init = (jnp.full((tq, 1), -jnp.inf, F32), jnp.zeros((tq, 1), F32), jnp.zeros((tq, LANE), F32))
        m, l, acc = lax.fori_loop(0, seq // tk, body, init)
        outs.append(acc / l)
    o_ref[0] = (jnp.where(_low_half(outs[0].shape), outs[0], outs[1]) * sg_ref[0]).astype(BF16)


def _attention_call(kernel_fn, tq, q, k, vt, sg, extra_in, extra_specs, scratch, name):
    B, _, S, _ = q.shape
    return pl.pallas_call(
        functools.partial(kernel_fn, seq=S),
        grid=(B, MLA_HEADS // 2, S // tq),
        in_specs=extra_specs + [
            pl.BlockSpec((1, 2, tq, LANE), lambda b, hp, i: (b, hp, i, 0)),
            pl.BlockSpec((1, 2, S, LANE), lambda b, hp, i: (b, hp, 0, 0)),
            pl.BlockSpec((1, S // ATTN_TK, LANE, ATTN_TK), lambda b, hp, i: (b, 0, hp, 0)),
            pl.BlockSpec((1, tq, LANE), lambda b, hp, i: (b, i, hp)),
        ],
        out_specs=pl.BlockSpec((1, tq, LANE), lambda b, hp, i: (b, i, hp)),
        out_shape=jax.ShapeDtypeStruct((B, S, MLA_WIDTH), BF16),
        scratch_shapes=scratch,
        compiler_params=pltpu.CompilerParams(
            dimension_semantics=("arbitrary", "arbitrary", "arbitrary"), vmem_limit_bytes=VMEM_LIMIT),
        name=name,
    )(*extra_in, q, k, vt, sg)


def _attention(q, k, vt, sg, m2, bounded):
    if bounded:
        scratch = [pltpu.VMEM((2, V_HEAD, ATTN_TQ), F32), pltpu.VMEM((2, 8, ATTN_TQ), F32)]
        return _attention_call(_attn_bounded_kernel, ATTN_TQ, q, k, vt, sg, [m2],
                               [pl.BlockSpec(memory_space=pltpu.SMEM)], scratch, "attn")
    return _attention_call(_attn_online_kernel, SAFE_TQ, q, k, vt, sg, [], [], [], "attn_online")


def _out_proj(x, o_mix, o_mem, wout_ref):
    return x + _dot(o_mix, wout_ref[0, :MLA_WIDTH, :]) + _dot(o_mem, wout_ref[0, MLA_WIDTH:, :])


def _spatial_mix(vn, ws_ref, bias):
    rows = vn.shape[0]
    n_blk = SG_WIDTH // LANE
    masked = []
    for g in range(SG_GROUPS):
        lo, hi = g * SG_GDIM, (g + 1) * SG_GDIM
        b0, b1 = lo // LANE, (hi - 1) // LANE
        blk = vn[:, b0 * LANE:(b1 + 1) * LANE]
        col = lax.broadcasted_iota(jnp.int32, blk.shape, 1) + b0 * LANE
        masked.append((b0, b1, jnp.where(col >= lo, jnp.where(col < hi, blk, 0.0), 0.0).astype(BF16)))
    chunks = []
    for c in range(rows // CHUNK):
        parts = [[] for _ in range(n_blk)]
        for g, (b0, b1, mv) in enumerate(masked):
            res = _dot(ws_ref[0, g], mv[c * CHUNK:(c + 1) * CHUNK])
            for b in range(b0, b1 + 1):
                parts[b].append(res[:, (b - b0) * LANE:(b - b0 + 1) * LANE])
        chunks.append(jnp.concatenate([functools.reduce(lambda a, b: a + b, p) for p in parts], axis=1) + bias)
    return jnp.concatenate(chunks, axis=0)


def _post_b_kernel(mm2_ref, x_ref, og_ref, omem_ref, wout_a_ref, ln_ref, win_ref, vg_ref, ws_ref, bias_ref,
                   mqg_ref, kmem_ref, vmem_ref, wout_ref, y_ref, *, layer, bounded):
    y_ref[0] = _out_proj(x_ref[0], og_ref[0], omem_ref[0], wout_a_ref)
    x = y_ref[0]
    h = (x * _rsqrt_mean(x, D_MODEL) * ln_ref[0]).astype(BF16)
    u = _gelu_tanh(_dot(h, win_ref[0, :, :SG_WIDTH]))
    v = _gelu_tanh(_dot(h, win_ref[0, :, SG_WIDTH:2 * SG_WIDTH]))
    vn = v * _rsqrt_mean(v, SG_WIDTH) * vg_ref[0]
    sg = _silu(_dot(h, win_ref[0, :, 2 * SG_WIDTH + MEM_WIDTH:]))
    o_mix = (u * _spatial_mix(vn, ws_ref, bias_ref[0]) * sg[:, :SG_WIDTH]).astype(BF16)
    qm = _dot(h, win_ref[0, :, 2 * SG_WIDTH:2 * SG_WIDTH + MEM_WIDTH])
    o_mem = _memory_attention(qm, mqg_ref[0], kmem_ref, vmem_ref, _mem_shift(mm2_ref, layer, bounded))
    o_mem = (o_mem * sg[:, SG_WIDTH:]).astype(BF16)
    y_ref[0] = _out_proj(x, o_mix, o_mem, wout_ref)


def _post_b(x, o_gated, o_mem, w, j, layer, kmem, vmem, bounded):
    B, S, _ = x.shape
    tm = ROW_TILE
    weights = [w["wout_a"], w["ln_g"], w["b_win"], w["b_vg"], w["b_ws"], w["b_bias"], w["mq_g"]]
    wlayer = [j, layer, j, j, j, j, layer]
    in_specs = [
        pl.BlockSpec(memory_space=pltpu.SMEM),
        pl.BlockSpec((1, tm, D_MODEL), lambda b, i: (b, i, 0)),
        pl.BlockSpec((1, tm, MLA_WIDTH), lambda b, i: (b, i, 0)),
        pl.BlockSpec((1, tm, MEM_WIDTH), lambda b, i: (b, i, 0)),
    ]
    in_specs += [_const_spec(a.shape, l) for a, l in zip(weights, wlayer)]
    in_specs += _mem_specs(layer) + [_const_spec(w["wout_b"].shape, j)]
    return pl.pallas_call(
        functools.partial(_post_b_kernel, layer=layer, bounded=bounded),
        grid=(B, S // tm),
        in_specs=in_specs,
        out_specs=pl.BlockSpec((1, tm, D_MODEL), lambda b, i: (b, i, 0)),
        out_shape=jax.ShapeDtypeStruct((B, S, D_MODEL), F32),
        compiler_params=pltpu.CompilerParams(
            dimension_semantics=("arbitrary", "arbitrary"), vmem_limit_bytes=VMEM_LIMIT),
        name="post_b",
    )(w["mem_m2"], x, o_gated, o_mem, *weights, kmem, vmem, w["wout_b"])


def _zeros_like_last(a, n):
    return jnp.zeros(a.shape[:-1] + (n,), a.dtype)


def _head_lanes(nope, rope):
    ref = nope if nope is not None else rope
    z = lambda n: _zeros_like_last(ref, n)
    n0, n1 = (nope[..., :48], nope[..., 48:]) if nope is not None else (z(48), z(QK_NOPE - 48))
    r0, r1 = (rope[..., :HALF_ROPE], rope[..., HALF_ROPE:]) if rope is not None else (z(HALF_ROPE), z(HALF_ROPE))
    return jnp.concatenate([r0, n0, r1, n1, z(LANE - QK_HEAD)], axis=-1)


def _prep_weights(ln_g, w_in_a, a_qlat_g, a_kvlat_g, a_w_uq, a_w_ukv, a_q_norm_g, a_k_norm_g, w_in_b,
                  b_v_norm_g, b_w_s, b_bias, mem_norm_g, w_mem_kv, mem_q_norm_g, mem_k_norm_g, w_out):
    w = {}
    na = w_in_a.shape[0]
    w["ln_g"] = ln_g[:, None, :]
    w["mem_g"] = mem_norm_g[:, None, :]

    o2, o3, o4 = Q_LORA + KV_LORA, Q_LORA + KV_LORA + QK_ROPE, Q_LORA + KV_LORA + QK_ROPE + MEM_WIDTH
    w["a_wlat"] = jnp.concatenate([w_in_a[:, :, :o2], _head_lanes(None, w_in_a[:, :, o2:o3])], axis=-1).astype(BF16)
    w["a_wqm"] = w_in_a[:, :, o3:o4].astype(BF16)
    w["a_wgate"] = w_in_a[:, :, o4:].astype(BF16)
    w["a_qlat_g"] = a_qlat_g[:, None, :]
    w["a_kvlat_g"] = a_kvlat_g[:, None, :]
    uq = a_w_uq.reshape(na, Q_LORA, MLA_HEADS, QK_HEAD)
    w["a_wuq"] = _head_lanes(uq[..., :QK_NOPE], uq[..., QK_NOPE:]).reshape(na, Q_LORA, MLA_HEADS * LANE).astype(BF16)
    uq_partner = jnp.concatenate([uq[..., QK_NOPE + HALF_ROPE:], uq[..., QK_NOPE:QK_NOPE + HALF_ROPE]], axis=-1)
    w["a_wuqs"] = _head_lanes(None, uq_partner).reshape(na, Q_LORA, MLA_HEADS * LANE).astype(BF16)
    ukv = a_w_ukv.reshape(na, KV_LORA, MLA_HEADS, QK_NOPE + V_HEAD)
    w["a_wk"] = _head_lanes(ukv[..., :QK_NOPE], None).reshape(na, KV_LORA, MLA_HEADS * LANE).astype(BF16)
    w["a_wvt"] = jnp.swapaxes(ukv[..., QK_NOPE:].reshape(na, KV_LORA, MLA_WIDTH), 1, 2).astype(BF16)

    def norm_gain_rows(g):
        g_c = _head_lanes(g[:, :QK_NOPE], g[:, QK_NOPE:])
        return jnp.stack([g_c, jnp.roll(g_c, HALF_LANE, axis=-1)], axis=1)

    w["a_qg"] = norm_gain_rows(a_q_norm_g)
    w["a_kg"] = norm_gain_rows(a_k_norm_g)
    g_max = jnp.max(jnp.abs(a_q_norm_g), axis=1) * jnp.max(jnp.abs(a_k_norm_g), axis=1)
    w["a_m2"] = (QK_HEAD ** 0.5 * LOG2E * g_max)[:, None]

    w["b_win"] = w_in_b.astype(BF16)
    w["b_vg"] = b_v_norm_g[:, None, :]
    w["b_ws"] = b_w_s.astype(BF16)
    w["b_bias"] = jnp.repeat(jnp.swapaxes(b_bias, 1, 2), SG_GDIM, axis=2)

    w["mem_wk"] = w_mem_kv[:, :, :MEM_WIDTH].astype(BF16)
    w["mem_wv"] = w_mem_kv[:, :, MEM_WIDTH:].astype(BF16)
    w["mem_m2"] = (MEM_HDIM ** 0.5 * LOG2E * jnp.max(jnp.abs(mem_q_norm_g), axis=1)
                   * jnp.max(jnp.abs(mem_k_norm_g), axis=1))
    w["mq_g"] = jnp.tile(mem_q_norm_g, (1, MEM_HEADS))[:, None, :]
    w["mk_g"] = jnp.tile(mem_k_norm_g, (1, MEM_HEADS))[:, None, :]
    w["wout_a"] = w_out[0::2].astype(BF16)
    w["wout_b"] = w_out[1::2].astype(BF16)
    return w


def _rope_tables(seq):
    inv = 1.0 / (ROPE_BASE ** (jnp.arange(0, QK_ROPE, 2, dtype=F32) / QK_ROPE))
    ang = jnp.arange(seq, dtype=F32)[:, None] * inv[None, :]
    cos, sin = jnp.cos(ang), jnp.sin(ang)
    ct = _head_lanes(jnp.ones((seq, QK_NOPE), F32), jnp.concatenate([cos, cos], axis=1))
    st = _head_lanes(None, jnp.concatenate([-sin, sin], axis=1))
    return ct, st


def _trunk(x, mem, w, bounded):
    S = x.shape[1]
    ct, st = _rope_tables(S)
    kmem, vmem = _mem_kv(mem, w["mem_g"], w["mem_wk"], w["mem_wv"], w["mk_g"])
    for j in range(DEPTH // 2):
        q, k, vt, sg, o_mem = _pre_a(x, ct, st, w, j, 2 * j, kmem, vmem, bounded)
        o_gated = _attention(q, k, vt, sg, w["a_m2"][j], bounded)
        x = _post_b(x, o_gated, o_mem, w, j, 2 * j + 1, kmem, vmem, bounded)
    return x


def kernel(x_prompt, x_sample, mem_prompt, mem_sample, ln_g, w_in_a, a_qlat_g, a_kvlat_g, a_w_uq, a_w_ukv,
           a_q_norm_g, a_k_norm_g, w_in_b, b_v_norm_g, b_w_s, b_bias, mem_norm_g, w_mem_kv, mem_q_norm_g,
           mem_k_norm_g, w_out):
    w = _prep_weights(ln_g, w_in_a, a_qlat_g, a_kvlat_g, a_w_uq, a_w_ukv, a_q_norm_g, a_k_norm_g, w_in_b,
                      b_v_norm_g, b_w_s, b_bias, mem_norm_g, w_mem_kv, mem_q_norm_g, mem_k_norm_g, w_out)

    def run(bounded):
        return lambda ops: (_trunk(ops[0], ops[1], w, bounded), _trunk(ops[2], ops[3], w, bounded))

    shifts_ok = jnp.maximum(jnp.max(w["a_m2"]), jnp.max(w["mem_m2"])) <= MAX_FIXED_SHIFT
    return lax.cond(shifts_ok, run(True), run(False), (x_prompt, mem_prompt, x_sample, mem_sample))
```

```python
import functools
import math

import jax
import jax.numpy as jnp
from jax import lax
from jax.experimental import pallas as pl
from jax.experimental.pallas import tpu as pltpu

D_MODEL = 1024
DEPTH = 4
EPS = 1e-6
N_MEM = 256
MEM_HEADS = 4
MEM_HDIM = 64
MEM_WIDTH = MEM_HEADS * MEM_HDIM
MLA_HEADS = 12
QK_NOPE = 64
QK_ROPE = 32
QK_HEAD = QK_NOPE + QK_ROPE
V_HEAD = 64
Q_LORA = 384
KV_LORA = 256
MLA_WIDTH = MLA_HEADS * V_HEAD
ROPE_BASE = 10000.0
CHUNK = 128
SG_GROUPS = 8
SG_WIDTH = 768
SG_GDIM = SG_WIDTH // SG_GROUPS
BRANCH = MLA_WIDTH + MEM_WIDTH

LANE = 128
HALF_LANE = LANE // 2
HALF_ROPE = QK_ROPE // 2

ROW_TILE = 512
ATTN_TQ = 1024
ATTN_TK = 2048
SAFE_TQ = 256
VMEM_LIMIT = 56 * 1024 * 1024
LOG2E = 1.4426950408889634
MAX_FIXED_SHIFT = 40.0

BF16 = jnp.bfloat16
F32 = jnp.float32


def _rsqrt_mean(x, n):
    return lax.rsqrt(jnp.sum(x * x, axis=-1, keepdims=True) * (1.0 / n) + EPS)


def _silu(g):
    return g / (1.0 + jnp.exp(-g))


def _gelu_tanh(x):
    c = math.sqrt(2.0 / math.pi)
    return 0.5 * x * (1.0 + jnp.tanh(c * (x + 0.044715 * (x * x * x))))


def _dot(a, b):
    return jnp.dot(a, b, preferred_element_type=F32)


def _dot_nt(a, b):
    return lax.dot_general(a, b, (((1,), (1,)), ((), ())), preferred_element_type=F32)


def _low_half(shape):
    return lax.broadcasted_iota(jnp.int32, shape, 1) < HALF_LANE


def _norm_heads64(x, g):
    out = []
    for a in range(MEM_WIDTH // LANE):
        blk = x[:, a * LANE:(a + 1) * LANE]
        lo = _low_half(blk.shape)
        sq = blk * blk
        s_lo = jnp.sum(jnp.where(lo, sq, 0.0), axis=-1, keepdims=True)
        s_hi = jnp.sum(jnp.where(lo, 0.0, sq), axis=-1, keepdims=True)
        r = jnp.where(lo, lax.rsqrt(s_lo * (1.0 / MEM_HDIM) + EPS), lax.rsqrt(s_hi * (1.0 / MEM_HDIM) + EPS))
        out.append(blk * r * g[:, a * LANE:(a + 1) * LANE])
    return out


def _mem_kv_kernel(mem_ref, g_ref, wk_ref, wv_ref, kg_ref, k_out, v_out):
    m = mem_ref[0]
    mn = (m * _rsqrt_mean(m, D_MODEL) * g_ref[0]).astype(BF16)
    kn = _norm_heads64(_dot(mn, wk_ref[0]), kg_ref[0])
    k_out[0, 0] = jnp.concatenate(kn, axis=1).astype(BF16)
    vv = _dot(mn, wv_ref[0])
    lane = lax.broadcasted_iota(jnp.int32, (N_MEM, MEM_WIDTH), 1)
    for h in range(MEM_HEADS):
        vh = jnp.where(lane >= h * MEM_HDIM, jnp.where(lane < (h + 1) * MEM_HDIM, vv, 0.0), 0.0)
        v_out[0, 0, h] = vh.astype(BF16)


def _mem_kv(mem, mem_g, wk, wv, kg):
    B = mem.shape[0]
    return pl.pallas_call(
        _mem_kv_kernel,
        grid=(DEPTH, B),
        in_specs=[
            pl.BlockSpec((1, N_MEM, D_MODEL), lambda l, b: (b, 0, 0)),
            pl.BlockSpec((1, 1, D_MODEL), lambda l, b: (l, 0, 0)),
            pl.BlockSpec((1, D_MODEL, MEM_WIDTH), lambda l, b: (l, 0, 0)),
            pl.BlockSpec((1, D_MODEL, MEM_WIDTH), lambda l, b: (l, 0, 0)),
            pl.BlockSpec((1, 1, MEM_WIDTH), lambda l, b: (l, 0, 0)),
        ],
        out_specs=[
            pl.BlockSpec((1, 1, N_MEM, MEM_WIDTH), lambda l, b: (l, b, 0, 0)),
            pl.BlockSpec((1, 1, MEM_HEADS, N_MEM, MEM_WIDTH), lambda l, b: (l, b, 0, 0, 0)),
        ],
        out_shape=[
            jax.ShapeDtypeStruct((DEPTH, B, N_MEM, MEM_WIDTH), BF16),
            jax.ShapeDtypeStruct((DEPTH, B, MEM_HEADS, N_MEM, MEM_WIDTH), BF16),
        ],
        compiler_params=pltpu.CompilerParams(
            dimension_semantics=("arbitrary", "arbitrary"), vmem_limit_bytes=VMEM_LIMIT),
        name="mem_kv",
    )(mem, mem_g, wk, wv, kg)


def _mem_specs(layer):
    return [
        pl.BlockSpec((1, 1, N_MEM, MEM_WIDTH), lambda b, i: (layer, b, 0, 0)),
        pl.BlockSpec((1, 1, MEM_HEADS, N_MEM, MEM_WIDTH), lambda b, i: (layer, b, 0, 0, 0)),
    ]


def _memory_attention(qm, qg, kmem_ref, vmem_ref, shift):
    qn = _norm_heads64(qm, qg)
    o = None
    inv_l = []
    for h in range(MEM_HEADS):
        a, upper = divmod(h, 2)
        lo = _low_half(qn[a].shape)
        qh = jnp.where(lo, 0.0, qn[a]) if upper else jnp.where(lo, qn[a], 0.0)
        qh = (qh * (MEM_HDIM ** -0.5 * LOG2E)).astype(BF16)
        s = _dot_nt(qh, kmem_ref[0, 0, :, a * LANE:(a + 1) * LANE])
        if shift is None:
            e = jnp.exp2(s - jnp.max(s, axis=-1, keepdims=True))
            p = (e / jnp.sum(e, axis=-1, keepdims=True)).astype(BF16)
        else:
            e = jnp.exp2(s - shift)
            inv_l.append(1.0 / jnp.sum(e[:, :LANE] + e[:, LANE:], axis=-1, keepdims=True))
            p = e.astype(BF16)
        oh = _dot(p, vmem_ref[0, 0, h])
        o = oh if o is None else o + oh
    if shift is not None:
        lane = lax.broadcasted_iota(jnp.int32, o.shape, 1)
        scale = inv_l[MEM_HEADS - 1]
        for h in range(MEM_HEADS - 2, -1, -1):
            scale = jnp.where(lane < (h + 1) * MEM_HDIM, inv_l[h], scale)
        o = o * scale
    return o


def _mem_shift(mm2_ref, layer, bounded):
    return mm2_ref[layer] if bounded else None


def _pre_a_kernel(mm2_ref, x_ref, ct_ref, st_ref, ln_ref, wlat_ref, wqm_ref, wgate_ref, qlg_ref, kvlg_ref,
                  wuq_ref, wuqs_ref, wk_ref, wvt_ref, qg_ref, kg_ref, mqg_ref, kmem_ref, vmem_ref,
                  q_out, k_out, vt_out, sg_out, om_out, *, layer, bounded):
    x = x_ref[0]
    h = (x * _rsqrt_mean(x, D_MODEL) * ln_ref[0]).astype(BF16)

    lat = _dot(h, wlat_ref[0])
    q_lat = lat[:, :Q_LORA]
    kv_lat = lat[:, Q_LORA:Q_LORA + KV_LORA]
    kpe = lat[:, Q_LORA + KV_LORA:]

    ct = ct_ref[...]
    st = st_ref[...]

    qn = (q_lat * _rsqrt_mean(q_lat, Q_LORA) * qlg_ref[0]).astype(BF16)
    q_raw = _dot(qn, wuq_ref[0])
    q_sw = _dot(qn, wuqs_ref[0])
    q_c = ct * qg_ref[0, 0:1]
    q_s = st * qg_ref[0, 1:2]
    scale = QK_HEAD ** -0.5 * LOG2E
    for hd in range(MLA_HEADS):
        qh = q_raw[:, hd * LANE:(hd + 1) * LANE]
        r = _rsqrt_mean(qh, QK_HEAD)
        qo = (qh * q_c + q_sw[:, hd * LANE:(hd + 1) * LANE] * q_s) * (r * scale)
        q_out[0, hd] = qo.astype(BF16)

    kvn = (kv_lat * _rsqrt_mean(kv_lat, KV_LORA) * kvlg_ref[0]).astype(BF16)
    k_nope = _dot(kvn, wk_ref[0])
    vt_out[0, 0] = _dot_nt(wvt_ref[0], kvn).astype(BF16)
    k_cg = kg_ref[0, 0:1]
    kpe_rot = kpe * (ct * k_cg) + pltpu.roll(kpe, HALF_LANE, 1) * (st * kg_ref[0, 1:2])
    pe_sq = jnp.sum(kpe * kpe, axis=-1, keepdims=True)
    for hd in range(MLA_HEADS):
        kh = k_nope[:, hd * LANE:(hd + 1) * LANE]
        r = lax.rsqrt((jnp.sum(kh * kh, axis=-1, keepdims=True) + pe_sq) * (1.0 / QK_HEAD) + EPS)
        k_out[0, hd] = ((kh * k_cg + kpe_rot) * r).astype(BF16)

    sg = _silu(_dot(h, wgate_ref[0]))
    sg_out[0] = sg[:, :MLA_WIDTH]
    o_mem = _memory_attention(_dot(h, wqm_ref[0]), mqg_ref[0], kmem_ref, vmem_ref,
                              _mem_shift(mm2_ref, layer, bounded))
    om_out[0] = (o_mem * sg[:, MLA_WIDTH:]).astype(BF16)


def _const_spec(shape, layer):
    nd = len(shape)
    return pl.BlockSpec((1,) + tuple(shape[1:]), lambda b, i: (layer,) + (0,) * (nd - 1))


def _pre_a(x, ct, st, w, j, layer, kmem, vmem, bounded):
    B, S, _ = x.shape
    tm = ROW_TILE
    kv_per_chunk = ATTN_TK // tm
    weights = [w["ln_g"], w["a_wlat"], w["a_wqm"], w["a_wgate"], w["a_qlat_g"], w["a_kvlat_g"],
               w["a_wuq"], w["a_wuqs"], w["a_wk"], w["a_wvt"], w["a_qg"], w["a_kg"], w["mq_g"]]
    wlayer = [layer, j, j, j, j, j, j, j, j, j, j, j, layer]
    in_specs = [
        pl.BlockSpec(memory_space=pltpu.SMEM),
        pl.BlockSpec((1, tm, D_MODEL), lambda b, i: (b, i, 0)),
        pl.BlockSpec((tm, LANE), lambda b, i: (i, 0)),
        pl.BlockSpec((tm, LANE), lambda b, i: (i, 0)),
    ] + [_const_spec(a.shape, l) for a, l in zip(weights, wlayer)] + _mem_specs(layer)
    out_specs = [
        pl.BlockSpec((1, MLA_HEADS, tm, LANE), lambda b, i: (b, 0, i, 0)),
        pl.BlockSpec((1, MLA_HEADS, tm, LANE), lambda b, i: (b, 0, i, 0)),
        pl.BlockSpec((1, 1, MLA_WIDTH, tm), lambda b, i: (b, i // kv_per_chunk, 0, i % kv_per_chunk)),
        pl.BlockSpec((1, tm, MLA_WIDTH), lambda b, i: (b, i, 0)),
        pl.BlockSpec((1, tm, MEM_WIDTH), lambda b, i: (b, i, 0)),
    ]
    out_shape = [
        jax.ShapeDtypeStruct((B, MLA_HEADS, S, LANE), BF16),
        jax.ShapeDtypeStruct((B, MLA_HEADS, S, LANE), BF16),
        jax.ShapeDtypeStruct((B, S // ATTN_TK, MLA_WIDTH, ATTN_TK), BF16),
        jax.ShapeDtypeStruct((B, S, MLA_WIDTH), F32),
        jax.ShapeDtypeStruct((B, S, MEM_WIDTH), BF16),
    ]
    return pl.pallas_call(
        functools.partial(_pre_a_kernel, layer=layer, bounded=bounded),
        grid=(B, S // tm),
        in_specs=in_specs,
        out_specs=out_specs,
        out_shape=out_shape,
        compiler_params=pltpu.CompilerParams(
            dimension_semantics=("arbitrary", "arbitrary"), vmem_limit_bytes=VMEM_LIMIT),
        name="pre_a",
    )(w["mem_m2"], x, ct, st, *weights, kmem, vmem)


def _attn_bounded_kernel(m2_ref, q_ref, k_ref, vt_ref, sg_ref, o_ref, acc_ref, l_ref, *, seq):
    tq, tk = ATTN_TQ, ATTN_TK
    m2 = m2_ref[0]
    acc_ref[...] = jnp.zeros_like(acc_ref)
    l_ref[...] = jnp.zeros_like(l_ref)

    def body(j, carry):
        for hh in range(2):
            k = k_ref[0, hh, pl.ds(pl.multiple_of(j * tk, tk), tk), :]
            pt = jnp.exp2(_dot_nt(k, q_ref[0, hh]) - m2)
            l_ref[hh] += jnp.sum(pt.reshape(tk // 8, 8, tq), axis=0)
            acc_ref[hh] += _dot(vt_ref[0, j, hh * V_HEAD:(hh + 1) * V_HEAD, :], pt.astype(BF16))
        return carry

    lax.fori_loop(0, seq // tk, body, 0, unroll=2)
    ot = [acc_ref[hh] / jnp.sum(l_ref[hh], axis=0, keepdims=True) for hh in range(2)]
    o = jnp.concatenate(ot, axis=0).T
    o_ref[0] = (o * sg_ref[0]).astype(BF16)


def _attn_online_kernel(q_ref, k_ref, vt_ref, sg_ref, o_ref, *, seq):
    tq, tk = SAFE_TQ, ATTN_TK
    outs = []
    for hh in range(2):
        q = q_ref[0, hh]

        def body(j, carry):
            m, l, acc = carry
            k = k_ref[0, hh, pl.ds(pl.multiple_of(j * tk, tk), tk), :]
            s = _dot_nt(q, k)
            m_new = jnp.maximum(m, jnp.max(s, axis=-1, keepdims=True))
            alpha = jnp.exp2(m - m_new)
            p = jnp.exp2(s - m_new)
            l = alpha * l + jnp.sum(p, axis=-1, keepdims=True)
            acc = alpha * acc + _dot_nt(p.astype(BF16), vt_ref[0, j])
            return m_new, l, acc

        init = (jnp.full((tq, 1), -jnp.inf, F32), jnp.zeros((tq, 1), F32), jnp.zeros((tq, LANE), F32))
        m, l, acc = lax.fori_loop(0, seq // tk, body, init)
        outs.append(acc / l)
    o_ref[0] = (jnp.where(_low_half(outs[0].shape), outs[0], outs[1]) * sg_ref[0]).astype(BF16)


def _attention_call(kernel_fn, tq, q, k, vt, sg, extra_in, extra_specs, scratch, name):
    B, _, S, _ = q.shape
    return pl.pallas_call(
        functools.partial(kernel_fn, seq=S),
        grid=(B, MLA_HEADS // 2, S // tq),
        in_specs=extra_specs + [
            pl.BlockSpec((1, 2, tq, LANE), lambda b, hp, i: (b, hp, i, 0)),
            pl.BlockSpec((1, 2, S, LANE), lambda b, hp, i: (b, hp, 0, 0)),
            pl.BlockSpec((1, S // ATTN_TK, LANE, ATTN_TK), lambda b, hp, i: (b, 0, hp, 0)),
            pl.BlockSpec((1, tq, LANE), lambda b, hp, i: (b, i, hp)),
        ],
        out_specs=pl.BlockSpec((1, tq, LANE), lambda b, hp, i: (b, i, hp)),
        out_shape=jax.ShapeDtypeStruct((B, S, MLA_WIDTH), BF16),
        scratch_shapes=scratch,
        compiler_params=pltpu.CompilerParams(
            dimension_semantics=("arbitrary", "arbitrary", "arbitrary"), vmem_limit_bytes=VMEM_LIMIT),
        name=name,
    )(*extra_in, q, k, vt, sg)


def _attention(q, k, vt, sg, m2, bounded):
    if bounded:
        scratch = [pltpu.VMEM((2, V_HEAD, ATTN_TQ), F32), pltpu.VMEM((2, 8, ATTN_TQ), F32)]
        return _attention_call(_attn_bounded_kernel, ATTN_TQ, q, k, vt, sg, [m2],
                               [pl.BlockSpec(memory_space=pltpu.SMEM)], scratch, "attn")
    return _attention_call(_attn_online_kernel, SAFE_TQ, q, k, vt, sg, [], [], [], "attn_online")


def _out_proj(x, o_mix, o_mem, wout_ref):
    return x + _dot(o_mix, wout_ref[0, :MLA_WIDTH, :]) + _dot(o_mem, wout_ref[0, MLA_WIDTH:, :])


def _spatial_mix(vn, ws_ref, bias):
    rows = vn.shape[0]
    n_blk = SG_WIDTH // LANE
    masked = []
    for g in range(SG_GROUPS):
        lo, hi = g * SG_GDIM, (g + 1) * SG_GDIM
        b0, b1 = lo // LANE, (hi - 1) // LANE
        blk = vn[:, b0 * LANE:(b1 + 1) * LANE]
        col = lax.broadcasted_iota(jnp.int32, blk.shape, 1) + b0 * LANE
        masked.append((b0, b1, jnp.where(col >= lo, jnp.where(col < hi, blk, 0.0), 0.0).astype(BF16)))
    chunks = []
    for c in range(rows // CHUNK):
        parts = [[] for _ in range(n_blk)]
        for g, (b0, b1, mv) in enumerate(masked):
            res = _dot(ws_ref[0, g], mv[c * CHUNK:(c + 1) * CHUNK])
            for b in range(b0, b1 + 1):
                parts[b].append(res[:, (b - b0) * LANE:(b - b0 + 1) * LANE])
        chunks.append(jnp.concatenate([functools.reduce(lambda a, b: a + b, p) for p in parts], axis=1) + bias)
    return jnp.concatenate(chunks, axis=0)


def _post_b_kernel(mm2_ref, x_ref, og_ref, omem_ref, wout_a_ref, ln_ref, win_ref, vg_ref, ws_ref, bias_ref,
                   mqg_ref, kmem_ref, vmem_ref, wout_ref, y_ref, *, layer, bounded):
    y_ref[0] = _out_proj(x_ref[0], og_ref[0], omem_ref[0], wout_a_ref)
    x = y_ref[0]
    h = (x * _rsqrt_mean(x, D_MODEL) * ln_ref[0]).astype(BF16)
    u = _gelu_tanh(_dot(h, win_ref[0, :, :SG_WIDTH]))
    v = _gelu_tanh(_dot(h, win_ref[0, :, SG_WIDTH:2 * SG_WIDTH]))
    vn = v * _rsqrt_mean(v, SG_WIDTH) * vg_ref[0]
    sg = _silu(_dot(h, win_ref[0, :, 2 * SG_WIDTH + MEM_WIDTH:]))
    o_mix = (u * _spatial_mix(vn, ws_ref, bias_ref[0]) * sg[:, :SG_WIDTH]).astype(BF16)
    qm = _dot(h, win_ref[0, :, 2 * SG_WIDTH:2 * SG_WIDTH + MEM_WIDTH])
    o_mem = _memory_attention(qm, mqg_ref[0], kmem_ref, vmem_ref, _mem_shift(mm2_ref, layer, bounded))
    o_mem = (o_mem * sg[:, SG_WIDTH:]).astype(BF16)
    y_ref[0] = _out_proj(x, o_mix, o_mem, wout_ref)


def _post_b(x, o_gated, o_mem, w, j, layer, kmem, vmem, bounded):
    B, S, _ = x.shape
    tm = ROW_TILE
    weights = [w["wout_a"], w["ln_g"], w["b_win"], w["b_vg"], w["b_ws"], w["b_bias"], w["mq_g"]]
    wlayer = [j, layer, j, j, j, j, layer]
    in_specs = [
        pl.BlockSpec(memory_space=pltpu.SMEM),
        pl.BlockSpec((1, tm, D_MODEL), lambda b, i: (b, i, 0)),
        pl.BlockSpec((1, tm, MLA_WIDTH), lambda b, i: (b, i, 0)),
        pl.BlockSpec((1, tm, MEM_WIDTH), lambda b, i: (b, i, 0)),
    ]
    in_specs += [_const_spec(a.shape, l) for a, l in zip(weights, wlayer)]
    in_specs += _mem_specs(layer) + [_const_spec(w["wout_b"].shape, j)]
    return pl.pallas_call(
        functools.partial(_post_b_kernel, layer=layer, bounded=bounded),
        grid=(B, S // tm),
        in_specs=in_specs,
        out_specs=pl.BlockSpec((1, tm, D_MODEL), lambda b, i: (b, i, 0)),
        out_shape=jax.ShapeDtypeStruct((B, S, D_MODEL), F32),
        compiler_params=pltpu.CompilerParams(
            dimension_semantics=("arbitrary", "arbitrary"), vmem_limit_bytes=VMEM_LIMIT),
        name="post_b",
    )(w["mem_m2"], x, o_gated, o_mem, *weights, kmem, vmem, w["wout_b"])


def _zeros_like_last(a, n):
    return jnp.zeros(a.shape[:-1] + (n,), a.dtype)


def _head_lanes(nope, rope):
    ref = nope if nope is not None else rope
    z = lambda n: _zeros_like_last(ref, n)
    n0, n1 = (nope[..., :48], nope[..., 48:]) if nope is not None else (z(48), z(QK_NOPE - 48))
    r0, r1 = (rope[..., :HALF_ROPE], rope[..., HALF_ROPE:]) if rope is not None else (z(HALF_ROPE), z(HALF_ROPE))
    return jnp.concatenate([r0, n0, r1, n1, z(LANE - QK_HEAD)], axis=-1)


def _prep_weights(ln_g, w_in_a, a_qlat_g, a_kvlat_g, a_w_uq, a_w_ukv, a_q_norm_g, a_k_norm_g, w_in_b,
                  b_v_norm_g, b_w_s, b_bias, mem_norm_g, w_mem_kv, mem_q_norm_g, mem_k_norm_g, w_out):
    w = {}
    na = w_in_a.shape[0]
    w["ln_g"] = ln_g[:, None, :]
    w["mem_g"] = mem_norm_g[:, None, :]

    o2, o3, o4 = Q_LORA + KV_LORA, Q_LORA + KV_LORA + QK_ROPE, Q_LORA + KV_LORA + QK_ROPE + MEM_WIDTH
    w["a_wlat"] = jnp.concatenate([w_in_a[:, :, :o2], _head_lanes(None, w_in_a[:, :, o2:o3])], axis=-1).astype(BF16)
    w["a_wqm"] = w_in_a[:, :, o3:o4].astype(BF16)
    w["a_wgate"] = w_in_a[:, :, o4:].astype(BF16)
    w["a_qlat_g"] = a_qlat_g[:, None, :]
    w["a_kvlat_g"] = a_kvlat_g[:, None, :]
    uq = a_w_uq.reshape(na, Q_LORA, MLA_HEADS, QK_HEAD)
    w["a_wuq"] = _head_lanes(uq[..., :QK_NOPE], uq[..., QK_NOPE:]).reshape(na, Q_LORA, MLA_HEADS * LANE).astype(BF16)
    uq_partner = jnp.concatenate([uq[..., QK_NOPE + HALF_ROPE:], uq[..., QK_NOPE:QK_NOPE + HALF_ROPE]], axis=-1)
    w["a_wuqs"] = _head_lanes(None, uq_partner).reshape(na, Q_LORA, MLA_HEADS * LANE).astype(BF16)
    ukv = a_w_ukv.reshape(na, KV_LORA, MLA_HEADS, QK_NOPE + V_HEAD)
    w["a_wk"] = _head_lanes(ukv[..., :QK_NOPE], None).reshape(na, KV_LORA, MLA_HEADS * LANE).astype(BF16)
    w["a_wvt"] = jnp.swapaxes(ukv[..., QK_NOPE:].reshape(na, KV_LORA, MLA_WIDTH), 1, 2).astype(BF16)

    def norm_gain_rows(g):
        g_c = _head_lanes(g[:, :QK_NOPE], g[:, QK_NOPE:])
        return jnp.stack([g_c, jnp.roll(g_c, HALF_LANE, axis=-1)], axis=1)

    w["a_qg"] = norm_gain_rows(a_q_norm_g)
    w["a_kg"] = norm_gain_rows(a_k_norm_g)
    g_max = jnp.max(jnp.abs(a_q_norm_g), axis=1) * jnp.max(jnp.abs(a_k_norm_g), axis=1)
    w["a_m2"] = (QK_HEAD ** 0.5 * LOG2E * g_max)[:, None]

    w["b_win"] = w_in_b.astype(BF16)
    w["b_vg"] = b_v_norm_g[:, None, :]
    w["b_ws"] = b_w_s.astype(BF16)
    w["b_bias"] = jnp.repeat(jnp.swapaxes(b_bias, 1, 2), SG_GDIM, axis=2)

    w["mem_wk"] = w_mem_kv[:, :, :MEM_WIDTH].astype(BF16)
    w["mem_wv"] = w_mem_kv[:, :, MEM_WIDTH:].astype(BF16)
    w["mem_m2"] = (MEM_HDIM ** 0.5 * LOG2E * jnp.max(jnp.abs(mem_q_norm_g), axis=1)
                   * jnp.max(jnp.abs(mem_k_norm_g), axis=1))
    w["mq_g"] = jnp.tile(mem_q_norm_g, (1, MEM_HEADS))[:, None, :]
    w["mk_g"] = jnp.tile(mem_k_norm_g, (1, MEM_HEADS))[:, None, :]
    w["wout_a"] = w_out[0::2].astype(BF16)
    w["wout_b"] = w_out[1::2].astype(BF16)
    return w


def _rope_tables(seq):
    inv = 1.0 / (ROPE_BASE ** (jnp.arange(0, QK_ROPE, 2, dtype=F32) / QK_ROPE))
    ang = jnp.arange(seq, dtype=F32)[:, None] * inv[None, :]
    cos, sin = jnp.cos(ang), jnp.sin(ang)
    ct = _head_lanes(jnp.ones((seq, QK_NOPE), F32), jnp.concatenate([cos, cos], axis=1))
    st = _head_lanes(None, jnp.concatenate([-sin, sin], axis=1))
    return ct, st


def _trunk(x, mem, w, bounded):
    S = x.shape[1]
    ct, st = _rope_tables(S)
    kmem, vmem = _mem_kv(mem, w["mem_g"], w["mem_wk"], w["mem_wv"], w["mk_g"])
    for j in range(DEPTH // 2):
        q, k, vt, sg, o_mem = _pre_a(x, ct, st, w, j, 2 * j, kmem, vmem, bounded)
        o_gated = _attention(q, k, vt, sg, w["a_m2"][j], bounded)
        x = _post_b(x, o_gated, o_mem, w, j, 2 * j + 1, kmem, vmem, bounded)
    return x


def kernel(x_prompt, x_sample, mem_prompt, mem_sample, ln_g, w_in_a, a_qlat_g, a_kvlat_g, a_w_uq, a_w_ukv,
           a_q_norm_g, a_k_norm_g, w_in_b, b_v_norm_g, b_w_s, b_bias, mem_norm_g, w_mem_kv, mem_q_norm_g,
           mem_k_norm_g, w_out):
    w = _prep_weights(ln_g, w_in_a, a_qlat_g, a_kvlat_g, a_w_uq, a_w_ukv, a_q_norm_g, a_k_norm_g, w_in_b,
                      b_v_norm_g, b_w_s, b_bias, mem_norm_g, w_mem_kv, mem_q_norm_g, mem_k_norm_g, w_out)

    def run(bounded):
        return lambda ops: (_trunk(ops[0], ops[1], w, bounded), _trunk(ops[2], ops[3], w, bounded))

    shifts_ok = jnp.maximum(jnp.max(w["a_m2"]), jnp.max(w["mem_m2"])) <= MAX_FIXED_SHIFT
    return lax.cond(shifts_ok, run(True), run(False), (x_prompt, mem_prompt, x_sample, mem_sample))
```

```python
import functools
import math

import jax
import jax.numpy as jnp
from jax import lax
from jax.experimental import pallas as pl
from jax.experimental.pallas import tpu as pltpu

D_MODEL = 1024
DEPTH = 4
EPS = 1e-6
N_MEM = 256
MEM_HEADS = 4
MEM_HDIM = 64
MEM_WIDTH = MEM_HEADS * MEM_HDIM
MLA_HEADS = 12
QK_NOPE = 64
QK_ROPE = 32
QK_HEAD = QK_NOPE + QK_ROPE
V_HEAD = 64
Q_LORA = 384
KV_LORA = 256
MLA_WIDTH = MLA_HEADS * V_HEAD
ROPE_BASE = 10000.0
CHUNK = 128
SG_GROUPS = 8
SG_WIDTH = 768
SG_GDIM = SG_WIDTH // SG_GROUPS
BRANCH = MLA_WIDTH + MEM_WIDTH

LANE = 128
HALF_LANE = LANE // 2
HALF_ROPE = QK_ROPE // 2

ROW_TILE = 512
ATTN_TQ = 2048
ATTN_TK = 2048
SAFE_TQ = 256
VMEM_LIMIT = 56 * 1024 * 1024
LOG2E = 1.4426950408889634
MAX_FIXED_SHIFT = 40.0

BF16 = jnp.bfloat16
F32 = jnp.float32


def _rsqrt_mean(x, n):
    return lax.rsqrt(jnp.sum(x * x, axis=-1, keepdims=True) * (1.0 / n) + EPS)


def _silu(g):
    return g / (1.0 + jnp.exp(-g))


def _gelu_tanh(x):
    c = math.sqrt(2.0 / math.pi)
    return 0.5 * x * (1.0 + jnp.tanh(c * (x + 0.044715 * (x * x * x))))


def _dot(a, b):
    return jnp.dot(a, b, preferred_element_type=F32)


def _dot_nt(a, b):
    return lax.dot_general(a, b, (((1,), (1,)), ((), ())), preferred_element_type=F32)


def _low_half(shape):
    return lax.broadcasted_iota(jnp.int32, shape, 1) < HALF_LANE


def _norm_heads64(x, g):
    out = []
    for a in range(MEM_WIDTH // LANE):
        blk = x[:, a * LANE:(a + 1) * LANE]
        lo = _low_half(blk.shape)
        sq = blk * blk
        s_lo = jnp.sum(jnp.where(lo, sq, 0.0), axis=-1, keepdims=True)
        s_hi = jnp.sum(jnp.where(lo, 0.0, sq), axis=-1, keepdims=True)
        r = jnp.where(lo, lax.rsqrt(s_lo * (1.0 / MEM_HDIM) + EPS), lax.rsqrt(s_hi * (1.0 / MEM_HDIM) + EPS))
        out.append(blk * r * g[:, a * LANE:(a + 1) * LANE])
    return out


def _mem_kv_kernel(mem_ref, g_ref, wk_ref, wv_ref, kg_ref, k_out, v_out):
    m = mem_ref[0]
    mn = (m * _rsqrt_mean(m, D_MODEL) * g_ref[0]).astype(BF16)
    kn = _norm_heads64(_dot(mn, wk_ref[0]), kg_ref[0])
    k_out[0, 0] = jnp.concatenate(kn, axis=1).astype(BF16)
    vv = _dot(mn, wv_ref[0])
    lane = lax.broadcasted_iota(jnp.int32, (N_MEM, MEM_WIDTH), 1)
    for h in range(MEM_HEADS):
        vh = jnp.where(lane >= h * MEM_HDIM, jnp.where(lane < (h + 1) * MEM_HDIM, vv, 0.0), 0.0)
        v_out[0, 0, h] = vh.astype(BF16)


def _mem_kv(mem, mem_g, wk, wv, kg):
    B = mem.shape[0]
    return pl.pallas_call(
        _mem_kv_kernel,
        grid=(DEPTH, B),
        in_specs=[
            pl.BlockSpec((1, N_MEM, D_MODEL), lambda l, b: (b, 0, 0)),
            pl.BlockSpec((1, 1, D_MODEL), lambda l, b: (l, 0, 0)),
            pl.BlockSpec((1, D_MODEL, MEM_WIDTH), lambda l, b: (l, 0, 0)),
            pl.BlockSpec((1, D_MODEL, MEM_WIDTH), lambda l, b: (l, 0, 0)),
            pl.BlockSpec((1, 1, MEM_WIDTH), lambda l, b: (l, 0, 0)),
        ],
        out_specs=[
            pl.BlockSpec((1, 1, N_MEM, MEM_WIDTH), lambda l, b: (l, b, 0, 0)),
            pl.BlockSpec((1, 1, MEM_HEADS, N_MEM, MEM_WIDTH), lambda l, b: (l, b, 0, 0, 0)),
        ],
        out_shape=[
            jax.ShapeDtypeStruct((DEPTH, B, N_MEM, MEM_WIDTH), BF16),
            jax.ShapeDtypeStruct((DEPTH, B, MEM_HEADS, N_MEM, MEM_WIDTH), BF16),
        ],
        compiler_params=pltpu.CompilerParams(
            dimension_semantics=("arbitrary", "arbitrary"), vmem_limit_bytes=VMEM_LIMIT),
        name="mem_kv",
    )(mem, mem_g, wk, wv, kg)


def _mem_specs(layer):
    return [
        pl.BlockSpec((1, 1, N_MEM, MEM_WIDTH), lambda b, i: (layer, b, 0, 0)),
        pl.BlockSpec((1, 1, MEM_HEADS, N_MEM, MEM_WIDTH), lambda b, i: (layer, b, 0, 0, 0)),
    ]


def _memory_attention(qm, qg, kmem_ref, vmem_ref, shift):
    qn = _norm_heads64(qm, qg)
    o = None
    inv_l = []
    for h in range(MEM_HEADS):
        a, upper = divmod(h, 2)
        lo = _low_half(qn[a].shape)
        qh = jnp.where(lo, 0.0, qn[a]) if upper else jnp.where(lo, qn[a], 0.0)
        qh = (qh * (MEM_HDIM ** -0.5 * LOG2E)).astype(BF16)
        s = _dot_nt(qh, kmem_ref[0, 0, :, a * LANE:(a + 1) * LANE])
        if shift is None:
            e = jnp.exp2(s - jnp.max(s, axis=-1, keepdims=True))
            p = (e / jnp.sum(e, axis=-1, keepdims=True)).astype(BF16)
        else:
            e = jnp.exp2(s - shift)
            inv_l.append(1.0 / jnp.sum(e[:, :LANE] + e[:, LANE:], axis=-1, keepdims=True))
            p = e.astype(BF16)
        oh = _dot(p, vmem_ref[0, 0, h])
        o = oh if o is None else o + oh
    if shift is not None:
        lane = lax.broadcasted_iota(jnp.int32, o.shape, 1)
        scale = inv_l[MEM_HEADS - 1]
        for h in range(MEM_HEADS - 2, -1, -1):
            scale = jnp.where(lane < (h + 1) * MEM_HDIM, inv_l[h], scale)
        o = o * scale
    return o


def _mem_shift(mm2_ref, layer, bounded):
    return mm2_ref[layer] if bounded else None


def _pre_a_kernel(mm2_ref, x_ref, ct_ref, st_ref, ln_ref, wlat_ref, wqm_ref, wgate_ref, qlg_ref, kvlg_ref,
                  wuq_ref, wuqs_ref, wk_ref, wvt_ref, qg_ref, kg_ref, mqg_ref, kmem_ref, vmem_ref,
                  q_out, k_out, vt_out, sg_out, om_out, *, layer, bounded):
    x = x_ref[0]
    h = (x * _rsqrt_mean(x, D_MODEL) * ln_ref[0]).astype(BF16)

    lat = _dot(h, wlat_ref[0])
    q_lat = lat[:, :Q_LORA]
    kv_lat = lat[:, Q_LORA:Q_LORA + KV_LORA]
    kpe = lat[:, Q_LORA + KV_LORA:]

    ct = ct_ref[...]
    st = st_ref[...]

    qn = (q_lat * _rsqrt_mean(q_lat, Q_LORA) * qlg_ref[0]).astype(BF16)
    q_raw = _dot(qn, wuq_ref[0])
    q_sw = _dot(qn, wuqs_ref[0])
    q_c = ct * qg_ref[0, 0:1]
    q_s = st * qg_ref[0, 1:2]
    scale = QK_HEAD ** -0.5 * LOG2E
    for hd in range(MLA_HEADS):
        qh = q_raw[:, hd * LANE:(hd + 1) * LANE]
        r = _rsqrt_mean(qh, QK_HEAD)
        qo = (qh * q_c + q_sw[:, hd * LANE:(hd + 1) * LANE] * q_s) * (r * scale)
        q_out[0, hd] = qo.astype(BF16)

    kvn = (kv_lat * _rsqrt_mean(kv_lat, KV_LORA) * kvlg_ref[0]).astype(BF16)
    k_nope = _dot(kvn, wk_ref[0])
    vt_out[0, 0] = _dot_nt(wvt_ref[0], kvn).astype(BF16)
    k_cg = kg_ref[0, 0:1]
    kpe_rot = kpe * (ct * k_cg) + pltpu.roll(kpe, HALF_LANE, 1) * (st * kg_ref[0, 1:2])
    pe_sq = jnp.sum(kpe * kpe, axis=-1, keepdims=True)
    for hd in range(MLA_HEADS):
        kh = k_nope[:, hd * LANE:(hd + 1) * LANE]
        r = lax.rsqrt((jnp.sum(kh * kh, axis=-1, keepdims=True) + pe_sq) * (1.0 / QK_HEAD) + EPS)
        k_out[0, hd] = ((kh * k_cg + kpe_rot) * r).astype(BF16)

    sg = _silu(_dot(h, wgate_ref[0]))
    sg_out[0] = sg[:, :MLA_WIDTH]
    o_mem = _memory_attention(_dot(h, wqm_ref[0]), mqg_ref[0], kmem_ref, vmem_ref,
                              _mem_shift(mm2_ref, layer, bounded))
    om_out[0] = (o_mem * sg[:, MLA_WIDTH:]).astype(BF16)


def _const_spec(shape, layer):
    nd = len(shape)
    return pl.BlockSpec((1,) + tuple(shape[1:]), lambda b, i: (layer,) + (0,) * (nd - 1))


def _pre_a(x, ct, st, w, j, layer, kmem, vmem, bounded):
    B, S, _ = x.shape
    tm = ROW_TILE
    kv_per_chunk = ATTN_TK // tm
    weights = [w["ln_g"], w["a_wlat"], w["a_wqm"], w["a_wgate"], w["a_qlat_g"], w["a_kvlat_g"],
               w["a_wuq"], w["a_wuqs"], w["a_wk"], w["a_wvt"], w["a_qg"], w["a_kg"], w["mq_g"]]
    wlayer = [layer, j, j, j, j, j, j, j, j, j, j, j, layer]
    in_specs = [
        pl.BlockSpec(memory_space=pltpu.SMEM),
        pl.BlockSpec((1, tm, D_MODEL), lambda b, i: (b, i, 0)),
        pl.BlockSpec((tm, LANE), lambda b, i: (i, 0)),
        pl.BlockSpec((tm, LANE), lambda b, i: (i, 0)),
    ] + [_const_spec(a.shape, l) for a, l in zip(weights, wlayer)] + _mem_specs(layer)
    out_specs = [
        pl.BlockSpec((1, MLA_HEADS, tm, LANE), lambda b, i: (b, 0, i, 0)),
        pl.BlockSpec((1, MLA_HEADS, tm, LANE), lambda b, i: (b, 0, i, 0)),
        pl.BlockSpec((1, 1, MLA_WIDTH, tm), lambda b, i: (b, i // kv_per_chunk, 0, i % kv_per_chunk)),
        pl.BlockSpec((1, tm, MLA_WIDTH), lambda b, i: (b, i, 0)),
        pl.BlockSpec((1, tm, MEM_WIDTH), lambda b, i: (b, i, 0)),
    ]
    out_shape = [
        jax.ShapeDtypeStruct((B, MLA_HEADS, S, LANE), BF16),
        jax.ShapeDtypeStruct((B, MLA_HEADS, S, LANE), BF16),
        jax.ShapeDtypeStruct((B, S // ATTN_TK, MLA_WIDTH, ATTN_TK), BF16),
        jax.ShapeDtypeStruct((B, S, MLA_WIDTH), F32),
        jax.ShapeDtypeStruct((B, S, MEM_WIDTH), BF16),
    ]
    return pl.pallas_call(
        functools.partial(_pre_a_kernel, layer=layer, bounded=bounded),
        grid=(B, S // tm),
        in_specs=in_specs,
        out_specs=out_specs,
        out_shape=out_shape,
        compiler_params=pltpu.CompilerParams(
            dimension_semantics=("arbitrary", "arbitrary"), vmem_limit_bytes=VMEM_LIMIT),
        name="pre_a",
    )(w["mem_m2"], x, ct, st, *weights, kmem, vmem)


def _attn_bounded_kernel(m2_ref, q_ref, k_ref, vt_ref, sg_ref, o_ref, acc_ref, l_ref, *, seq):
    tq, tk = ATTN_TQ, ATTN_TK
    m2 = m2_ref[0]
    acc_ref[...] = jnp.zeros_like(acc_ref)
    l_ref[...] = jnp.zeros_like(l_ref)

    def body(j, carry):
        for hh in range(2):
            k = k_ref[0, hh, pl.ds(pl.multiple_of(j * tk, tk), tk), :]
            pt = jnp.exp2(_dot_nt(k, q_ref[0, hh]) - m2)
            l_ref[hh] += jnp.sum(pt.reshape(tk // 8, 8, tq), axis=0)
            acc_ref[hh] += _dot(vt_ref[0, j, hh * V_HEAD:(hh + 1) * V_HEAD, :], pt.astype(BF16))
        return carry

    lax.fori_loop(0, seq // tk, body, 0, unroll=2)
    ot = [acc_ref[hh] / jnp.sum(l_ref[hh], axis=0, keepdims=True) for hh in range(2)]
    o = jnp.concatenate(ot, axis=0).T
    o_ref[0] = (o * sg_ref[0]).astype(BF16)


def _attn_online_kernel(q_ref, k_ref, vt_ref, sg_ref, o_ref, *, seq):
    tq, tk = SAFE_TQ, ATTN_TK
    outs = []
    for hh in range(2):
        q = q_ref[0, hh]

        def body(j, carry):
            m, l, acc = carry
            k = k_ref[0, hh, pl.ds(pl.multiple_of(j * tk, tk), tk), :]
            s = _dot_nt(q, k)
            m_new = jnp.maximum(m, jnp.max(s, axis=-1, keepdims=True))
            alpha = jnp.exp2(m - m_new)
            p = jnp.exp2(s - m_new)
            l = alpha * l + jnp.sum(p, axis=-1, keepdims=True)
            acc = alpha * acc + _dot_nt(p.astype(BF16), vt_ref[0, j])
            return m_new, l, acc

        init = (jnp.full((tq, 1), -jnp.inf, F32), jnp.zeros((tq, 1), F32), jnp.zeros((tq, LANE), F32))
        m, l, acc = lax.fori_loop(0, seq // tk, body, init)
        outs.append(acc / l)
    o_ref[0] = (jnp.where(_low_half(outs[0].shape), outs[0], outs[1]) * sg_ref[0]).astype(BF16)


def _attention_call(kernel_fn, tq, q, k, vt, sg, extra_in, extra_specs, scratch, name):
    B, _, S, _ = q.shape
    return pl.pallas_call(
        functools.partial(kernel_fn, seq=S),
        grid=(B, MLA_HEADS // 2, S // tq),
        in_specs=extra_specs + [
            pl.BlockSpec((1, 2, tq, LANE), lambda b, hp, i: (b, hp, i, 0)),
            pl.BlockSpec((1, 2, S, LANE), lambda b, hp, i: (b, hp, 0, 0)),
            pl.BlockSpec((1, S // ATTN_TK, LANE, ATTN_TK), lambda b, hp, i: (b, 0, hp, 0)),
            pl.BlockSpec((1, tq, LANE), lambda b, hp, i: (b, i, hp)),
        ],
        out_specs=pl.BlockSpec((1, tq, LANE), lambda b, hp, i: (b, i, hp)),
        out_shape=jax.ShapeDtypeStruct((B, S, MLA_WIDTH), BF16),
        scratch_shapes=scratch,
        compiler_params=pltpu.CompilerParams(
            dimension_semantics=("arbitrary", "arbitrary", "arbitrary"), vmem_limit_bytes=VMEM_LIMIT),
        name=name,
    )(*extra_in, q, k, vt, sg)


def _attention(q, k, vt, sg, m2, bounded):
    if bounded:
        scratch = [pltpu.VMEM((2, V_HEAD, ATTN_TQ), F32), pltpu.VMEM((2, 8, ATTN_TQ), F32)]
        return _attention_call(_attn_bounded_kernel, ATTN_TQ, q, k, vt, sg, [m2],
                               [pl.BlockSpec(memory_space=pltpu.SMEM)], scratch, "attn")
    return _attention_call(_attn_online_kernel, SAFE_TQ, q, k, vt, sg, [], [], [], "attn_online")


def _out_proj(x, o_mix, o_mem, wout_ref):
    return x + _dot(o_mix, wout_ref[0, :MLA_WIDTH, :]) + _dot(o_mem, wout_ref[0, MLA_WIDTH:, :])


def _spatial_mix(vn, ws_ref, bias):
    rows = vn.shape[0]
    n_blk = SG_WIDTH // LANE
    masked = []
    for g in range(SG_GROUPS):
        lo, hi = g * SG_GDIM, (g + 1) * SG_GDIM
        b0, b1 = lo // LANE, (hi - 1) // LANE
        blk = vn[:, b0 * LANE:(b1 + 1) * LANE]
        col = lax.broadcasted_iota(jnp.int32, blk.shape, 1) + b0 * LANE
        masked.append((b0, b1, jnp.where(col >= lo, jnp.where(col < hi, blk, 0.0), 0.0).astype(BF16)))
    chunks = []
    for c in range(rows // CHUNK):
        parts = [[] for _ in range(n_blk)]
        for g, (b0, b1, mv) in enumerate(masked):
            res = _dot(ws_ref[0, g], mv[c * CHUNK:(c + 1) * CHUNK])
            for b in range(b0, b1 + 1):
                parts[b].append(res[:, (b - b0) * LANE:(b - b0 + 1) * LANE])
        chunks.append(jnp.concatenate([functools.reduce(lambda a, b: a + b, p) for p in parts], axis=1) + bias)
    return jnp.concatenate(chunks, axis=0)


def _post_b_kernel(mm2_ref, x_ref, og_ref, omem_ref, wout_a_ref, ln_ref, win_ref, vg_ref, ws_ref, bias_ref,
                   mqg_ref, kmem_ref, vmem_ref, wout_ref, y_ref, *, layer, bounded):
    y_ref[0] = _out_proj(x_ref[0], og_ref[0], omem_ref[0], wout_a_ref)
    x = y_ref[0]
    h = (x * _rsqrt_mean(x, D_MODEL) * ln_ref[0]).astype(BF16)
    u = _gelu_tanh(_dot(h, win_ref[0, :, :SG_WIDTH]))
    v = _gelu_tanh(_dot(h, win_ref[0, :, SG_WIDTH:2 * SG_WIDTH]))
    vn = v * _rsqrt_mean(v, SG_WIDTH) * vg_ref[0]
    sg = _silu(_dot(h, win_ref[0, :, 2 * SG_WIDTH + MEM_WIDTH:]))
    o_mix = (u * _spatial_mix(vn, ws_ref, bias_ref[0]) * sg[:, :SG_WIDTH]).astype(BF16)
    qm = _dot(h, win_ref[0, :, 2 * SG_WIDTH:2 * SG_WIDTH + MEM_WIDTH])
    o_mem = _memory_attention(qm, mqg_ref[0], kmem_ref, vmem_ref, _mem_shift(mm2_ref, layer, bounded))
    o_mem = (o_mem * sg[:, SG_WIDTH:]).astype(BF16)
    y_ref[0] = _out_proj(x, o_mix, o_mem, wout_ref)


def _post_b(x, o_gated, o_mem, w, j, layer, kmem, vmem, bounded):
    B, S, _ = x.shape
    tm = ROW_TILE
    weights = [w["wout_a"], w["ln_g"], w["b_win"], w["b_vg"], w["b_ws"], w["b_bias"], w["mq_g"]]
    wlayer = [j, layer, j, j, j, j, layer]
    in_specs = [
        pl.BlockSpec(memory_space=pltpu.SMEM),
        pl.BlockSpec((1, tm, D_MODEL), lambda b, i: (b, i, 0)),
        pl.BlockSpec((1, tm, MLA_WIDTH), lambda b, i: (b, i, 0)),
        pl.BlockSpec((1, tm, MEM_WIDTH), lambda b, i: (b, i, 0)),
    ]
    in_specs += [_const_spec(a.shape, l) for a, l in zip(weights, wlayer)]
    in_specs += _mem_specs(layer) + [_const_spec(w["wout_b"].shape, j)]
    return pl.pallas_call(
        functools.partial(_post_b_kernel, layer=layer, bounded=bounded),
        grid=(B, S // tm),
        in_specs=in_specs,
        out_specs=pl.BlockSpec((1, tm, D_MODEL), lambda b, i: (b, i, 0)),
        out_shape=jax.ShapeDtypeStruct((B, S, D_MODEL), F32),
        compiler_params=pltpu.CompilerParams(
            dimension_semantics=("arbitrary", "arbitrary"), vmem_limit_bytes=VMEM_LIMIT),
        name="post_b",
    )(w["mem_m2"], x, o_gated, o_mem, *weights, kmem, vmem, w["wout_b"])


def _zeros_like_last(a, n):
    return jnp.zeros(a.shape[:-1] + (n,), a.dtype)


def _head_lanes(nope, rope):
    ref = nope if nope is not None else rope
    z = lambda n: _zeros_like_last(ref, n)
    n0, n1 = (nope[..., :48], nope[..., 48:]) if nope is not None else (z(48), z(QK_NOPE - 48))
    r0, r1 = (rope[..., :HALF_ROPE], rope[..., HALF_ROPE:]) if rope is not None else (z(HALF_ROPE), z(HALF_ROPE))
    return jnp.concatenate([r0, n0, r1, n1, z(LANE - QK_HEAD)], axis=-1)


def _prep_weights(ln_g, w_in_a, a_qlat_g, a_kvlat_g, a_w_uq, a_w_ukv, a_q_norm_g, a_k_norm_g, w_in_b,
                  b_v_norm_g, b_w_s, b_bias, mem_norm_g, w_mem_kv, mem_q_norm_g, mem_k_norm_g, w_out):
    w = {}
    na = w_in_a.shape[0]
    w["ln_g"] = ln_g[:, None, :]
    w["mem_g"] = mem_norm_g[:, None, :]

    o2, o3, o4 = Q_LORA + KV_LORA, Q_LORA + KV_LORA + QK_ROPE, Q_LORA + KV_LORA + QK_ROPE + MEM_WIDTH
    w["a_wlat"] = jnp.concatenate([w_in_a[:, :, :o2], _head_lanes(None, w_in_a[:, :, o2:o3])], axis=-1).astype(BF16)
    w["a_wqm"] = w_in_a[:, :, o3:o4].astype(BF16)
    w["a_wgate"] = w_in_a[:, :, o4:].astype(BF16)
    w["a_qlat_g"] = a_qlat_g[:, None, :]
    w["a_kvlat_g"] = a_kvlat_g[:, None, :]
    uq = a_w_uq.reshape(na, Q_LORA, MLA_HEADS, QK_HEAD)
    w["a_wuq"] = _head_lanes(uq[..., :QK_NOPE], uq[..., QK_NOPE:]).reshape(na, Q_LORA, MLA_HEADS * LANE).astype(BF16)
    uq_partner = jnp.concatenate([uq[..., QK_NOPE + HALF_ROPE:], uq[..., QK_NOPE:QK_NOPE + HALF_ROPE]], axis=-1)
    w["a_wuqs"] = _head_lanes(None, uq_partner).reshape(na, Q_LORA, MLA_HEADS * LANE).astype(BF16)
    ukv = a_w_ukv.reshape(na, KV_LORA, MLA_HEADS, QK_NOPE + V_HEAD)
    w["a_wk"] = _head_lanes(ukv[..., :QK_NOPE], None).reshape(na, KV_LORA, MLA_HEADS * LANE).astype(BF16)
    w["a_wvt"] = jnp.swapaxes(ukv[..., QK_NOPE:].reshape(na, KV_LORA, MLA_WIDTH), 1, 2).astype(BF16)

    def norm_gain_rows(g):
        g_c = _head_lanes(g[:, :QK_NOPE], g[:, QK_NOPE:])
        return jnp.stack([g_c, jnp.roll(g_c, HALF_LANE, axis=-1)], axis=1)

    w["a_qg"] = norm_gain_rows(a_q_norm_g)
    w["a_kg"] = norm_gain_rows(a_k_norm_g)
    g_max = jnp.max(jnp.abs(a_q_norm_g), axis=1) * jnp.max(jnp.abs(a_k_norm_g), axis=1)
    w["a_m2"] = (QK_HEAD ** 0.5 * LOG2E * g_max)[:, None]

    w["b_win"] = w_in_b.astype(BF16)
    w["b_vg"] = b_v_norm_g[:, None, :]
    w["b_ws"] = b_w_s.astype(BF16)
    w["b_bias"] = jnp.repeat(jnp.swapaxes(b_bias, 1, 2), SG_GDIM, axis=2)

    w["mem_wk"] = w_mem_kv[:, :, :MEM_WIDTH].astype(BF16)
    w["mem_wv"] = w_mem_kv[:, :, MEM_WIDTH:].astype(BF16)
    w["mem_m2"] = (MEM_HDIM ** 0.5 * LOG2E * jnp.max(jnp.abs(mem_q_norm_g), axis=1)
                   * jnp.max(jnp.abs(mem_k_norm_g), axis=1))
    w["mq_g"] = jnp.tile(mem_q_norm_g, (1, MEM_HEADS))[:, None, :]
    w["mk_g"] = jnp.tile(mem_k_norm_g, (1, MEM_HEADS))[:, None, :]
    w["wout_a"] = w_out[0::2].astype(BF16)
    w["wout_b"] = w_out[1::2].astype(BF16)
    return w


def _rope_tables(seq):
    inv = 1.0 / (ROPE_BASE ** (jnp.arange(0, QK_ROPE, 2, dtype=F32) / QK_ROPE))
    ang = jnp.arange(seq, dtype=F32)[:, None] * inv[None, :]
    cos, sin = jnp.cos(ang), jnp.sin(ang)
    ct = _head_lanes(jnp.ones((seq, QK_NOPE), F32), jnp.concatenate([cos, cos], axis=1))
    st = _head_lanes(None, jnp.concatenate([-sin, sin], axis=1))
    return ct, st


def _trunk(x, mem, w, bounded):
    S = x.shape[1]
    ct, st = _rope_tables(S)
    kmem, vmem = _mem_kv(mem, w["mem_g"], w["mem_wk"], w["mem_wv"], w["mk_g"])
    for j in range(DEPTH // 2):
        q, k, vt, sg, o_mem = _pre_a(x, ct, st, w, j, 2 * j, kmem, vmem, bounded)
        o_gated = _attention(q, k, vt, sg, w["a_m2"][j], bounded)
        x = _post_b(x, o_gated, o_mem, w, j, 2 * j + 1, kmem, vmem, bounded)
    return x


def kernel(x_prompt, x_sample, mem_prompt, mem_sample, ln_g, w_in_a, a_qlat_g, a_kvlat_g, a_w_uq, a_w_ukv,
           a_q_norm_g, a_k_norm_g, w_in_b, b_v_norm_g, b_w_s, b_bias, mem_norm_g, w_mem_kv, mem_q_norm_g,
           mem_k_norm_g, w_out):
    w = _prep_weights(ln_g, w_in_a, a_qlat_g, a_kvlat_g, a_w_uq, a_w_ukv, a_q_norm_g, a_k_norm_g, w_in_b,
                      b_v_norm_g, b_w_s, b_bias, mem_norm_g, w_mem_kv, mem_q_norm_g, mem_k_norm_g, w_out)

    def run(bounded):
        return lambda ops: (_trunk(ops[0], ops[1], w, bounded), _trunk(ops[2], ops[3], w, bounded))

    shifts_ok = jnp.maximum(jnp.max(w["a_m2"]), jnp.max(w["mem_m2"])) <= MAX_FIXED_SHIFT
    return lax.cond(shifts_ok, run(True), run(False), (x_prompt, mem_prompt, x_sample, mem_sample))
```

```python
import functools
import math

import jax
import jax.numpy as jnp
from jax import lax
from jax.experimental import pallas as pl
from jax.experimental.pallas import tpu as pltpu

D_MODEL = 1024
DEPTH = 4
EPS = 1e-6
N_MEM = 256
MEM_HEADS = 4
MEM_HDIM = 64
MEM_WIDTH = MEM_HEADS * MEM_HDIM
MLA_HEADS = 12
QK_NOPE = 64
QK_ROPE = 32
QK_HEAD = QK_NOPE + QK_ROPE
V_HEAD = 64
Q_LORA = 384
KV_LORA = 256
MLA_WIDTH = MLA_HEADS * V_HEAD
ROPE_BASE = 10000.0
CHUNK = 128
SG_GROUPS = 8
SG_WIDTH = 768
SG_GDIM = SG_WIDTH // SG_GROUPS
BRANCH = MLA_WIDTH + MEM_WIDTH

LANE = 128
HALF_LANE = LANE // 2
HALF_ROPE = QK_ROPE // 2

ROW_TILE = 512
ATTN_TQ = 2048
ATTN_TK = 2048
SAFE_TQ = 512
VMEM_LIMIT = 56 * 1024 * 1024
LOG2E = 1.4426950408889634
MAX_FIXED_SHIFT = 40.0

BF16 = jnp.bfloat16
F32 = jnp.float32


def _rsqrt_mean(x, n):
    return lax.rsqrt(jnp.sum(x * x, axis=-1, keepdims=True) * (1.0 / n) + EPS)


def _silu(g):
    return g / (1.0 + jnp.exp(-g))


def _gelu_tanh(x):
    c = math.sqrt(2.0 / math.pi)
    return 0.5 * x * (1.0 + jnp.tanh(c * (x + 0.044715 * (x * x * x))))


def _dot(a, b):
    return jnp.dot(a, b, preferred_element_type=F32)


def _dot_nt(a, b):
    return lax.dot_general(a, b, (((1,), (1,)), ((), ())), preferred_element_type=F32)


def _low_half(shape):
    return lax.broadcasted_iota(jnp.int32, shape, 1) < HALF_LANE


def _norm_heads64(x, g):
    out = []
    for a in range(MEM_WIDTH // LANE):
        blk = x[:, a * LANE:(a + 1) * LANE]
        lo = _low_half(blk.shape)
        sq = blk * blk
        s_lo = jnp.sum(jnp.where(lo, sq, 0.0), axis=-1, keepdims=True)
        s_hi = jnp.sum(jnp.where(lo, 0.0, sq), axis=-1, keepdims=True)
        r = jnp.where(lo, lax.rsqrt(s_lo * (1.0 / MEM_HDIM) + EPS), lax.rsqrt(s_hi * (1.0 / MEM_HDIM) + EPS))
        out.append(blk * r * g[:, a * LANE:(a + 1) * LANE])
    return out


def _mem_kv_kernel(mem_ref, g_ref, wk_ref, wv_ref, kg_ref, k_out, v_out):
    m = mem_ref[0]
    mn = (m * _rsqrt_mean(m, D_MODEL) * g_ref[0]).astype(BF16)
    kn = _norm_heads64(_dot(mn, wk_ref[0]), kg_ref[0])
    k_out[0, 0] = jnp.concatenate(kn, axis=1).astype(BF16)
    vv = _dot(mn, wv_ref[0])
    lane = lax.broadcasted_iota(jnp.int32, (N_MEM, MEM_WIDTH), 1)
    for h in range(MEM_HEADS):
        vh = jnp.where(lane >= h * MEM_HDIM, jnp.where(lane < (h + 1) * MEM_HDIM, vv, 0.0), 0.0)
        v_out[0, 0, h] = vh.astype(BF16)


def _mem_kv(mem, mem_g, wk, wv, kg):
    B = mem.shape[0]
    return pl.pallas_call(
        _mem_kv_kernel,
        grid=(DEPTH, B),
        in_specs=[
            pl.BlockSpec((1, N_MEM, D_MODEL), lambda l, b: (b, 0, 0)),
            pl.BlockSpec((1, 1, D_MODEL), lambda l, b: (l, 0, 0)),
            pl.BlockSpec((1, D_MODEL, MEM_WIDTH), lambda l, b: (l, 0, 0)),
            pl.BlockSpec((1, D_MODEL, MEM_WIDTH), lambda l, b: (l, 0, 0)),
            pl.BlockSpec((1, 1, MEM_WIDTH), lambda l, b: (l, 0, 0)),
        ],
        out_specs=[
            pl.BlockSpec((1, 1, N_MEM, MEM_WIDTH), lambda l, b: (l, b, 0, 0)),
            pl.BlockSpec((1, 1, MEM_HEADS, N_MEM, MEM_WIDTH), lambda l, b: (l, b, 0, 0, 0)),
        ],
        out_shape=[
            jax.ShapeDtypeStruct((DEPTH, B, N_MEM, MEM_WIDTH), BF16),
            jax.ShapeDtypeStruct((DEPTH, B, MEM_HEADS, N_MEM, MEM_WIDTH), BF16),
        ],
        compiler_params=pltpu.CompilerParams(
            dimension_semantics=("arbitrary", "arbitrary"), vmem_limit_bytes=VMEM_LIMIT),
        name="mem_kv",
    )(mem, mem_g, wk, wv, kg)


def _mem_specs(layer):
    return [
        pl.BlockSpec((1, 1, N_MEM, MEM_WIDTH), lambda b, i: (layer, b, 0, 0)),
        pl.BlockSpec((1, 1, MEM_HEADS, N_MEM, MEM_WIDTH), lambda b, i: (layer, b, 0, 0, 0)),
    ]


def _memory_attention(qm, qg, kmem_ref, vmem_ref, shift):
    qn = _norm_heads64(qm, qg)
    o = None
    inv_l = []
    for h in range(MEM_HEADS):
        a, upper = divmod(h, 2)
        lo = _low_half(qn[a].shape)
        qh = jnp.where(lo, 0.0, qn[a]) if upper else jnp.where(lo, qn[a], 0.0)
        qh = (qh * (MEM_HDIM ** -0.5 * LOG2E)).astype(BF16)
        s = _dot_nt(qh, kmem_ref[0, 0, :, a * LANE:(a + 1) * LANE])
        if shift is None:
            e = jnp.exp2(s - jnp.max(s, axis=-1, keepdims=True))
            p = (e / jnp.sum(e, axis=-1, keepdims=True)).astype(BF16)
        else:
            e = jnp.exp2(s - shift)
            inv_l.append(1.0 / jnp.sum(e[:, :LANE] + e[:, LANE:], axis=-1, keepdims=True))
            p = e.astype(BF16)
        oh = _dot(p, vmem_ref[0, 0, h])
        o = oh if o is None else o + oh
    if shift is not None:
        lane = lax.broadcasted_iota(jnp.int32, o.shape, 1)
        scale = inv_l[MEM_HEADS - 1]
        for h in range(MEM_HEADS - 2, -1, -1):
            scale = jnp.where(lane < (h + 1) * MEM_HDIM, inv_l[h], scale)
        o = o * scale
    return o


def _mem_shift(mm2_ref, layer, bounded):
    return mm2_ref[layer] if bounded else None


def _pre_a_kernel(mm2_ref, x_ref, ct_ref, st_ref, ctt_ref, stt_ref, ln_ref, wlat_ref, wqm_ref, wgate_ref,
                  qlg_ref, kvlg_ref, wuqt_ref, wk_ref, wvt_ref, qg_ref, kg_ref, mqg_ref, kmem_ref, vmem_ref,
                  qt_out, k_out, vt_out, sg_out, om_out, *, layer, bounded):
    x = x_ref[0]
    h = (x * _rsqrt_mean(x, D_MODEL) * ln_ref[0]).astype(BF16)

    lat = _dot(h, wlat_ref[0])
    q_lat = lat[:, :Q_LORA]
    kv_lat = lat[:, Q_LORA:Q_LORA + KV_LORA]
    kpe = lat[:, Q_LORA + KV_LORA:]

    ct = ct_ref[...]
    st = st_ref[...]

    rows = x.shape[0]
    qn = (q_lat * _rsqrt_mean(q_lat, Q_LORA) * qlg_ref[0]).astype(BF16)
    qt_raw = _dot_nt(wuqt_ref[0], qn)
    q_c = ctt_ref[...] * jnp.tile(qg_ref[0, 0], (1, rows // LANE))
    q_s = stt_ref[...] * jnp.tile(qg_ref[0, 1], (1, rows // LANE))
    scale = QK_HEAD ** -0.5 * LOG2E
    for hd in range(MLA_HEADS):
        qh = qt_raw[hd * LANE:(hd + 1) * LANE, :]
        r = lax.rsqrt(jnp.sum(qh * qh, axis=0, keepdims=True) * (1.0 / QK_HEAD) + EPS)
        partner = jnp.concatenate([qh[HALF_LANE:], qh[:HALF_LANE]], axis=0)
        qt_out[0, hd] = ((qh * q_c + partner * q_s) * (r * scale)).astype(BF16)

    kvn = (kv_lat * _rsqrt_mean(kv_lat, KV_LORA) * kvlg_ref[0]).astype(BF16)
    k_nope = _dot(kvn, wk_ref[0])
    vt_out[0, 0] = _dot_nt(wvt_ref[0], kvn).astype(BF16)
    k_cg = kg_ref[0, 0:1]
    kpe_rot = kpe * (ct * k_cg) + pltpu.roll(kpe, HALF_LANE, 1) * (st * kg_ref[0, 1:2])
    pe_sq = jnp.sum(kpe * kpe, axis=-1, keepdims=True)
    for hd in range(MLA_HEADS):
        kh = k_nope[:, hd * LANE:(hd + 1) * LANE]
        r = lax.rsqrt((jnp.sum(kh * kh, axis=-1, keepdims=True) + pe_sq) * (1.0 / QK_HEAD) + EPS)
        k_out[0, hd] = ((kh * k_cg + kpe_rot) * r).astype(BF16)

    sg = _silu(_dot(h, wgate_ref[0]))
    sg_out[0] = sg[:, :MLA_WIDTH]
    o_mem = _memory_attention(_dot(h, wqm_ref[0]), mqg_ref[0], kmem_ref, vmem_ref,
                              _mem_shift(mm2_ref, layer, bounded))
    om_out[0] = (o_mem * sg[:, MLA_WIDTH:]).astype(BF16)


def _const_spec(shape, layer):
    nd = len(shape)
    return pl.BlockSpec((1,) + tuple(shape[1:]), lambda b, i: (layer,) + (0,) * (nd - 1))


def _pre_a(x, rope, w, j, layer, kmem, vmem, bounded):
    B, S, _ = x.shape
    tm = ROW_TILE
    kv_per_chunk = ATTN_TK // tm
    weights = [w["ln_g"], w["a_wlat"], w["a_wqm"], w["a_wgate"], w["a_qlat_g"], w["a_kvlat_g"],
               w["a_wuqt"], w["a_wk"], w["a_wvt"], w["a_qg"], w["a_kg"], w["mq_g"]]
    wlayer = [layer, j, j, j, j, j, j, j, j, j, j, layer]
    in_specs = [
        pl.BlockSpec(memory_space=pltpu.SMEM),
        pl.BlockSpec((1, tm, D_MODEL), lambda b, i: (b, i, 0)),
        pl.BlockSpec((tm, LANE), lambda b, i: (i, 0)),
        pl.BlockSpec((tm, LANE), lambda b, i: (i, 0)),
        pl.BlockSpec((LANE, tm), lambda b, i: (0, i)),
        pl.BlockSpec((LANE, tm), lambda b, i: (0, i)),
    ] + [_const_spec(a.shape, l) for a, l in zip(weights, wlayer)] + _mem_specs(layer)
    out_specs = [
        pl.BlockSpec((1, MLA_HEADS, LANE, tm), lambda b, i: (b, 0, 0, i)),
        pl.BlockSpec((1, MLA_HEADS, tm, LANE), lambda b, i: (b, 0, i, 0)),
        pl.BlockSpec((1, 1, MLA_WIDTH, tm), lambda b, i: (b, i // kv_per_chunk, 0, i % kv_per_chunk)),
        pl.BlockSpec((1, tm, MLA_WIDTH), lambda b, i: (b, i, 0)),
        pl.BlockSpec((1, tm, MEM_WIDTH), lambda b, i: (b, i, 0)),
    ]
    out_shape = [
        jax.ShapeDtypeStruct((B, MLA_HEADS, LANE, S), BF16),
        jax.ShapeDtypeStruct((B, MLA_HEADS, S, LANE), BF16),
        jax.ShapeDtypeStruct((B, S // ATTN_TK, MLA_WIDTH, ATTN_TK), BF16),
        jax.ShapeDtypeStruct((B, S, MLA_WIDTH), F32),
        jax.ShapeDtypeStruct((B, S, MEM_WIDTH), BF16),
    ]
    return pl.pallas_call(
        functools.partial(_pre_a_kernel, layer=layer, bounded=bounded),
        grid=(B, S // tm),
        in_specs=in_specs,
        out_specs=out_specs,
        out_shape=out_shape,
        compiler_params=pltpu.CompilerParams(
            dimension_semantics=("arbitrary", "arbitrary"), vmem_limit_bytes=VMEM_LIMIT),
        name="pre_a",
    )(w["mem_m2"], x, *rope, *weights, kmem, vmem)


def _attn_kernel(m2_ref, qt_ref, k_ref, vt_ref, sg_ref, o_ref, acc_ref, l_ref, m_ref, *, seq, bounded):
    tq, tk = o_ref.shape[1], ATTN_TK
    acc_ref[...] = jnp.zeros_like(acc_ref)
    l_ref[...] = jnp.zeros_like(l_ref)
    if not bounded:
        m_ref[...] = jnp.full_like(m_ref, -jnp.inf)

    def body(j, carry):
        for hh in range(2):
            k = k_ref[0, hh, pl.ds(pl.multiple_of(j * tk, tk), tk), :]
            st = _dot(k, qt_ref[0, hh])
            vt = vt_ref[0, j, hh * V_HEAD:(hh + 1) * V_HEAD, :]
            if bounded:
                pt = jnp.exp2(st - m2_ref[0])
                l_ref[hh] += jnp.sum(pt.reshape(tk // 8, 8, tq), axis=0)
                acc_ref[hh] += _dot(vt, pt.astype(BF16))
            else:
                m_old = m_ref[hh]
                m_new = jnp.maximum(m_old, jnp.max(st, axis=0, keepdims=True))
                alpha = jnp.exp2(m_old - m_new)
                pt = jnp.exp2(st - m_new)
                l_ref[hh] = alpha * l_ref[hh] + jnp.sum(pt.reshape(tk // 8, 8, tq), axis=0)
                acc_ref[hh] = alpha * acc_ref[hh] + _dot(vt, pt.astype(BF16))
                m_ref[hh] = m_new
        return carry

    lax.fori_loop(0, seq // tk, body, 0, unroll=2)
    ot = [acc_ref[hh] / jnp.sum(l_ref[hh], axis=0, keepdims=True) for hh in range(2)]
    o = jnp.concatenate(ot, axis=0).T
    o_ref[0] = (o * sg_ref[0]).astype(BF16)


def _attention(qt, k, vt, sg, m2, bounded):
    B, _, S, _ = k.shape
    tq = ATTN_TQ if bounded else SAFE_TQ
    return pl.pallas_call(
        functools.partial(_attn_kernel, seq=S, bounded=bounded),
        grid=(B, MLA_HEADS // 2, S // tq),
        in_specs=[
            pl.BlockSpec(memory_space=pltpu.SMEM),
            pl.BlockSpec((1, 2, LANE, tq), lambda b, hp, i: (b, hp, 0, i)),
            pl.BlockSpec((1, 2, S, LANE), lambda b, hp, i: (b, hp, 0, 0)),
            pl.BlockSpec((1, S // ATTN_TK, LANE, ATTN_TK), lambda b, hp, i: (b, 0, hp, 0)),
            pl.BlockSpec((1, tq, LANE), lambda b, hp, i: (b, i, hp)),
        ],
        out_specs=pl.BlockSpec((1, tq, LANE), lambda b, hp, i: (b, i, hp)),
        out_shape=jax.ShapeDtypeStruct((B, S, MLA_WIDTH), BF16),
        scratch_shapes=[pltpu.VMEM((2, V_HEAD, tq), F32), pltpu.VMEM((2, 8, tq), F32),
                        pltpu.VMEM((2, 1, tq), F32)],
        compiler_params=pltpu.CompilerParams(
            dimension_semantics=("arbitrary", "arbitrary", "arbitrary"), vmem_limit_bytes=VMEM_LIMIT),
        name="attn",
    )(m2, qt, k, vt, sg)


def _out_proj(x, o_mix, o_mem, wout_ref):
    return x + _dot(o_mix, wout_ref[0, :MLA_WIDTH, :]) + _dot(o_mem, wout_ref[0, MLA_WIDTH:, :])


def _spatial_mix(vn, ws_ref, bias):
    rows = vn.shape[0]
    n_blk = SG_WIDTH // LANE
    masked = []
    for g in range(SG_GROUPS):
        lo, hi = g * SG_GDIM, (g + 1) * SG_GDIM
        b0, b1 = lo // LANE, (hi - 1) // LANE
        blk = vn[:, b0 * LANE:(b1 + 1) * LANE]
        col = lax.broadcasted_iota(jnp.int32, blk.shape, 1) + b0 * LANE
        masked.append((b0, b1, jnp.where(col >= lo, jnp.where(col < hi, blk, 0.0), 0.0).astype(BF16)))
    chunks = []
    for c in range(rows // CHUNK):
        parts = [[] for _ in range(n_blk)]
        for g, (b0, b1, mv) in enumerate(masked):
            res = _dot(ws_ref[0, g], mv[c * CHUNK:(c + 1) * CHUNK])
            for b in range(b0, b1 + 1):
                parts[b].append(res[:, (b - b0) * LANE:(b - b0 + 1) * LANE])
        chunks.append(jnp.concatenate([functools.reduce(lambda a, b: a + b, p) for p in parts], axis=1) + bias)
    return jnp.concatenate(chunks, axis=0)


def _post_b_kernel(mm2_ref, x_ref, og_ref, omem_ref, wout_a_ref, ln_ref, win_ref, vg_ref, ws_ref, bias_ref,
                   mqg_ref, kmem_ref, vmem_ref, wout_ref, y_ref, *, layer, bounded):
    y_ref[0] = _out_proj(x_ref[0], og_ref[0], omem_ref[0], wout_a_ref)
    x = y_ref[0]
    h = (x * _rsqrt_mean(x, D_MODEL) * ln_ref[0]).astype(BF16)
    u = _gelu_tanh(_dot(h, win_ref[0, :, :SG_WIDTH]))
    v = _gelu_tanh(_dot(h, win_ref[0, :, SG_WIDTH:2 * SG_WIDTH]))
    vn = v * _rsqrt_mean(v, SG_WIDTH) * vg_ref[0]
    sg = _silu(_dot(h, win_ref[0, :, 2 * SG_WIDTH + MEM_WIDTH:]))
    o_mix = (u * _spatial_mix(vn, ws_ref, bias_ref[0]) * sg[:, :SG_WIDTH]).astype(BF16)
    qm = _dot(h, win_ref[0, :, 2 * SG_WIDTH:2 * SG_WIDTH + MEM_WIDTH])
    o_mem = _memory_attention(qm, mqg_ref[0], kmem_ref, vmem_ref, _mem_shift(mm2_ref, layer, bounded))
    o_mem = (o_mem * sg[:, SG_WIDTH:]).astype(BF16)
    y_ref[0] = _out_proj(x, o_mix, o_mem, wout_ref)


def _post_b(x, o_gated, o_mem, w, j, layer, kmem, vmem, bounded):
    B, S, _ = x.shape
    tm = ROW_TILE
    weights = [w["wout_a"], w["ln_g"], w["b_win"], w["b_vg"], w["b_ws"], w["b_bias"], w["mq_g"]]
    wlayer = [j, layer, j, j, j, j, layer]
    in_specs = [
        pl.BlockSpec(memory_space=pltpu.SMEM),
        pl.BlockSpec((1, tm, D_MODEL), lambda b, i: (b, i, 0)),
        pl.BlockSpec((1, tm, MLA_WIDTH), lambda b, i: (b, i, 0)),
        pl.BlockSpec((1, tm, MEM_WIDTH), lambda b, i: (b, i, 0)),
    ]
    in_specs += [_const_spec(a.shape, l) for a, l in zip(weights, wlayer)]
    in_specs += _mem_specs(layer) + [_const_spec(w["wout_b"].shape, j)]
    return pl.pallas_call(
        functools.partial(_post_b_kernel, layer=layer, bounded=bounded),
        grid=(B, S // tm),
        in_specs=in_specs,
        out_specs=pl.BlockSpec((1, tm, D_MODEL), lambda b, i: (b, i, 0)),
        out_shape=jax.ShapeDtypeStruct((B, S, D_MODEL), F32),
        compiler_params=pltpu.CompilerParams(
            dimension_semantics=("arbitrary", "arbitrary"), vmem_limit_bytes=VMEM_LIMIT),
        name="post_b",
    )(w["mem_m2"], x, o_gated, o_mem, *weights, kmem, vmem, w["wout_b"])


def _zeros_like_last(a, n):
    return jnp.zeros(a.shape[:-1] + (n,), a.dtype)


def _head_lanes(nope, rope):
    ref = nope if nope is not None else rope
    z = lambda n: _zeros_like_last(ref, n)
    n0, n1 = (nope[..., :48], nope[..., 48:]) if nope is not None else (z(48), z(QK_NOPE - 48))
    r0, r1 = (rope[..., :HALF_ROPE], rope[..., HALF_ROPE:]) if rope is not None else (z(HALF_ROPE), z(HALF_ROPE))
    return jnp.concatenate([r0, n0, r1, n1, z(LANE - QK_HEAD)], axis=-1)


def _prep_weights(ln_g, w_in_a, a_qlat_g, a_kvlat_g, a_w_uq, a_w_ukv, a_q_norm_g, a_k_norm_g, w_in_b,
                  b_v_norm_g, b_w_s, b_bias, mem_norm_g, w_mem_kv, mem_q_norm_g, mem_k_norm_g, w_out):
    w = {}
    na = w_in_a.shape[0]
    w["ln_g"] = ln_g[:, None, :]
    w["mem_g"] = mem_norm_g[:, None, :]

    o2, o3, o4 = Q_LORA + KV_LORA, Q_LORA + KV_LORA + QK_ROPE, Q_LORA + KV_LORA + QK_ROPE + MEM_WIDTH
    w["a_wlat"] = jnp.concatenate([w_in_a[:, :, :o2], _head_lanes(None, w_in_a[:, :, o2:o3])], axis=-1).astype(BF16)
    w["a_wqm"] = w_in_a[:, :, o3:o4].astype(BF16)
    w["a_wgate"] = w_in_a[:, :, o4:].astype(BF16)
    w["a_qlat_g"] = a_qlat_g[:, None, :]
    w["a_kvlat_g"] = a_kvlat_g[:, None, :]
    uq = a_w_uq.reshape(na, Q_LORA, MLA_HEADS, QK_HEAD)
    wuq = _head_lanes(uq[..., :QK_NOPE], uq[..., QK_NOPE:]).reshape(na, Q_LORA, MLA_HEADS * LANE)
    w["a_wuqt"] = jnp.swapaxes(wuq, 1, 2).astype(BF16)
    ukv =a_w_ukv.reshape(na, KV_LORA, MLA_HEADS, QK_NOPE + V_HEAD)
    w["a_wk"] = _head_lanes(ukv[..., :QK_NOPE], None).reshape(na, KV_LORA, MLA_HEADS * LANE).astype(BF16)
    w["a_wvt"] = jnp.swapaxes(ukv[..., QK_NOPE:].reshape(na, KV_LORA, MLA_WIDTH), 1, 2).astype(BF16)

    def norm_gain_rows(g):
        g_c = _head_lanes(g[:, :QK_NOPE], g[:, QK_NOPE:])
        return jnp.stack([g_c, jnp.roll(g_c, HALF_LANE, axis=-1)], axis=1)

    w["a_qg"] = jnp.broadcast_to(norm_gain_rows(a_q_norm_g)[..., None], (na, 2, LANE, LANE))
    w["a_kg"] = norm_gain_rows(a_k_norm_g)
    g_max = jnp.max(jnp.abs(a_q_norm_g), axis=1) * jnp.max(jnp.abs(a_k_norm_g), axis=1)
    w["a_m2"] = (QK_HEAD ** 0.5 * LOG2E * g_max)[:, None]

    w["b_win"] = w_in_b.astype(BF16)
    w["b_vg"] = b_v_norm_g[:, None, :]
    w["b_ws"] = b_w_s.astype(BF16)
    w["b_bias"] = jnp.repeat(jnp.swapaxes(b_bias, 1, 2), SG_GDIM, axis=2)

    w["mem_wk"] = w_mem_kv[:, :, :MEM_WIDTH].astype(BF16)
    w["mem_wv"] = w_mem_kv[:, :, MEM_WIDTH:].astype(BF16)
    w["mem_m2"] = (MEM_HDIM ** 0.5 * LOG2E * jnp.max(jnp.abs(mem_q_norm_g), axis=1)
                   * jnp.max(jnp.abs(mem_k_norm_g), axis=1))
    w["mq_g"] = jnp.tile(mem_q_norm_g, (1, MEM_HEADS))[:, None, :]
    w["mk_g"] = jnp.tile(mem_k_norm_g, (1, MEM_HEADS))[:, None, :]
    w["wout_a"] = w_out[0::2].astype(BF16)
    w["wout_b"] = w_out[1::2].astype(BF16)
    return w


def _rope_tables(seq):
    inv = 1.0 / (ROPE_BASE ** (jnp.arange(0, QK_ROPE, 2, dtype=F32) / QK_ROPE))
    ang = jnp.arange(seq, dtype=F32)[:, None] * inv[None, :]
    cos, sin = jnp.cos(ang), jnp.sin(ang)
    ct = _head_lanes(jnp.ones((seq, QK_NOPE), F32), jnp.concatenate([cos, cos], axis=1))
    st = _head_lanes(None, jnp.concatenate([-sin, sin], axis=1))
    return ct, st, ct.T, st.T


def _trunk(x, mem, rope, w, bounded):
    kmem, vmem = _mem_kv(mem, w["mem_g"], w["mem_wk"], w["mem_wv"], w["mk_g"])
    for j in range(DEPTH // 2):
        qt, k, vt, sg, o_mem = _pre_a(x, rope, w, j, 2 * j, kmem, vmem, bounded)
        o_gated = _attention(qt, k, vt, sg, w["a_m2"][j], bounded)
        x = _post_b(x, o_gated, o_mem, w, j, 2 * j + 1, kmem, vmem, bounded)
    return x


def kernel(x_prompt, x_sample, mem_prompt, mem_sample, ln_g, w_in_a, a_qlat_g, a_kvlat_g, a_w_uq, a_w_ukv,
           a_q_norm_g, a_k_norm_g, w_in_b, b_v_norm_g, b_w_s, b_bias, mem_norm_g, w_mem_kv, mem_q_norm_g,
           mem_k_norm_g, w_out):
    w = _prep_weights(ln_g, w_in_a, a_qlat_g, a_kvlat_g, a_w_uq, a_w_ukv, a_q_norm_g, a_k_norm_g, w_in_b,
                      b_v_norm_g, b_w_s, b_bias, mem_norm_g, w_mem_kv, mem_q_norm_g, mem_k_norm_g, w_out)

    rope = _rope_tables(max(x_prompt.shape[1], x_sample.shape[1]))

    def run(bounded):
        return lambda ops: (_trunk(ops[0], ops[1], rope, w, bounded), _trunk(ops[2], ops[3], rope, w, bounded))

    shifts_ok = jnp.maximum(jnp.max(w["a_m2"]), jnp.max(w["mem_m2"])) <= MAX_FIXED_SHIFT
    return lax.cond(shifts_ok, run(True), run(False), (x_prompt, mem_prompt, x_sample, mem_sample))
```

```python
import functools
import math

import jax
import jax.numpy as jnp
from jax import lax
from jax.experimental import pallas as pl
from jax.experimental.pallas import tpu as pltpu

D_MODEL = 1024
DEPTH = 4
EPS = 1e-6
N_MEM = 256
MEM_HEADS = 4
MEM_HDIM = 64
MEM_WIDTH = MEM_HEADS * MEM_HDIM
MLA_HEADS = 12
QK_NOPE = 64
QK_ROPE = 32
QK_HEAD = QK_NOPE + QK_ROPE
V_HEAD = 64
Q_LORA = 384
KV_LORA = 256
MLA_WIDTH = MLA_HEADS * V_HEAD
ROPE_BASE = 10000.0
CHUNK = 128
SG_GROUPS = 8
SG_WIDTH = 768
SG_GDIM = SG_WIDTH // SG_GROUPS
BRANCH = MLA_WIDTH + MEM_WIDTH

LANE = 128
HALF_LANE = LANE // 2
HALF_ROPE = QK_ROPE // 2

ROW_TILE = 512
ATTN_TQ = 2048
ATTN_TK = 2048
SAFE_TQ = 512
VMEM_LIMIT = 56 * 1024 * 1024
LOG2E = 1.4426950408889634
MAX_FIXED_SHIFT = 40.0

BF16 = jnp.bfloat16
F32 = jnp.float32


def _rsqrt_mean(x, n):
    return lax.rsqrt(jnp.sum(x * x, axis=-1, keepdims=True) * (1.0 / n) + EPS)


def _silu(g):
    return g / (1.0 + jnp.exp(-g))


def _gelu_tanh(x):
    c = math.sqrt(2.0 / math.pi)
    return 0.5 * x * (1.0 + jnp.tanh(c * (x + 0.044715 * (x * x * x))))


def _dot(a, b):
    return jnp.dot(a, b, preferred_element_type=F32)


def _dot_nt(a, b):
    return lax.dot_general(a, b, (((1,), (1,)), ((), ())), preferred_element_type=F32)


def _low_half(shape):
    return lax.broadcasted_iota(jnp.int32, shape, 1) < HALF_LANE


def _norm_heads64(x, g):
    out = []
    for a in range(MEM_WIDTH // LANE):
        blk = x[:, a * LANE:(a + 1) * LANE]
        lo = _low_half(blk.shape)
        sq = blk * blk
        s_lo = jnp.sum(jnp.where(lo, sq, 0.0), axis=-1, keepdims=True)
        s_hi = jnp.sum(jnp.where(lo, 0.0, sq), axis=-1, keepdims=True)
        r = jnp.where(lo, lax.rsqrt(s_lo * (1.0 / MEM_HDIM) + EPS), lax.rsqrt(s_hi * (1.0 / MEM_HDIM) + EPS))
        out.append(blk * r * g[:, a * LANE:(a + 1) * LANE])
    return out


def _mem_kv_kernel(mem_ref, g_ref, wk_ref, wv_ref, kg_ref, k_out, v_out):
    m = mem_ref[0]
    mn = (m * _rsqrt_mean(m, D_MODEL) * g_ref[0]).astype(BF16)
    kn = _norm_heads64(_dot(mn, wk_ref[0]), kg_ref[0])
    k_out[0, 0] = jnp.concatenate(kn, axis=1).astype(BF16)
    vv = _dot(mn, wv_ref[0])
    lane = lax.broadcasted_iota(jnp.int32, (N_MEM, MEM_WIDTH), 1)
    for h in range(MEM_HEADS):
        vh = jnp.where(lane >= h * MEM_HDIM, jnp.where(lane < (h + 1) * MEM_HDIM, vv, 0.0), 0.0)
        v_out[0, 0, h] = vh.astype(BF16)


def _mem_kv(mem, mem_g, wk, wv, kg):
    B = mem.shape[0]
    return pl.pallas_call(
        _mem_kv_kernel,
        grid=(DEPTH, B),
        in_specs=[
            pl.BlockSpec((1, N_MEM, D_MODEL), lambda l, b: (b, 0, 0)),
            pl.BlockSpec((1, 1, D_MODEL), lambda l, b: (l, 0, 0)),
            pl.BlockSpec((1, D_MODEL, MEM_WIDTH), lambda l, b: (l, 0, 0)),
            pl.BlockSpec((1, D_MODEL, MEM_WIDTH), lambda l, b: (l, 0, 0)),
            pl.BlockSpec((1, 1, MEM_WIDTH), lambda l, b: (l, 0, 0)),
        ],
        out_specs=[
            pl.BlockSpec((1, 1, N_MEM, MEM_WIDTH), lambda l, b: (l, b, 0, 0)),
            pl.BlockSpec((1, 1, MEM_HEADS, N_MEM, MEM_WIDTH), lambda l, b: (l, b, 0, 0, 0)),
        ],
        out_shape=[
            jax.ShapeDtypeStruct((DEPTH, B, N_MEM, MEM_WIDTH), BF16),
            jax.ShapeDtypeStruct((DEPTH, B, MEM_HEADS, N_MEM, MEM_WIDTH), BF16),
        ],
        compiler_params=pltpu.CompilerParams(
            dimension_semantics=("arbitrary", "arbitrary"), vmem_limit_bytes=VMEM_LIMIT),
        name="mem_kv",
    )(mem, mem_g, wk, wv, kg)


def _mem_specs(layer):
    return [
        pl.BlockSpec((1, 1, N_MEM, MEM_WIDTH), lambda b, i: (layer, b, 0, 0)),
        pl.BlockSpec((1, 1, MEM_HEADS, N_MEM, MEM_WIDTH), lambda b, i: (layer, b, 0, 0, 0)),
    ]


def _memory_attention(qm, qg, kmem_ref, vmem_ref, shift):
    qn = _norm_heads64(qm, qg)
    o = None
    inv_l = []
    for h in range(MEM_HEADS):
        a, upper = divmod(h, 2)
        lo = _low_half(qn[a].shape)
        qh = jnp.where(lo, 0.0, qn[a]) if upper else jnp.where(lo, qn[a], 0.0)
        qh = (qh * (MEM_HDIM ** -0.5 * LOG2E)).astype(BF16)
        s = _dot_nt(qh, kmem_ref[0, 0, :, a * LANE:(a + 1) * LANE])
        if shift is None:
            e = jnp.exp2(s - jnp.max(s, axis=-1, keepdims=True))
            p = (e / jnp.sum(e, axis=-1, keepdims=True)).astype(BF16)
        else:
            e = jnp.exp2(s - shift)
            inv_l.append(1.0 / jnp.sum(e[:, :LANE] + e[:, LANE:], axis=-1, keepdims=True))
            p = e.astype(BF16)
        oh = _dot(p, vmem_ref[0, 0, h])
        o = oh if o is None else o + oh
    if shift is not None:
        lane = lax.broadcasted_iota(jnp.int32, o.shape, 1)
        scale = inv_l[MEM_HEADS - 1]
        for h in range(MEM_HEADS - 2, -1, -1):
            scale = jnp.where(lane < (h + 1) * MEM_HDIM, inv_l[h], scale)
        o = o * scale
    return o


def _mem_shift(mm2_ref, layer, bounded):
    return mm2_ref[layer] if bounded else None


def _pre_a_kernel(mm2_ref, x_ref, ct_ref, st_ref, ctt_ref, stt_ref, ln_ref, wlat_ref, wqm_ref, wgate_ref,
                  qlg_ref, kvlg_ref, wuqt_ref, wk_ref, wvt_ref, qg_ref, kg_ref, mqg_ref, kmem_ref, vmem_ref,
                  qt_out, k_out, vt_out, sg_out, om_out, *, layer, bounded):
    x = x_ref[0]
    h = (x * _rsqrt_mean(x, D_MODEL) * ln_ref[0]).astype(BF16)

    lat = _dot(h, wlat_ref[0])
    q_lat = lat[:, :Q_LORA]
    kv_lat = lat[:, Q_LORA:Q_LORA + KV_LORA]
    kpe = lat[:, Q_LORA + KV_LORA:]

    ct = ct_ref[...]
    st = st_ref[...]

    rows = x.shape[0]
    qn = (q_lat * _rsqrt_mean(q_lat, Q_LORA) * qlg_ref[0]).astype(BF16)
    qt_raw = _dot_nt(wuqt_ref[0], qn)
    q_c = ctt_ref[...] * jnp.tile(qg_ref[0, 0], (1, rows // LANE))
    q_s = stt_ref[...] * jnp.tile(qg_ref[0, 1], (1, rows // LANE))
    scale = QK_HEAD ** -0.5 * LOG2E
    for hd in range(MLA_HEADS):
        qh = qt_raw[hd * LANE:(hd + 1) * LANE, :]
        r = lax.rsqrt(jnp.sum(qh * qh, axis=0, keepdims=True) * (1.0 / QK_HEAD) + EPS)
        partner = jnp.concatenate([qh[HALF_LANE:], qh[:HALF_LANE]], axis=0)
        qt_out[0, hd] = ((qh * q_c + partner * q_s) * (r * scale)).astype(BF16)

    kvn = (kv_lat * _rsqrt_mean(kv_lat, KV_LORA) * kvlg_ref[0]).astype(BF16)
    k_nope = _dot(kvn, wk_ref[0])
    vt_out[0, 0] = _dot_nt(wvt_ref[0], kvn).astype(BF16)
    k_cg = kg_ref[0, 0:1]
    kpe_rot = kpe * (ct * k_cg) + pltpu.roll(kpe, HALF_LANE, 1) * (st * kg_ref[0, 1:2])
    pe_sq = jnp.sum(kpe * kpe, axis=-1, keepdims=True)
    for hd in range(MLA_HEADS):
        kh = k_nope[:, hd * LANE:(hd + 1) * LANE]
        r = lax.rsqrt((jnp.sum(kh * kh, axis=-1, keepdims=True) + pe_sq) * (1.0 / QK_HEAD) + EPS)
        k_out[0, hd] = ((kh * k_cg + kpe_rot) * r).astype(BF16)

    sg = _silu(_dot(h, wgate_ref[0]))
    sg_out[0] = sg[:, :MLA_WIDTH]
    o_mem = _memory_attention(_dot(h, wqm_ref[0]), mqg_ref[0], kmem_ref, vmem_ref,
                              _mem_shift(mm2_ref, layer, bounded))
    om_out[0] = (o_mem * sg[:, MLA_WIDTH:]).astype(BF16)


def _const_spec(shape, layer):
    nd = len(shape)
    return pl.BlockSpec((1,) + tuple(shape[1:]), lambda b, i: (layer,) + (0,) * (nd - 1))


def _pre_a(x, rope, w, j, layer, kmem, vmem, bounded):
    B, S, _ = x.shape
    tm = ROW_TILE
    kv_per_chunk = ATTN_TK // tm
    weights = [w["ln_g"], w["a_wlat"], w["a_wqm"], w["a_wgate"], w["a_qlat_g"], w["a_kvlat_g"],
               w["a_wuqt"], w["a_wk"], w["a_wvt"], w["a_qg"], w["a_kg"], w["mq_g"]]
    wlayer = [layer, j, j, j, j, j, j, j, j, j, j, layer]
    in_specs = [
        pl.BlockSpec(memory_space=pltpu.SMEM),
        pl.BlockSpec((1, tm, D_MODEL), lambda b, i: (b, i, 0)),
        pl.BlockSpec((tm, LANE), lambda b, i: (i, 0)),
        pl.BlockSpec((tm, LANE), lambda b, i: (i, 0)),
        pl.BlockSpec((LANE, tm), lambda b, i: (0, i)),
        pl.BlockSpec((LANE, tm), lambda b, i: (0, i)),
    ] + [_const_spec(a.shape, l) for a, l in zip(weights, wlayer)] + _mem_specs(layer)
    out_specs = [
        pl.BlockSpec((1, MLA_HEADS, LANE, tm), lambda b, i: (b, 0, 0, i)),
        pl.BlockSpec((1, MLA_HEADS, tm, LANE), lambda b, i: (b, 0, i, 0)),
        pl.BlockSpec((1, 1, MLA_WIDTH, tm), lambda b, i: (b, i // kv_per_chunk, 0, i % kv_per_chunk)),
        pl.BlockSpec((1, tm, MLA_WIDTH), lambda b, i: (b, i, 0)),
        pl.BlockSpec((1, tm, MEM_WIDTH), lambda b, i: (b, i, 0)),
    ]
    out_shape = [
        jax.ShapeDtypeStruct((B, MLA_HEADS, LANE, S), BF16),
        jax.ShapeDtypeStruct((B, MLA_HEADS, S, LANE), BF16),
        jax.ShapeDtypeStruct((B, S // ATTN_TK, MLA_WIDTH, ATTN_TK), BF16),
        jax.ShapeDtypeStruct((B, S, MLA_WIDTH), F32),
        jax.ShapeDtypeStruct((B, S, MEM_WIDTH), BF16),
    ]
    return pl.pallas_call(
        functools.partial(_pre_a_kernel, layer=layer, bounded=bounded),
        grid=(B, S // tm),
        in_specs=in_specs,
        out_specs=out_specs,
        out_shape=out_shape,
        compiler_params=pltpu.CompilerParams(
            dimension_semantics=("arbitrary", "arbitrary"), vmem_limit_bytes=VMEM_LIMIT),
        name="pre_a",
    )(w["mem_m2"], x, *rope, *weights, kmem, vmem)


def _attn_kernel(m2_ref, qt_ref, k_ref, vt_ref, sg_ref, o_ref, acc_ref, l_ref, m_ref, *, seq, bounded):
    tq, tk = o_ref.shape[1], ATTN_TK
    acc_ref[...] = jnp.zeros_like(acc_ref)
    l_ref[...] = jnp.zeros_like(l_ref)
    if not bounded:
        m_ref[...] = jnp.full_like(m_ref, -jnp.inf)

    def body(j, carry):
        for hh in range(2):
            k = k_ref[0, hh, pl.ds(pl.multiple_of(j * tk, tk), tk), :]
            st = _dot(k, qt_ref[0, hh])
            vt = vt_ref[0, j, hh * V_HEAD:(hh + 1) * V_HEAD, :]
            if bounded:
                pt = jnp.exp2(st - m2_ref[0])
                l_ref[hh] += jnp.sum(pt.reshape(tk // 8, 8, tq), axis=0)
                acc_ref[hh] += _dot(vt, pt.astype(BF16))
            else:
                m_old = m_ref[hh]
                m_new = jnp.maximum(m_old, jnp.max(st, axis=0, keepdims=True))
                alpha = jnp.exp2(m_old - m_new)
                pt = jnp.exp2(st - m_new)
                l_ref[hh] = alpha * l_ref[hh] + jnp.sum(pt.reshape(tk // 8, 8, tq), axis=0)
                acc_ref[hh] = alpha * acc_ref[hh] + _dot(vt, pt.astype(BF16))
                m_ref[hh] = m_new
        return carry

    lax.fori_loop(0, seq // tk, body, 0, unroll=2)
    ot = [acc_ref[hh] / jnp.sum(l_ref[hh], axis=0, keepdims=True) for hh in range(2)]
    o = jnp.concatenate(ot, axis=0).T
    o_ref[0] = (o * sg_ref[0]).astype(BF16)


def _attention(qt, k, vt, sg, m2, bounded):
    B, _, S, _ = k.shape
    tq = ATTN_TQ if bounded else SAFE_TQ
    return pl.pallas_call(
        functools.partial(_attn_kernel, seq=S, bounded=bounded),
        grid=(B, MLA_HEADS // 2, S // tq),
        in_specs=[
            pl.BlockSpec(memory_space=pltpu.SMEM),
            pl.BlockSpec((1, 2, LANE, tq), lambda b, hp, i: (b, hp, 0, i)),
            pl.BlockSpec((1, 2, S, LANE), lambda b, hp, i: (b, hp, 0, 0)),
            pl.BlockSpec((1, S // ATTN_TK, LANE, ATTN_TK), lambda b, hp, i: (b, 0, hp, 0)),
            pl.BlockSpec((1, tq, LANE), lambda b, hp, i: (b, i, hp)),
        ],
        out_specs=pl.BlockSpec((1, tq, LANE), lambda b, hp, i: (b, i, hp)),
        out_shape=jax.ShapeDtypeStruct((B, S, MLA_WIDTH), BF16),
        scratch_shapes=[pltpu.VMEM((2, V_HEAD, tq), F32), pltpu.VMEM((2, 8, tq), F32),
                        pltpu.VMEM((2, 1, tq), F32)],
        compiler_params=pltpu.CompilerParams(
            dimension_semantics=("arbitrary", "arbitrary", "arbitrary"), vmem_limit_bytes=VMEM_LIMIT),
        name="attn",
    )(m2, qt, k, vt, sg)


def _out_proj(x, o_mix, o_mem, wout_ref):
    return x + _dot(o_mix, wout_ref[0, :MLA_WIDTH, :]) + _dot(o_mem, wout_ref[0, MLA_WIDTH:, :])


def _spatial_mix(vn, ws_ref, bias):
    rows = vn.shape[0]
    n_blk = SG_WIDTH // LANE
    masked = []
    for g in range(SG_GROUPS):
        lo, hi = g * SG_GDIM, (g + 1) * SG_GDIM
        b0, b1 = lo // LANE, (hi - 1) // LANE
        blk = vn[:, b0 * LANE:(b1 + 1) * LANE]
        col = lax.broadcasted_iota(jnp.int32, blk.shape, 1) + b0 * LANE
        masked.append((b0, b1, jnp.where(col >= lo, jnp.where(col < hi, blk, 0.0), 0.0).astype(BF16)))
    chunks = []
    for c in range(rows // CHUNK):
        parts = [[] for _ in range(n_blk)]
        for g, (b0, b1, mv) in enumerate(masked):
            res = _dot(ws_ref[0, g], mv[c * CHUNK:(c + 1) * CHUNK])
            for b in range(b0, b1 + 1):
                parts[b].append(res[:, (b - b0) * LANE:(b - b0 + 1) * LANE])
        chunks.append(jnp.concatenate([functools.reduce(lambda a, b: a + b, p) for p in parts], axis=1) + bias)
    return jnp.concatenate(chunks, axis=0)


def _post_b_kernel(mm2_ref, x_ref, og_ref, omem_ref, wout_a_ref, ln_ref, win_ref, vg_ref, ws_ref, bias_ref,
                   mqg_ref, kmem_ref, vmem_ref, wout_ref, y_ref, *, layer, bounded):
    y_ref[0] = _out_proj(x_ref[0], og_ref[0], omem_ref[0], wout_a_ref)
    x = y_ref[0]
    h = (x * _rsqrt_mean(x, D_MODEL) * ln_ref[0]).astype(BF16)
    u = _gelu_tanh(_dot(h, win_ref[0, :, :SG_WIDTH]))
    v = _gelu_tanh(_dot(h, win_ref[0, :, SG_WIDTH:2 * SG_WIDTH]))
    vn = v * _rsqrt_mean(v, SG_WIDTH) * vg_ref[0]
    sg = _silu(_dot(h, win_ref[0, :, 2 * SG_WIDTH + MEM_WIDTH:]))
    o_mix = (u * _spatial_mix(vn, ws_ref, bias_ref[0]) * sg[:, :SG_WIDTH]).astype(BF16)
    qm = _dot(h, win_ref[0, :, 2 * SG_WIDTH:2 * SG_WIDTH + MEM_WIDTH])
    o_mem = _memory_attention(qm, mqg_ref[0], kmem_ref, vmem_ref, _mem_shift(mm2_ref, layer, bounded))
    o_mem = (o_mem * sg[:, SG_WIDTH:]).astype(BF16)
    y_ref[0] = _out_proj(x, o_mix, o_mem, wout_ref)


def _post_b(x, o_gated, o_mem, w, j, layer, kmem, vmem, bounded):
    B, S, _ = x.shape
    tm = ROW_TILE
    weights = [w["wout_a"], w["ln_g"], w["b_win"], w["b_vg"], w["b_ws"], w["b_bias"], w["mq_g"]]
    wlayer = [j, layer, j, j, j, j, layer]
    in_specs = [
        pl.BlockSpec(memory_space=pltpu.SMEM),
        pl.BlockSpec((1, tm, D_MODEL), lambda b, i: (b, i, 0)),
        pl.BlockSpec((1, tm, MLA_WIDTH), lambda b, i: (b, i, 0)),
        pl.BlockSpec((1, tm, MEM_WIDTH), lambda b, i: (b, i, 0)),
    ]
    in_specs += [_const_spec(a.shape, l) for a, l in zip(weights, wlayer)]
    in_specs += _mem_specs(layer) + [_const_spec(w["wout_b"].shape, j)]
    return pl.pallas_call(
        functools.partial(_post_b_kernel, layer=layer, bounded=bounded),
        grid=(B, S // tm),
        in_specs=in_specs,
        out_specs=pl.BlockSpec((1, tm, D_MODEL), lambda b, i: (b, i, 0)),
        out_shape=jax.ShapeDtypeStruct((B, S, D_MODEL), F32),
        compiler_params=pltpu.CompilerParams(
            dimension_semantics=("arbitrary", "arbitrary"), vmem_limit_bytes=VMEM_LIMIT),
        name="post_b",
    )(w["mem_m2"], x, o_gated, o_mem, *weights, kmem, vmem, w["wout_b"])


def _head_lanes(nope, rope, axis=-1):
    ref = nope if nope is not None else rope
    axis = axis % ref.ndim

    def z(n):
        return jnp.zeros(ref.shape[:axis] + (n,) + ref.shape[axis + 1:], ref.dtype)

    def cut(a, lo, hi):
        return lax.slice_in_dim(a, lo, hi, axis=axis)

    n0, n1 = (cut(nope, 0, 48), cut(nope, 48, QK_NOPE)) if nope is not None else (z(48), z(QK_NOPE - 48))
    r0, r1 = ((cut(rope, 0, HALF_ROPE), cut(rope, HALF_ROPE, QK_ROPE)) if rope is not None
              else (z(HALF_ROPE), z(HALF_ROPE)))
    return jnp.concatenate([r0, n0, r1, n1, z(LANE - QK_HEAD)], axis=axis)


def _prep_weights(ln_g, w_in_a, a_qlat_g, a_kvlat_g, a_w_uq, a_w_ukv, a_q_norm_g, a_k_norm_g, w_in_b,
                  b_v_norm_g, b_w_s, b_bias, mem_norm_g, w_mem_kv, mem_q_norm_g, mem_k_norm_g, w_out):
    w = {}
    na = w_in_a.shape[0]
    w["ln_g"] = ln_g[:, None, :]
    w["mem_g"] = mem_norm_g[:, None, :]

    o2, o3, o4 = Q_LORA + KV_LORA, Q_LORA + KV_LORA + QK_ROPE, Q_LORA + KV_LORA + QK_ROPE + MEM_WIDTH
    w["a_wlat"] = jnp.concatenate([w_in_a[:, :, :o2], _head_lanes(None, w_in_a[:, :, o2:o3])], axis=-1).astype(BF16)
    w["a_wqm"] = w_in_a[:, :, o3:o4].astype(BF16)
    w["a_wgate"] = w_in_a[:, :, o4:].astype(BF16)
    w["a_qlat_g"] = a_qlat_g[:, None, :]
    w["a_kvlat_g"] = a_kvlat_g[:, None, :]
    uq = a_w_uq.reshape(na, Q_LORA, MLA_HEADS, QK_HEAD)
    wuq = _head_lanes(uq[..., :QK_NOPE], uq[..., QK_NOPE:]).reshape(na, Q_LORA, MLA_HEADS * LANE)
    w["a_wuqt"] = jnp.swapaxes(wuq, 1, 2).astype(BF16)
    ukv =a_w_ukv.reshape(na, KV_LORA, MLA_HEADS, QK_NOPE + V_HEAD)
    w["a_wk"] = _head_lanes(ukv[..., :QK_NOPE], None).reshape(na, KV_LORA, MLA_HEADS * LANE).astype(BF16)
    w["a_wvt"] = jnp.swapaxes(ukv[..., QK_NOPE:].reshape(na, KV_LORA, MLA_WIDTH), 1, 2).astype(BF16)

    def norm_gain_rows(g):
        g_c = _head_lanes(g[:, :QK_NOPE], g[:, QK_NOPE:])
        return jnp.stack([g_c, jnp.roll(g_c, HALF_LANE, axis=-1)], axis=1)

    w["a_qg"] = jnp.broadcast_to(norm_gain_rows(a_q_norm_g)[..., None], (na, 2, LANE, LANE))
    w["a_kg"] = norm_gain_rows(a_k_norm_g)
    g_max = jnp.max(jnp.abs(a_q_norm_g), axis=1) * jnp.max(jnp.abs(a_k_norm_g), axis=1)
    w["a_m2"] = (QK_HEAD ** 0.5 * LOG2E * g_max)[:, None]

    w["b_win"] = w_in_b.astype(BF16)
    w["b_vg"] = b_v_norm_g[:, None, :]
    w["b_ws"] = b_w_s.astype(BF16)
    w["b_bias"] = jnp.repeat(jnp.swapaxes(b_bias, 1, 2), SG_GDIM, axis=2)

    w["mem_wk"] = w_mem_kv[:, :, :MEM_WIDTH].astype(BF16)
    w["mem_wv"] = w_mem_kv[:, :, MEM_WIDTH:].astype(BF16)
    w["mem_m2"] = (MEM_HDIM ** 0.5 * LOG2E * jnp.max(jnp.abs(mem_q_norm_g), axis=1)
                   * jnp.max(jnp.abs(mem_k_norm_g), axis=1))
    w["mq_g"] = jnp.tile(mem_q_norm_g, (1, MEM_HEADS))[:, None, :]
    w["mk_g"] = jnp.tile(mem_k_norm_g, (1, MEM_HEADS))[:, None, :]
    w["wout_a"] = w_out[0::2].astype(BF16)
    w["wout_b"] = w_out[1::2].astype(BF16)
    return w


def _rope_tables(seq):
    inv = 1.0 / (ROPE_BASE ** (jnp.arange(0, QK_ROPE, 2, dtype=F32) / QK_ROPE))
    ang = inv[:, None] * jnp.arange(seq, dtype=F32)[None, :]
    cos, sin = jnp.cos(ang), jnp.sin(ang)
    ctt = _head_lanes(jnp.ones((QK_NOPE, seq), F32), jnp.concatenate([cos, cos], axis=0), axis=0)
    stt = _head_lanes(None, jnp.concatenate([-sin, sin], axis=0), axis=0)
    return ctt.T, stt.T, ctt, stt


def _trunk(x, mem, rope, w, bounded):
    kmem, vmem = _mem_kv(mem, w["mem_g"], w["mem_wk"], w["mem_wv"], w["mk_g"])
    for j in range(DEPTH // 2):
        qt, k, vt, sg, o_mem = _pre_a(x, rope, w, j, 2 * j, kmem, vmem, bounded)
        o_gated = _attention(qt, k, vt, sg, w["a_m2"][j], bounded)
        x = _post_b(x, o_gated, o_mem, w, j, 2 * j + 1, kmem, vmem, bounded)
    return x


def kernel(x_prompt, x_sample, mem_prompt, mem_sample, ln_g, w_in_a, a_qlat_g, a_kvlat_g, a_w_uq, a_w_ukv,
           a_q_norm_g, a_k_norm_g, w_in_b, b_v_norm_g, b_w_s, b_bias, mem_norm_g, w_mem_kv, mem_q_norm_g,
           mem_k_norm_g, w_out):
    w = _prep_weights(ln_g, w_in_a, a_qlat_g, a_kvlat_g, a_w_uq, a_w_ukv, a_q_norm_g, a_k_norm_g, w_in_b,
                      b_v_norm_g, b_w_s, b_bias, mem_norm_g, w_mem_kv, mem_q_norm_g, mem_k_norm_g, w_out)

    rope = _rope_tables(max(x_prompt.shape[1], x_sample.shape[1]))

    def run(bounded):
        return lambda ops: (_trunk(ops[0], ops[1], rope, w, bounded), _trunk(ops[2], ops[3], rope, w, bounded))

    shifts_ok = jnp.maximum(jnp.max(w["a_m2"]), jnp.max(w["mem_m2"])) <= MAX_FIXED_SHIFT
    return lax.cond(shifts_ok, run(True), run(False), (x_prompt, mem_prompt, x_sample, mem_sample))
```

```python
import functools
import math

import jax
import jax.numpy as jnp
from jax import lax
from jax.experimental import pallas as pl
from jax.experimental.pallas import tpu as pltpu

D_MODEL = 1024
DEPTH = 4
EPS = 1e-6
N_MEM = 256
MEM_HEADS = 4
MEM_HDIM = 64
MEM_WIDTH = MEM_HEADS * MEM_HDIM
MLA_HEADS = 12
QK_NOPE = 64
QK_ROPE = 32
QK_HEAD = QK_NOPE + QK_ROPE
V_HEAD = 64
Q_LORA = 384
KV_LORA = 256
MLA_WIDTH = MLA_HEADS * V_HEAD
ROPE_BASE = 10000.0
CHUNK = 128
SG_GROUPS = 8
SG_WIDTH = 768
SG_GDIM = SG_WIDTH // SG_GROUPS
BRANCH = MLA_WIDTH + MEM_WIDTH

LANE = 128
HALF_LANE = LANE // 2
HALF_ROPE = QK_ROPE // 2

ROW_TILE = 1024
ATTN_TQ = 2048
ATTN_TK = 2048
SAFE_TQ = 512
VMEM_LIMIT = 56 * 1024 * 1024
LOG2E = 1.4426950408889634
MAX_FIXED_SHIFT = 40.0

BF16 = jnp.bfloat16
F32 = jnp.float32


def _rsqrt_mean(x, n):
    return lax.rsqrt(jnp.sum(x * x, axis=-1, keepdims=True) * (1.0 / n) + EPS)


def _silu(g):
    return g / (1.0 + jnp.exp(-g))


def _gelu_tanh(x):
    c = math.sqrt(2.0 / math.pi)
    return 0.5 * x * (1.0 + jnp.tanh(c * (x + 0.044715 * (x * x * x))))


def _dot(a, b):
    return jnp.dot(a, b, preferred_element_type=F32)


def _dot_nt(a, b):
    return lax.dot_general(a, b, (((1,), (1,)), ((), ())), preferred_element_type=F32)


def _low_half(shape):
    return lax.broadcasted_iota(jnp.int32, shape, 1) < HALF_LANE


def _norm_heads64(x, g):
    out = []
    for a in range(MEM_WIDTH // LANE):
        blk = x[:, a * LANE:(a + 1) * LANE]
        lo = _low_half(blk.shape)
        sq = blk * blk
        s_lo = jnp.sum(jnp.where(lo, sq, 0.0), axis=-1, keepdims=True)
        s_hi = jnp.sum(jnp.where(lo, 0.0, sq), axis=-1, keepdims=True)
        r = jnp.where(lo, lax.rsqrt(s_lo * (1.0 / MEM_HDIM) + EPS), lax.rsqrt(s_hi * (1.0 / MEM_HDIM) + EPS))
        out.append(blk * r * g[:, a * LANE:(a + 1) * LANE])
    return out


def _mem_kv_kernel(mem_ref, g_ref, wk_ref, wv_ref, kg_ref, k_out, v_out):
    m = mem_ref[0]
    mn = (m * _rsqrt_mean(m, D_MODEL) * g_ref[0]).astype(BF16)
    kn = _norm_heads64(_dot(mn, wk_ref[0]), kg_ref[0])
    k_out[0, 0] = jnp.concatenate(kn, axis=1).astype(BF16)
    vv = _dot(mn, wv_ref[0])
    lane = lax.broadcasted_iota(jnp.int32, (N_MEM, MEM_WIDTH), 1)
    for h in range(MEM_HEADS):
        vh = jnp.where(lane >= h * MEM_HDIM, jnp.where(lane < (h + 1) * MEM_HDIM, vv, 0.0), 0.0)
        v_out[0, 0, h] = vh.astype(BF16)


def _mem_kv(mem, mem_g, wk, wv, kg):
    B = mem.shape[0]
    return pl.pallas_call(
        _mem_kv_kernel,
        grid=(DEPTH, B),
        in_specs=[
            pl.BlockSpec((1, N_MEM, D_MODEL), lambda l, b: (b, 0, 0)),
            pl.BlockSpec((1, 1, D_MODEL), lambda l, b: (l, 0, 0)),
            pl.BlockSpec((1, D_MODEL, MEM_WIDTH), lambda l, b: (l, 0, 0)),
            pl.BlockSpec((1, D_MODEL, MEM_WIDTH), lambda l, b: (l, 0, 0)),
            pl.BlockSpec((1, 1, MEM_WIDTH), lambda l, b: (l, 0, 0)),
        ],
        out_specs=[
            pl.BlockSpec((1, 1, N_MEM, MEM_WIDTH), lambda l, b: (l, b, 0, 0)),
            pl.BlockSpec((1, 1, MEM_HEADS, N_MEM, MEM_WIDTH), lambda l, b: (l, b, 0, 0, 0)),
        ],
        out_shape=[
            jax.ShapeDtypeStruct((DEPTH, B, N_MEM, MEM_WIDTH), BF16),
            jax.ShapeDtypeStruct((DEPTH, B, MEM_HEADS, N_MEM, MEM_WIDTH), BF16),
        ],
        compiler_params=pltpu.CompilerParams(
            dimension_semantics=("arbitrary", "arbitrary"), vmem_limit_bytes=VMEM_LIMIT),
        name="mem_kv",
    )(mem, mem_g, wk, wv, kg)


def _mem_specs(layer):
    return [
        pl.BlockSpec((1, 1, N_MEM, MEM_WIDTH), lambda b, i: (layer, b, 0, 0)),
        pl.BlockSpec((1, 1, MEM_HEADS, N_MEM, MEM_WIDTH), lambda b, i: (layer, b, 0, 0, 0)),
    ]


def _memory_attention(qm, qg, kmem_ref, vmem_ref, shift):
    qn = _norm_heads64(qm, qg)
    o = None
    inv_l = []
    for h in range(MEM_HEADS):
        a, upper = divmod(h, 2)
        lo = _low_half(qn[a].shape)
        qh = jnp.where(lo, 0.0, qn[a]) if upper else jnp.where(lo, qn[a], 0.0)
        qh = (qh * (MEM_HDIM ** -0.5 * LOG2E)).astype(BF16)
        s = _dot_nt(qh, kmem_ref[0, 0, :, a * LANE:(a + 1) * LANE])
        if shift is None:
            e = jnp.exp2(s - jnp.max(s, axis=-1, keepdims=True))
            p = (e / jnp.sum(e, axis=-1, keepdims=True)).astype(BF16)
        else:
            e = jnp.exp2(s - shift)
            inv_l.append(1.0 / jnp.sum(e[:, :LANE] + e[:, LANE:], axis=-1, keepdims=True))
            p = e.astype(BF16)
        oh = _dot(p, vmem_ref[0, 0, h])
        o = oh if o is None else o + oh
    if shift is not None:
        lane = lax.broadcasted_iota(jnp.int32, o.shape, 1)
        scale = inv_l[MEM_HEADS - 1]
        for h in range(MEM_HEADS - 2, -1, -1):
            scale = jnp.where(lane < (h + 1) * MEM_HDIM, inv_l[h], scale)
        o = o * scale
    return o


def _mem_shift(mm2_ref, layer, bounded):
    return mm2_ref[layer] if bounded else None


def _pre_a_kernel(mm2_ref, x_ref, ct_ref, st_ref, ctt_ref, stt_ref, ln_ref, wlat_ref, wqm_ref, wgate_ref,
                  qlg_ref, kvlg_ref, wuqt_ref, wk_ref, wvt_ref, qg_ref, kg_ref, mqg_ref, kmem_ref, vmem_ref,
                  qt_out, k_out, vt_out, sg_out, om_out, *, layer, bounded):
    x = x_ref[0]
    h = (x * _rsqrt_mean(x, D_MODEL) * ln_ref[0]).astype(BF16)

    lat = _dot(h, wlat_ref[0])
    q_lat = lat[:, :Q_LORA]
    kv_lat = lat[:, Q_LORA:Q_LORA + KV_LORA]
    kpe = lat[:, Q_LORA + KV_LORA:]

    ct = ct_ref[...]
    st = st_ref[...]

    rows = x.shape[0]
    qn = (q_lat * _rsqrt_mean(q_lat, Q_LORA) * qlg_ref[0]).astype(BF16)
    qt_raw = _dot_nt(wuqt_ref[0], qn)
    q_c = ctt_ref[...] * jnp.tile(qg_ref[0, 0], (1, rows // LANE))
    q_s = stt_ref[...] * jnp.tile(qg_ref[0, 1], (1, rows // LANE))
    scale = QK_HEAD ** -0.5 * LOG2E
    for hd in range(MLA_HEADS):
        qh = qt_raw[hd * LANE:(hd + 1) * LANE, :]
        r = lax.rsqrt(jnp.sum(qh * qh, axis=0, keepdims=True) * (1.0 / QK_HEAD) + EPS)
        partner = jnp.concatenate([qh[HALF_LANE:], qh[:HALF_LANE]], axis=0)
        qt_out[0, hd] = ((qh * q_c + partner * q_s) * (r * scale)).astype(BF16)

    kvn = (kv_lat * _rsqrt_mean(kv_lat, KV_LORA) * kvlg_ref[0]).astype(BF16)
    k_nope = _dot(kvn, wk_ref[0])
    vt_out[0, 0] = _dot_nt(wvt_ref[0], kvn).astype(BF16)
    k_cg = kg_ref[0, 0:1]
    kpe_rot = kpe * (ct * k_cg) + pltpu.roll(kpe, HALF_LANE, 1) * (st * kg_ref[0, 1:2])
    pe_sq = jnp.sum(kpe * kpe, axis=-1, keepdims=True)
    for hd in range(MLA_HEADS):
        kh = k_nope[:, hd * LANE:(hd + 1) * LANE]
        r = lax.rsqrt((jnp.sum(kh * kh, axis=-1, keepdims=True) + pe_sq) * (1.0 / QK_HEAD) + EPS)
        k_out[0, hd] = ((kh * k_cg + kpe_rot) * r).astype(BF16)

    sg = _silu(_dot(h, wgate_ref[0]))
    sg_out[0] = sg[:, :MLA_WIDTH]
    o_mem = _memory_attention(_dot(h, wqm_ref[0]), mqg_ref[0], kmem_ref, vmem_ref,
                              _mem_shift(mm2_ref, layer, bounded))
    om_out[0] = (o_mem * sg[:, MLA_WIDTH:]).astype(BF16)


def _const_spec(shape, layer):
    nd = len(shape)
    return pl.BlockSpec((1,) + tuple(shape[1:]), lambda b, i: (layer,) + (0,) * (nd - 1))


def _pre_a(x, rope, w, j, layer, kmem, vmem, bounded):
    B, S, _ = x.shape
    tm = ROW_TILE
    kv_per_chunk = ATTN_TK // tm
    weights = [w["ln_g"], w["a_wlat"], w["a_wqm"], w["a_wgate"], w["a_qlat_g"], w["a_kvlat_g"],
               w["a_wuqt"], w["a_wk"], w["a_wvt"], w["a_qg"], w["a_kg"], w["mq_g"]]
    wlayer = [layer, j, j, j, j, j, j, j, j, j, j, layer]
    in_specs = [
        pl.BlockSpec(memory_space=pltpu.SMEM),
        pl.BlockSpec((1, tm, D_MODEL), lambda b, i: (b, i, 0)),
        pl.BlockSpec((tm, LANE), lambda b, i: (i, 0)),
        pl.BlockSpec((tm, LANE), lambda b, i: (i, 0)),
        pl.BlockSpec((LANE, tm), lambda b, i: (0, i)),
        pl.BlockSpec((LANE, tm), lambda b, i: (0, i)),
    ] + [_const_spec(a.shape, l) for a, l in zip(weights, wlayer)] + _mem_specs(layer)
    out_specs = [
        pl.BlockSpec((1, MLA_HEADS, LANE, tm), lambda b, i: (b, 0, 0, i)),
        pl.BlockSpec((1, MLA_HEADS, tm, LANE), lambda b, i: (b, 0, i, 0)),
        pl.BlockSpec((1, 1, MLA_WIDTH, tm), lambda b, i: (b, i // kv_per_chunk, 0, i % kv_per_chunk)),
        pl.BlockSpec((1, tm, MLA_WIDTH), lambda b, i: (b, i, 0)),
        pl.BlockSpec((1, tm, MEM_WIDTH), lambda b, i: (b, i, 0)),
    ]
    out_shape = [
        jax.ShapeDtypeStruct((B, MLA_HEADS, LANE, S), BF16),
        jax.ShapeDtypeStruct((B, MLA_HEADS, S, LANE), BF16),
        jax.ShapeDtypeStruct((B, S // ATTN_TK, MLA_WIDTH, ATTN_TK), BF16),
        jax.ShapeDtypeStruct((B, S, MLA_WIDTH), F32),
        jax.ShapeDtypeStruct((B, S, MEM_WIDTH), BF16),
    ]
    return pl.pallas_call(
        functools.partial(_pre_a_kernel, layer=layer, bounded=bounded),
        grid=(B, S // tm),
        in_specs=in_specs,
        out_specs=out_specs,
        out_shape=out_shape,
        compiler_params=pltpu.CompilerParams(
            dimension_semantics=("arbitrary", "arbitrary"), vmem_limit_bytes=VMEM_LIMIT),
        name="pre_a",
    )(w["mem_m2"], x, *rope, *weights, kmem, vmem)


def _attn_kernel(m2_ref, qt_ref, k_ref, vt_ref, sg_ref, o_ref, acc_ref, l_ref, m_ref, *, seq, bounded):
    tq, tk = o_ref.shape[1], ATTN_TK
    acc_ref[...] = jnp.zeros_like(acc_ref)
    l_ref[...] = jnp.zeros_like(l_ref)
    if not bounded:
        m_ref[...] = jnp.full_like(m_ref, -jnp.inf)

    def body(j, carry):
        for hh in range(2):
            k = k_ref[0, hh, pl.ds(pl.multiple_of(j * tk, tk), tk), :]
            st = _dot(k, qt_ref[0, hh])
            vt = vt_ref[0, j, hh * V_HEAD:(hh + 1) * V_HEAD, :]
            if bounded:
                pt = jnp.exp2(st - m2_ref[0])
                l_ref[hh] += jnp.sum(pt.reshape(tk // 8, 8, tq), axis=0)
                acc_ref[hh] += _dot(vt, pt.astype(BF16))
            else:
                m_old = m_ref[hh]
                m_new = jnp.maximum(m_old, jnp.max(st, axis=0, keepdims=True))
                alpha = jnp.exp2(m_old - m_new)
                pt = jnp.exp2(st - m_new)
                l_ref[hh] = alpha * l_ref[hh] + jnp.sum(pt.reshape(tk // 8, 8, tq), axis=0)
                acc_ref[hh] = alpha * acc_ref[hh] + _dot(vt, pt.astype(BF16))
                m_ref[hh] = m_new
        return carry

    lax.fori_loop(0, seq // tk, body, 0, unroll=2)
    ot = [acc_ref[hh] / jnp.sum(l_ref[hh], axis=0, keepdims=True) for hh in range(2)]
    o = jnp.concatenate(ot, axis=0).T
    o_ref[0] = (o * sg_ref[0]).astype(BF16)


def _attention(qt, k, vt, sg, m2, bounded):
    B, _, S, _ = k.shape
    tq = ATTN_TQ if bounded else SAFE_TQ
    return pl.pallas_call(
        functools.partial(_attn_kernel, seq=S, bounded=bounded),
        grid=(B, MLA_HEADS // 2, S // tq),
        in_specs=[
            pl.BlockSpec(memory_space=pltpu.SMEM),
            pl.BlockSpec((1, 2, LANE, tq), lambda b, hp, i: (b, hp, 0, i)),
            pl.BlockSpec((1, 2, S, LANE), lambda b, hp, i: (b, hp, 0, 0)),
            pl.BlockSpec((1, S // ATTN_TK, LANE, ATTN_TK), lambda b, hp, i: (b, 0, hp, 0)),
            pl.BlockSpec((1, tq, LANE), lambda b, hp, i: (b, i, hp)),
        ],
        out_specs=pl.BlockSpec((1, tq, LANE), lambda b, hp, i: (b, i, hp)),
        out_shape=jax.ShapeDtypeStruct((B, S, MLA_WIDTH), BF16),
        scratch_shapes=[pltpu.VMEM((2, V_HEAD, tq), F32), pltpu.VMEM((2, 8, tq), F32),
                        pltpu.VMEM((2, 1, tq), F32)],
        compiler_params=pltpu.CompilerParams(
            dimension_semantics=("arbitrary", "arbitrary", "arbitrary"), vmem_limit_bytes=VMEM_LIMIT),
        name="attn",
    )(m2, qt, k, vt, sg)


def _out_proj(x, o_mix, o_mem, wout_ref):
    return x + _dot(o_mix, wout_ref[0, :MLA_WIDTH, :]) + _dot(o_mem, wout_ref[0, MLA_WIDTH:, :])


def _spatial_mix(vn, ws_ref, bias):
    rows = vn.shape[0]
    n_blk = SG_WIDTH // LANE
    masked = []
    for g in range(SG_GROUPS):
        lo, hi = g * SG_GDIM, (g + 1) * SG_GDIM
        b0, b1 = lo // LANE, (hi - 1) // LANE
        blk = vn[:, b0 * LANE:(b1 + 1) * LANE]
        col = lax.broadcasted_iota(jnp.int32, blk.shape, 1) + b0 * LANE
        masked.append((b0, b1, jnp.where(col >= lo, jnp.where(col < hi, blk, 0.0), 0.0).astype(BF16)))
    chunks = []
    for c in range(rows // CHUNK):
        parts = [[] for _ in range(n_blk)]
        for g, (b0, b1, mv) in enumerate(masked):
            res = _dot(ws_ref[0, g], mv[c * CHUNK:(c + 1) * CHUNK])
            for b in range(b0, b1 + 1):
                parts[b].append(res[:, (b - b0) * LANE:(b - b0 + 1) * LANE])
        chunks.append(jnp.concatenate([functools.reduce(lambda a, b: a + b, p) for p in parts], axis=1) + bias)
    return jnp.concatenate(chunks, axis=0)


def _post_b_kernel(mm2_ref, x_ref, og_ref, omem_ref, wout_a_ref, ln_ref, win_ref, vg_ref, ws_ref, bias_ref,
                   mqg_ref, kmem_ref, vmem_ref, wout_ref, y_ref, *, layer, bounded):
    y_ref[0] = _out_proj(x_ref[0], og_ref[0], omem_ref[0], wout_a_ref)
    x = y_ref[0]
    h = (x * _rsqrt_mean(x, D_MODEL) * ln_ref[0]).astype(BF16)
    u = _gelu_tanh(_dot(h, win_ref[0, :, :SG_WIDTH]))
    v = _gelu_tanh(_dot(h, win_ref[0, :, SG_WIDTH:2 * SG_WIDTH]))
    vn = v * _rsqrt_mean(v, SG_WIDTH) * vg_ref[0]
    sg = _silu(_dot(h, win_ref[0, :, 2 * SG_WIDTH + MEM_WIDTH:]))
    o_mix = (u * _spatial_mix(vn, ws_ref, bias_ref[0]) * sg[:, :SG_WIDTH]).astype(BF16)
    qm = _dot(h, win_ref[0, :, 2 * SG_WIDTH:2 * SG_WIDTH + MEM_WIDTH])
    o_mem = _memory_attention(qm, mqg_ref[0], kmem_ref, vmem_ref, _mem_shift(mm2_ref, layer, bounded))
    o_mem = (o_mem * sg[:, SG_WIDTH:]).astype(BF16)
    y_ref[0] = _out_proj(x, o_mix, o_mem, wout_ref)


def _post_b(x, o_gated, o_mem, w, j, layer, kmem, vmem, bounded):
    B, S, _ = x.shape
    tm = ROW_TILE
    weights = [w["wout_a"], w["ln_g"], w["b_win"], w["b_vg"], w["b_ws"], w["b_bias"], w["mq_g"]]
    wlayer = [j, layer, j, j, j, j, layer]
    in_specs = [
        pl.BlockSpec(memory_space=pltpu.SMEM),
        pl.BlockSpec((1, tm, D_MODEL), lambda b, i: (b, i, 0)),
        pl.BlockSpec((1, tm, MLA_WIDTH), lambda b, i: (b, i, 0)),
        pl.BlockSpec((1, tm, MEM_WIDTH), lambda b, i: (b, i, 0)),
    ]
    in_specs += [_const_spec(a.shape, l) for a, l in zip(weights, wlayer)]
    in_specs += _mem_specs(layer) + [_const_spec(w["wout_b"].shape, j)]
    return pl.pallas_call(
        functools.partial(_post_b_kernel, layer=layer, bounded=bounded),
        grid=(B, S // tm),
        in_specs=in_specs,
        out_specs=pl.BlockSpec((1, tm, D_MODEL), lambda b, i: (b, i, 0)),
        out_shape=jax.ShapeDtypeStruct((B, S, D_MODEL), F32),
        compiler_params=pltpu.CompilerParams(
            dimension_semantics=("arbitrary", "arbitrary"), vmem_limit_bytes=VMEM_LIMIT),
        name="post_b",
    )(w["mem_m2"], x, o_gated, o_mem, *weights, kmem, vmem, w["wout_b"])


def _head_lanes(nope, rope, axis=-1):
    ref = nope if nope is not None else rope
    axis = axis % ref.ndim

    def z(n):
        return jnp.zeros(ref.shape[:axis] + (n,) + ref.shape[axis + 1:], ref.dtype)

    def cut(a, lo, hi):
        return lax.slice_in_dim(a, lo, hi, axis=axis)

    n0, n1 = (cut(nope, 0, 48), cut(nope, 48, QK_NOPE)) if nope is not None else (z(48), z(QK_NOPE - 48))
    r0, r1 = ((cut(rope, 0, HALF_ROPE), cut(rope, HALF_ROPE, QK_ROPE)) if rope is not None
              else (z(HALF_ROPE), z(HALF_ROPE)))
    return jnp.concatenate([r0, n0, r1, n1, z(LANE - QK_HEAD)], axis=axis)


def _prep_weights(ln_g, w_in_a, a_qlat_g, a_kvlat_g, a_w_uq, a_w_ukv, a_q_norm_g, a_k_norm_g, w_in_b,
                  b_v_norm_g, b_w_s, b_bias, mem_norm_g, w_mem_kv, mem_q_norm_g, mem_k_norm_g, w_out):
    w = {}
    na = w_in_a.shape[0]
    w["ln_g"] = ln_g[:, None, :]
    w["mem_g"] = mem_norm_g[:, None, :]

    o2, o3, o4 = Q_LORA + KV_LORA, Q_LORA + KV_LORA + QK_ROPE, Q_LORA + KV_LORA + QK_ROPE + MEM_WIDTH
    w["a_wlat"] = jnp.concatenate([w_in_a[:, :, :o2], _head_lanes(None, w_in_a[:, :, o2:o3])], axis=-1).astype(BF16)
    w["a_wqm"] = w_in_a[:, :, o3:o4].astype(BF16)
    w["a_wgate"] = w_in_a[:, :, o4:].astype(BF16)
    w["a_qlat_g"] = a_qlat_g[:, None, :]
    w["a_kvlat_g"] = a_kvlat_g[:, None, :]
    uq = a_w_uq.reshape(na, Q_LORA, MLA_HEADS, QK_HEAD)
    wuq = _head_lanes(uq[..., :QK_NOPE], uq[..., QK_NOPE:]).reshape(na, Q_LORA, MLA_HEADS * LANE)
    w["a_wuqt"] = jnp.swapaxes(wuq, 1, 2).astype(BF16)
    ukv =a_w_ukv.reshape(na, KV_LORA, MLA_HEADS, QK_NOPE + V_HEAD)
    w["a_wk"] = _head_lanes(ukv[..., :QK_NOPE], None).reshape(na, KV_LORA, MLA_HEADS * LANE).astype(BF16)
    w["a_wvt"] = jnp.swapaxes(ukv[..., QK_NOPE:].reshape(na, KV_LORA, MLA_WIDTH), 1, 2).astype(BF16)

    def norm_gain_rows(g):
        g_c = _head_lanes(g[:, :QK_NOPE], g[:, QK_NOPE:])
        return jnp.stack([g_c, jnp.roll(g_c, HALF_LANE, axis=-1)], axis=1)

    w["a_qg"] = jnp.broadcast_to(norm_gain_rows(a_q_norm_g)[..., None], (na, 2, LANE, LANE))
    w["a_kg"] = norm_gain_rows(a_k_norm_g)
    g_max = jnp.max(jnp.abs(a_q_norm_g), axis=1) * jnp.max(jnp.abs(a_k_norm_g), axis=1)
    w["a_m2"] = (QK_HEAD ** 0.5 * LOG2E * g_max)[:, None]

    w["b_win"] = w_in_b.astype(BF16)
    w["b_vg"] = b_v_norm_g[:, None, :]
    w["b_ws"] = b_w_s.astype(BF16)
    w["b_bias"] = jnp.repeat(jnp.swapaxes(b_bias, 1, 2), SG_GDIM, axis=2)

    w["mem_wk"] = w_mem_kv[:, :, :MEM_WIDTH].astype(BF16)
    w["mem_wv"] = w_mem_kv[:, :, MEM_WIDTH:].astype(BF16)
    w["mem_m2"] = (MEM_HDIM ** 0.5 * LOG2E * jnp.max(jnp.abs(mem_q_norm_g), axis=1)
                   * jnp.max(jnp.abs(mem_k_norm_g), axis=1))
    w["mq_g"] = jnp.tile(mem_q_norm_g, (1, MEM_HEADS))[:, None, :]
    w["mk_g"] = jnp.tile(mem_k_norm_g, (1, MEM_HEADS))[:, None, :]
    w["wout_a"] = w_out[0::2].astype(BF16)
    w["wout_b"] = w_out[1::2].astype(BF16)
    return w


def _rope_tables(seq):
    inv = 1.0 / (ROPE_BASE ** (jnp.arange(0, QK_ROPE, 2, dtype=F32) / QK_ROPE))
    ang = inv[:, None] * jnp.arange(seq, dtype=F32)[None, :]
    cos, sin = jnp.cos(ang), jnp.sin(ang)
    ctt = _head_lanes(jnp.ones((QK_NOPE, seq), F32), jnp.concatenate([cos, cos], axis=0), axis=0)
    stt = _head_lanes(None, jnp.concatenate([-sin, sin], axis=0), axis=0)
    return ctt.T, stt.T, ctt, stt


def _trunk(x, mem, rope, w, bounded):
    kmem, vmem = _mem_kv(mem, w["mem_g"], w["mem_wk"], w["mem_wv"], w["mk_g"])
    for j in range(DEPTH // 2):
        qt, k, vt, sg, o_mem = _pre_a(x, rope, w, j, 2 * j, kmem, vmem, bounded)
        o_gated = _attention(qt, k, vt, sg, w["a_m2"][j], bounded)
        x = _post_b(x, o_gated, o_mem, w, j, 2 * j + 1, kmem, vmem, bounded)
    return x


def kernel(x_prompt, x_sample, mem_prompt, mem_sample, ln_g, w_in_a, a_qlat_g, a_kvlat_g, a_w_uq, a_w_ukv,
           a_q_norm_g, a_k_norm_g, w_in_b, b_v_norm_g, b_w_s, b_bias, mem_norm_g, w_mem_kv, mem_q_norm_g,
           mem_k_norm_g, w_out):
    w = _prep_weights(ln_g, w_in_a, a_qlat_g, a_kvlat_g, a_w_uq, a_w_ukv, a_q_norm_g, a_k_norm_g, w_in_b,
                      b_v_norm_g, b_w_s, b_bias, mem_norm_g, w_mem_kv, mem_q_norm_g, mem_k_norm_g, w_out)

    rope = _rope_tables(max(x_prompt.shape[1], x_sample.shape[1]))

    def run(bounded):
        return lambda ops: (_trunk(ops[0], ops[1], rope, w, bounded), _trunk(ops[2], ops[3], rope, w, bounded))

    shifts_ok = jnp.maximum(jnp.max(w["a_m2"]), jnp.max(w["mem_m2"])) <= MAX_FIXED_SHIFT
    return lax.cond(shifts_ok, run(True), run(False), (x_prompt, mem_prompt, x_sample, mem_sample))
```

```python
import functools
import math

import jax
import jax.numpy as jnp
from jax import lax
from jax.experimental import pallas as pl
from jax.experimental.pallas import tpu as pltpu

D_MODEL = 1024
DEPTH = 4
EPS = 1e-6
N_MEM = 256
MEM_HEADS = 4
MEM_HDIM = 64
MEM_WIDTH = MEM_HEADS * MEM_HDIM
MLA_HEADS = 12
QK_NOPE = 64
QK_ROPE = 32
QK_HEAD = QK_NOPE + QK_ROPE
V_HEAD = 64
Q_LORA = 384
KV_LORA = 256
MLA_WIDTH = MLA_HEADS * V_HEAD
ROPE_BASE = 10000.0
CHUNK = 128
SG_GROUPS = 8
SG_WIDTH = 768
SG_GDIM = SG_WIDTH // SG_GROUPS
BRANCH = MLA_WIDTH + MEM_WIDTH

LANE = 128
HALF_LANE = LANE // 2
HALF_ROPE = QK_ROPE // 2

ROW_TILE = 1024
ATTN_TQ = 2048
ATTN_TK = 2048
SAFE_TQ = 512
VMEM_LIMIT = 56 * 1024 * 1024
LOG2E = 1.4426950408889634
MAX_FIXED_SHIFT = 40.0

BF16 = jnp.bfloat16
F32 = jnp.float32


def _rsqrt_mean(x, n):
    return lax.rsqrt(jnp.sum(x * x, axis=-1, keepdims=True) * (1.0 / n) + EPS)


def _silu(g):
    return g / (1.0 + jnp.exp(-g))


def _gelu_tanh(x):
    c = math.sqrt(2.0 / math.pi)
    return 0.5 * x * (1.0 + jnp.tanh(c * (x + 0.044715 * (x * x * x))))


def _dot(a, b):
    return jnp.dot(a, b, preferred_element_type=F32)


def _dot_nt(a, b):
    return lax.dot_general(a, b, (((1,), (1,)), ((), ())), preferred_element_type=F32)


def _low_half(shape):
    return lax.broadcasted_iota(jnp.int32, shape, 1) < HALF_LANE


def _norm_heads64(x, g):
    out = []
    for a in range(MEM_WIDTH // LANE):
        blk = x[:, a * LANE:(a + 1) * LANE]
        lo = _low_half(blk.shape)
        sq = blk * blk
        s_lo = jnp.sum(jnp.where(lo, sq, 0.0), axis=-1, keepdims=True)
        s_hi = jnp.sum(jnp.where(lo, 0.0, sq), axis=-1, keepdims=True)
        r = jnp.where(lo, lax.rsqrt(s_lo * (1.0 / MEM_HDIM) + EPS), lax.rsqrt(s_hi * (1.0 / MEM_HDIM) + EPS))
        out.append(blk * r * g[:, a * LANE:(a + 1) * LANE])
    return out


def _mem_kv_kernel(mem_ref, g_ref, wk_ref, wv_ref, kg_ref, k_out, v_out):
    m = mem_ref[0]
    m_hat = m * _rsqrt_mean(m, D_MODEL)
    lane = lax.broadcasted_iota(jnp.int32, (N_MEM, MEM_WIDTH), 1)
    for layer in range(DEPTH):
        mn = (m_hat * g_ref[layer]).astype(BF16)
        kn = _norm_heads64(_dot(mn, wk_ref[layer]), kg_ref[layer])
        k_out[layer, 0] = jnp.concatenate(kn, axis=1).astype(BF16)
        vv = _dot(mn, wv_ref[layer])
        for h in range(MEM_HEADS):
            vh = jnp.where(lane >= h * MEM_HDIM, jnp.where(lane < (h + 1) * MEM_HDIM, vv, 0.0), 0.0)
            v_out[layer, 0, h] = vh.astype(BF16)


def _mem_kv(mem, mem_g, wk, wv, kg):
    B = mem.shape[0]
    whole = lambda a: pl.BlockSpec(a.shape, lambda b: (0,) * a.ndim)
    return pl.pallas_call(
        _mem_kv_kernel,
        grid=(B,),
        in_specs=[pl.BlockSpec((1, N_MEM, D_MODEL), lambda b: (b, 0, 0)),
                  whole(mem_g), whole(wk), whole(wv), whole(kg)],
        out_specs=[
            pl.BlockSpec((DEPTH, 1, N_MEM, MEM_WIDTH), lambda b: (0, b, 0, 0)),
            pl.BlockSpec((DEPTH, 1, MEM_HEADS, N_MEM, MEM_WIDTH), lambda b: (0, b, 0, 0, 0)),
        ],
        out_shape=[
            jax.ShapeDtypeStruct((DEPTH, B, N_MEM, MEM_WIDTH), BF16),
            jax.ShapeDtypeStruct((DEPTH, B, MEM_HEADS, N_MEM, MEM_WIDTH), BF16),
        ],
        compiler_params=pltpu.CompilerParams(
            dimension_semantics=("arbitrary",), vmem_limit_bytes=VMEM_LIMIT),
        name="mem_kv",
    )(mem, mem_g, wk, wv, kg)


def _mem_specs(layer):
    return [
        pl.BlockSpec((1, 1, N_MEM, MEM_WIDTH), lambda b, i: (layer, b, 0, 0)),
        pl.BlockSpec((1, 1, MEM_HEADS, N_MEM, MEM_WIDTH), lambda b, i: (layer, b, 0, 0, 0)),
    ]


def _memory_attention(qm, qg, kmem_ref, vmem_ref, shift):
    qn = _norm_heads64(qm, qg)
    o = None
    inv_l = []
    for h in range(MEM_HEADS):
        a, upper = divmod(h, 2)
        lo = _low_half(qn[a].shape)
        qh = jnp.where(lo, 0.0, qn[a]) if upper else jnp.where(lo, qn[a], 0.0)
        qh = (qh * (MEM_HDIM ** -0.5 * LOG2E)).astype(BF16)
        s = _dot_nt(qh, kmem_ref[0, 0, :, a * LANE:(a + 1) * LANE])
        if shift is None:
            e = jnp.exp2(s - jnp.max(s, axis=-1, keepdims=True))
            p = (e / jnp.sum(e, axis=-1, keepdims=True)).astype(BF16)
        else:
            e = jnp.exp2(s - shift)
            inv_l.append(1.0 / jnp.sum(e[:, :LANE] + e[:, LANE:], axis=-1, keepdims=True))
            p = e.astype(BF16)
        oh = _dot(p, vmem_ref[0, 0, h])
        o = oh if o is None else o + oh
    if shift is not None:
        lane = lax.broadcasted_iota(jnp.int32, o.shape, 1)
        scale = inv_l[MEM_HEADS - 1]
        for h in range(MEM_HEADS - 2, -1, -1):
            scale = jnp.where(lane < (h + 1) * MEM_HDIM, inv_l[h], scale)
        o = o * scale
    return o


def _mem_shift(mm2_ref, layer, bounded):
    return mm2_ref[layer] if bounded else None


def _pre_a_kernel(mm2_ref, x_ref, ct_ref, st_ref, ctt_ref, stt_ref, ln_ref, wlat_ref, wqm_ref, wgate_ref,
                  qlg_ref, kvlg_ref, wuqt_ref, wk_ref, wvt_ref, qg_ref, kg_ref, mqg_ref, kmem_ref, vmem_ref,
                  qt_out, k_out, vt_out, sg_out, om_out, *, layer, bounded):
    x = x_ref[0]
    h = (x * _rsqrt_mean(x, D_MODEL) * ln_ref[0]).astype(BF16)

    lat = _dot(h, wlat_ref[0])
    q_lat = lat[:, :Q_LORA]
    kv_lat = lat[:, Q_LORA:Q_LORA + KV_LORA]
    kpe = lat[:, Q_LORA + KV_LORA:]

    ct = ct_ref[...]
    st = st_ref[...]

    rows = x.shape[0]
    qn = (q_lat * _rsqrt_mean(q_lat, Q_LORA) * qlg_ref[0]).astype(BF16)
    qt_raw = _dot_nt(wuqt_ref[0], qn)
    q_c = ctt_ref[...] * jnp.tile(qg_ref[0, 0], (1, rows // LANE))
    q_s = stt_ref[...] * jnp.tile(qg_ref[0, 1], (1, rows // LANE))
    scale = QK_HEAD ** -0.5 * LOG2E
    for hd in range(MLA_HEADS):
        qh = qt_raw[hd * LANE:(hd + 1) * LANE, :]
        r = lax.rsqrt(jnp.sum(qh * qh, axis=0, keepdims=True) * (1.0 / QK_HEAD) + EPS)
        partner = jnp.concatenate([qh[HALF_LANE:], qh[:HALF_LANE]], axis=0)
        qt_out[0, hd] = ((qh * q_c + partner * q_s) * (r * scale)).astype(BF16)

    kvn = (kv_lat * _rsqrt_mean(kv_lat, KV_LORA) * kvlg_ref[0]).astype(BF16)
    k_nope = _dot(kvn, wk_ref[0])
    vt_out[0, 0] = _dot_nt(wvt_ref[0], kvn).astype(BF16)
    k_cg = kg_ref[0, 0:1]
    kpe_rot = kpe * (ct * k_cg) + pltpu.roll(kpe, HALF_LANE, 1) * (st * kg_ref[0, 1:2])
    pe_sq = jnp.sum(kpe * kpe, axis=-1, keepdims=True)
    for hd in range(MLA_HEADS):
        kh = k_nope[:, hd * LANE:(hd + 1) * LANE]
        r = lax.rsqrt((jnp.sum(kh * kh, axis=-1, keepdims=True) + pe_sq) * (1.0 / QK_HEAD) + EPS)
        k_out[0, hd] = ((kh * k_cg + kpe_rot) * r).astype(BF16)

    sg = _silu(_dot(h, wgate_ref[0]))
    sg_out[0] = sg[:, :MLA_WIDTH]
    o_mem = _memory_attention(_dot(h, wqm_ref[0]), mqg_ref[0], kmem_ref, vmem_ref,
                              _mem_shift(mm2_ref, layer, bounded))
    om_out[0] = (o_mem * sg[:, MLA_WIDTH:]).astype(BF16)


def _const_spec(shape, layer):
    nd = len(shape)
    return pl.BlockSpec((1,) + tuple(shape[1:]), lambda b, i: (layer,) + (0,) * (nd - 1))


def _pre_a(x, rope, w, j, layer, kmem, vmem, bounded):
    B, S, _ = x.shape
    tm = ROW_TILE
    kv_per_chunk = ATTN_TK // tm
    weights = [w["ln_g"], w["a_wlat"], w["a_wqm"], w["a_wgate"], w["a_qlat_g"], w["a_kvlat_g"],
               w["a_wuqt"], w["a_wk"], w["a_wvt"], w["a_qg"], w["a_kg"], w["mq_g"]]
    wlayer = [layer, j, j, j, j, j, j, j, j, j, j, layer]
    in_specs = [
        pl.BlockSpec(memory_space=pltpu.SMEM),
        pl.BlockSpec((1, tm, D_MODEL), lambda b, i: (b, i, 0)),
        pl.BlockSpec((tm, LANE), lambda b, i: (i, 0)),
        pl.BlockSpec((tm, LANE), lambda b, i: (i, 0)),
        pl.BlockSpec((LANE, tm), lambda b, i: (0, i)),
        pl.BlockSpec((LANE, tm), lambda b, i: (0, i)),
    ] + [_const_spec(a.shape, l) for a, l in zip(weights, wlayer)] + _mem_specs(layer)
    out_specs = [
        pl.BlockSpec((1, MLA_HEADS, LANE, tm), lambda b, i: (b, 0, 0, i)),
        pl.BlockSpec((1, MLA_HEADS, tm, LANE), lambda b, i: (b, 0, i, 0)),
        pl.BlockSpec((1, 1, MLA_WIDTH, tm), lambda b, i: (b, i // kv_per_chunk, 0, i % kv_per_chunk)),
        pl.BlockSpec((1, tm, MLA_WIDTH), lambda b, i: (b, i, 0)),
        pl.BlockSpec((1, tm, MEM_WIDTH), lambda b, i: (b, i, 0)),
    ]
    out_shape = [
        jax.ShapeDtypeStruct((B, MLA_HEADS, LANE, S), BF16),
        jax.ShapeDtypeStruct((B, MLA_HEADS, S, LANE), BF16),
        jax.ShapeDtypeStruct((B, S // ATTN_TK, MLA_WIDTH, ATTN_TK), BF16),
        jax.ShapeDtypeStruct((B, S, MLA_WIDTH), F32),
        jax.ShapeDtypeStruct((B, S, MEM_WIDTH), BF16),
    ]
    return pl.pallas_call(
        functools.partial(_pre_a_kernel, layer=layer, bounded=bounded),
        grid=(B, S // tm),
        in_specs=in_specs,
        out_specs=out_specs,
        out_shape=out_shape,
        compiler_params=pltpu.CompilerParams(
            dimension_semantics=("arbitrary", "arbitrary"), vmem_limit_bytes=VMEM_LIMIT),
        name="pre_a",
    )(w["mem_m2"], x, *rope, *weights, kmem, vmem)


def _attn_kernel(m2_ref, qt_ref, k_ref, vt_ref, sg_ref, o_ref, acc_ref, l_ref, m_ref, *, seq, bounded):
    tq, tk = o_ref.shape[1], ATTN_TK
    acc_ref[...] = jnp.zeros_like(acc_ref)
    l_ref[...] = jnp.zeros_like(l_ref)
    if not bounded:
        m_ref[...] = jnp.full_like(m_ref, -jnp.inf)

    def body(j, carry):
        for hh in range(2):
            k = k_ref[0, hh, pl.ds(pl.multiple_of(j * tk, tk), tk), :]
            st = _dot(k, qt_ref[0, hh])
            vt = vt_ref[0, j, hh * V_HEAD:(hh + 1) * V_HEAD, :]
            if bounded:
                pt = jnp.exp2(st - m2_ref[0])
                l_ref[hh] += jnp.sum(pt.reshape(tk // 8, 8, tq), axis=0)
                acc_ref[hh] += _dot(vt, pt.astype(BF16))
            else:
                m_old = m_ref[hh]
                m_new = jnp.maximum(m_old, jnp.max(st, axis=0, keepdims=True))
                alpha = jnp.exp2(m_old - m_new)
                pt = jnp.exp2(st - m_new)
                l_ref[hh] = alpha * l_ref[hh] + jnp.sum(pt.reshape(tk // 8, 8, tq), axis=0)
                acc_ref[hh] = alpha * acc_ref[hh] + _dot(vt, pt.astype(BF16))
                m_ref[hh] = m_new
        return carry

    lax.fori_loop(0, seq // tk, body, 0, unroll=2)
    ot = [acc_ref[hh] / jnp.sum(l_ref[hh], axis=0, keepdims=True) for hh in range(2)]
    o = jnp.concatenate(ot, axis=0).T
    o_ref[0] = (o * sg_ref[0]).astype(BF16)


def _attention(qt, k, vt, sg, m2, bounded):
    B, _, S, _ = k.shape
    tq = ATTN_TQ if bounded else SAFE_TQ
    return pl.pallas_call(
        functools.partial(_attn_kernel, seq=S, bounded=bounded),
        grid=(B, MLA_HEADS // 2, S // tq),
        in_specs=[
            pl.BlockSpec(memory_space=pltpu.SMEM),
            pl.BlockSpec((1, 2, LANE, tq), lambda b, hp, i: (b, hp, 0, i)),
            pl.BlockSpec((1, 2, S, LANE), lambda b, hp, i: (b, hp, 0, 0)),
            pl.BlockSpec((1, S // ATTN_TK, LANE, ATTN_TK), lambda b, hp, i: (b, 0, hp, 0)),
            pl.BlockSpec((1, tq, LANE), lambda b, hp, i: (b, i, hp)),
        ],
        out_specs=pl.BlockSpec((1, tq, LANE), lambda b, hp, i: (b, i, hp)),
        out_shape=jax.ShapeDtypeStruct((B, S, MLA_WIDTH), BF16),
        scratch_shapes=[pltpu.VMEM((2, V_HEAD, tq), F32), pltpu.VMEM((2, 8, tq), F32),
                        pltpu.VMEM((2, 1, tq), F32)],
        compiler_params=pltpu.CompilerParams(
            dimension_semantics=("arbitrary", "arbitrary", "arbitrary"), vmem_limit_bytes=VMEM_LIMIT),
        name="attn",
    )(m2, qt, k, vt, sg)


def _out_proj(x, o_mix, o_mem, wout_ref):
    return x + _dot(o_mix, wout_ref[0, :MLA_WIDTH, :]) + _dot(o_mem, wout_ref[0, MLA_WIDTH:, :])


def _spatial_mix(vn, ws_ref, bias):
    rows = vn.shape[0]
    n_blk = SG_WIDTH // LANE
    masked = []
    for g in range(SG_GROUPS):
        lo, hi = g * SG_GDIM, (g + 1) * SG_GDIM
        b0, b1 = lo // LANE, (hi - 1) // LANE
        blk = vn[:, b0 * LANE:(b1 + 1) * LANE]
        col = lax.broadcasted_iota(jnp.int32, blk.shape, 1) + b0 * LANE
        masked.append((b0, b1, jnp.where(col >= lo, jnp.where(col < hi, blk, 0.0), 0.0).astype(BF16)))
    chunks = []
    for c in range(rows // CHUNK):
        parts = [[] for _ in range(n_blk)]
        for g, (b0, b1, mv) in enumerate(masked):
            res = _dot(ws_ref[0, g], mv[c * CHUNK:(c + 1) * CHUNK])
            for b in range(b0, b1 + 1):
                parts[b].append(res[:, (b - b0) * LANE:(b - b0 + 1) * LANE])
        chunks.append(jnp.concatenate([functools.reduce(lambda a, b: a + b, p) for p in parts], axis=1) + bias)
    return jnp.concatenate(chunks, axis=0)


def _post_b_kernel(mm2_ref, x_ref, og_ref, omem_ref, wout_a_ref, ln_ref, win_ref, vg_ref, ws_ref, bias_ref,
                   mqg_ref, kmem_ref, vmem_ref, wout_ref, y_ref, *, layer, bounded):
    y_ref[0] = _out_proj(x_ref[0], og_ref[0], omem_ref[0], wout_a_ref)
    x = y_ref[0]
    h = (x * _rsqrt_mean(x, D_MODEL) * ln_ref[0]).astype(BF16)
    u = _gelu_tanh(_dot(h, win_ref[0, :, :SG_WIDTH]))
    v = _gelu_tanh(_dot(h, win_ref[0, :, SG_WIDTH:2 * SG_WIDTH]))
    vn = v * _rsqrt_mean(v, SG_WIDTH) * vg_ref[0]
    sg = _silu(_dot(h, win_ref[0, :, 2 * SG_WIDTH + MEM_WIDTH:]))
    o_mix = (u * _spatial_mix(vn, ws_ref, bias_ref[0]) * sg[:, :SG_WIDTH]).astype(BF16)
    qm = _dot(h, win_ref[0, :, 2 * SG_WIDTH:2 * SG_WIDTH + MEM_WIDTH])
    o_mem = _memory_attention(qm, mqg_ref[0], kmem_ref, vmem_ref, _mem_shift(mm2_ref, layer, bounded))
    o_mem = (o_mem * sg[:, SG_WIDTH:]).astype(BF16)
    y_ref[0] = _out_proj(x, o_mix, o_mem, wout_ref)


def _post_b(x, o_gated, o_mem, w, j, layer, kmem, vmem, bounded):
    B, S, _ = x.shape
    tm = ROW_TILE
    weights = [w["wout_a"], w["ln_g"], w["b_win"], w["b_vg"], w["b_ws"], w["b_bias"], w["mq_g"]]
    wlayer = [j, layer, j, j, j, j, layer]
    in_specs = [
        pl.BlockSpec(memory_space=pltpu.SMEM),
        pl.BlockSpec((1, tm, D_MODEL), lambda b, i: (b, i, 0)),
        pl.BlockSpec((1, tm, MLA_WIDTH), lambda b, i: (b, i, 0)),
        pl.BlockSpec((1, tm, MEM_WIDTH), lambda b, i: (b, i, 0)),
    ]
    in_specs += [_const_spec(a.shape, l) for a, l in zip(weights, wlayer)]
    in_specs += _mem_specs(layer) + [_const_spec(w["wout_b"].shape, j)]
    return pl.pallas_call(
        functools.partial(_post_b_kernel, layer=layer, bounded=bounded),
        grid=(B, S // tm),
        in_specs=in_specs,
        out_specs=pl.BlockSpec((1, tm, D_MODEL), lambda b, i: (b, i, 0)),
        out_shape=jax.ShapeDtypeStruct((B, S, D_MODEL), F32),
        compiler_params=pltpu.CompilerParams(
            dimension_semantics=("arbitrary", "arbitrary"), vmem_limit_bytes=VMEM_LIMIT),
        name="post_b",
    )(w["mem_m2"], x, o_gated, o_mem, *weights, kmem, vmem, w["wout_b"])


def _head_lanes(nope, rope, axis=-1):
    ref = nope if nope is not None else rope
    axis = axis % ref.ndim

    def z(n):
        return jnp.zeros(ref.shape[:axis] + (n,) + ref.shape[axis + 1:], ref.dtype)

    def cut(a, lo, hi):
        return lax.slice_in_dim(a, lo, hi, axis=axis)

    n0, n1 = (cut(nope, 0, 48), cut(nope, 48, QK_NOPE)) if nope is not None else (z(48), z(QK_NOPE - 48))
    r0, r1 = ((cut(rope, 0, HALF_ROPE), cut(rope, HALF_ROPE, QK_ROPE)) if rope is not None
              else (z(HALF_ROPE), z(HALF_ROPE)))
    return jnp.concatenate([r0, n0, r1, n1, z(LANE - QK_HEAD)], axis=axis)


def _prep_weights(ln_g, w_in_a, a_qlat_g, a_kvlat_g, a_w_uq, a_w_ukv, a_q_norm_g, a_k_norm_g, w_in_b,
                  b_v_norm_g, b_w_s, b_bias, mem_norm_g, w_mem_kv, mem_q_norm_g, mem_k_norm_g, w_out):
    w = {}
    na = w_in_a.shape[0]
    w["ln_g"] = ln_g[:, None, :]
    w["mem_g"] = mem_norm_g[:, None, :]

    o2, o3, o4 = Q_LORA + KV_LORA, Q_LORA + KV_LORA + QK_ROPE, Q_LORA + KV_LORA + QK_ROPE + MEM_WIDTH
    w["a_wlat"] = jnp.concatenate([w_in_a[:, :, :o2], _head_lanes(None, w_in_a[:, :, o2:o3])], axis=-1).astype(BF16)
    w["a_wqm"] = w_in_a[:, :, o3:o4].astype(BF16)
    w["a_wgate"] = w_in_a[:, :, o4:].astype(BF16)
    w["a_qlat_g"] = a_qlat_g[:, None, :]
    w["a_kvlat_g"] = a_kvlat_g[:, None, :]
    uq = a_w_uq.reshape(na, Q_LORA, MLA_HEADS, QK_HEAD)
    wuq = _head_lanes(uq[..., :QK_NOPE], uq[..., QK_NOPE:]).reshape(na, Q_LORA, MLA_HEADS * LANE)
    w["a_wuqt"] = jnp.swapaxes(wuq, 1, 2).astype(BF16)
    ukv =a_w_ukv.reshape(na, KV_LORA, MLA_HEADS, QK_NOPE + V_HEAD)
    w["a_wk"] = _head_lanes(ukv[..., :QK_NOPE], None).reshape(na, KV_LORA, MLA_HEADS * LANE).astype(BF16)
    w["a_wvt"] = jnp.swapaxes(ukv[..., QK_NOPE:].reshape(na, KV_LORA, MLA_WIDTH), 1, 2).astype(BF16)

    def norm_gain_rows(g):
        g_c = _head_lanes(g[:, :QK_NOPE], g[:, QK_NOPE:])
        return jnp.stack([g_c, jnp.roll(g_c, HALF_LANE, axis=-1)], axis=1)

    w["a_qg"] = jnp.broadcast_to(norm_gain_rows(a_q_norm_g)[..., None], (na, 2, LANE, LANE))
    w["a_kg"] = norm_gain_rows(a_k_norm_g)
    g_max = jnp.max(jnp.abs(a_q_norm_g), axis=1) * jnp.max(jnp.abs(a_k_norm_g), axis=1)
    w["a_m2"] = (QK_HEAD ** 0.5 * LOG2E * g_max)[:, None]

    w["b_win"] = w_in_b.astype(BF16)
    w["b_vg"] = b_v_norm_g[:, None, :]
    w["b_ws"] = b_w_s.astype(BF16)
    w["b_bias"] = jnp.repeat(jnp.swapaxes(b_bias, 1, 2), SG_GDIM, axis=2)

    w["mem_wk"] = w_mem_kv[:, :, :MEM_WIDTH].astype(BF16)
    w["mem_wv"] = w_mem_kv[:, :, MEM_WIDTH:].astype(BF16)
    w["mem_m2"] = (MEM_HDIM ** 0.5 * LOG2E * jnp.max(jnp.abs(mem_q_norm_g), axis=1)
                   * jnp.max(jnp.abs(mem_k_norm_g), axis=1))
    w["mq_g"] = jnp.tile(mem_q_norm_g, (1, MEM_HEADS))[:, None, :]
    w["mk_g"] = jnp.tile(mem_k_norm_g, (1, MEM_HEADS))[:, None, :]
    w["wout_a"] = w_out[0::2].astype(BF16)
    w["wout_b"] = w_out[1::2].astype(BF16)
    return w


def _rope_tables(seq):
    inv = 1.0 / (ROPE_BASE ** (jnp.arange(0, QK_ROPE, 2, dtype=F32) / QK_ROPE))
    ang = inv[:, None] * jnp.arange(seq, dtype=F32)[None, :]
    cos, sin = jnp.cos(ang), jnp.sin(ang)
    ctt = _head_lanes(jnp.ones((QK_NOPE, seq), F32), jnp.concatenate([cos, cos], axis=0), axis=0)
    stt = _head_lanes(None, jnp.concatenate([-sin, sin], axis=0), axis=0)
    return ctt.T, stt.T, ctt, stt


def _trunk(x, mem, rope, w, bounded):
    kmem, vmem = _mem_kv(mem, w["mem_g"], w["mem_wk"], w["mem_wv"], w["mk_g"])
    for j in range(DEPTH // 2):
        qt, k, vt, sg, o_mem = _pre_a(x, rope, w, j, 2 * j, kmem, vmem, bounded)
        o_gated = _attention(qt, k, vt, sg, w["a_m2"][j], bounded)
        x = _post_b(x, o_gated, o_mem, w, j, 2 * j + 1, kmem, vmem, bounded)
    return x


def kernel(x_prompt, x_sample, mem_prompt, mem_sample, ln_g, w_in_a, a_qlat_g, a_kvlat_g, a_w_uq, a_w_ukv,
           a_q_norm_g, a_k_norm_g, w_in_b, b_v_norm_g, b_w_s, b_bias, mem_norm_g, w_mem_kv, mem_q_norm_g,
           mem_k_norm_g, w_out):
    w = _prep_weights(ln_g, w_in_a, a_qlat_g, a_kvlat_g, a_w_uq, a_w_ukv, a_q_norm_g, a_k_norm_g, w_in_b,
                      b_v_norm_g, b_w_s, b_bias, mem_norm_g, w_mem_kv, mem_q_norm_g, mem_k_norm_g, w_out)

    rope = _rope_tables(max(x_prompt.shape[1], x_sample.shape[1]))

    def run(bounded):
        return lambda ops: (_trunk(ops[0], ops[1], rope, w, bounded), _trunk(ops[2], ops[3], rope, w, bounded))

    shifts_ok = jnp.maximum(jnp.max(w["a_m2"]), jnp.max(w["mem_m2"])) <= MAX_FIXED_SHIFT
    return lax.cond(shifts_ok, run(True), run(False), (x_prompt, mem_prompt, x_sample, mem_sample))
```

```python
import functools
import math

import jax
import jax.numpy as jnp
from jax import lax
from jax.experimental import pallas as pl
from jax.experimental.pallas import tpu as pltpu

D_MODEL = 1024
DEPTH = 4
EPS = 1e-6
N_MEM = 256
MEM_HEADS = 4
MEM_HDIM = 64
MEM_WIDTH = MEM_HEADS * MEM_HDIM
MLA_HEADS = 12
QK_NOPE = 64
QK_ROPE = 32
QK_HEAD = QK_NOPE + QK_ROPE
V_HEAD = 64
Q_LORA = 384
KV_LORA = 256
MLA_WIDTH = MLA_HEADS * V_HEAD
ROPE_BASE = 10000.0
CHUNK = 128
SG_GROUPS = 8
SG_WIDTH = 768
SG_GDIM = SG_WIDTH // SG_GROUPS
BRANCH = MLA_WIDTH + MEM_WIDTH

LANE = 128
HALF_LANE = LANE // 2
HALF_ROPE = QK_ROPE // 2

ROW_TILE = 1024
ATTN_TQ = 2048
ATTN_TK = 2048
SAFE_TQ = 512
VMEM_LIMIT = 56 * 1024 * 1024
LOG2E = 1.4426950408889634
MAX_FIXED_SHIFT = 40.0

BF16 = jnp.bfloat16
F32 = jnp.float32


def _rsqrt_mean(x, n):
    return lax.rsqrt(jnp.sum(x * x, axis=-1, keepdims=True) * (1.0 / n) + EPS)


def _silu(g):
    return g / (1.0 + jnp.exp2(g * -LOG2E))


def _gelu_tanh(x):
    k1 = -2.0 * LOG2E * math.sqrt(2.0 / math.pi)
    k2 = k1 * 0.044715
    return x / (1.0 + jnp.exp2(x * (k1 + k2 * (x * x))))


def _dot(a, b):
    return jnp.dot(a, b, preferred_element_type=F32)


def _dot_nt(a, b):
    return lax.dot_general(a, b, (((1,), (1,)), ((), ())), preferred_element_type=F32)


def _low_half(shape):
    return lax.broadcasted_iota(jnp.int32, shape, 1) < HALF_LANE


def _norm_heads64(x, g):
    out = []
    for a in range(MEM_WIDTH // LANE):
        blk = x[:, a * LANE:(a + 1) * LANE]
        lo = _low_half(blk.shape)
        sq = blk * blk
        s_lo = jnp.sum(jnp.where(lo, sq, 0.0), axis=-1, keepdims=True)
        s_hi = jnp.sum(jnp.where(lo, 0.0, sq), axis=-1, keepdims=True)
        r = jnp.where(lo, lax.rsqrt(s_lo * (1.0 / MEM_HDIM) + EPS), lax.rsqrt(s_hi * (1.0 / MEM_HDIM) + EPS))
        out.append(blk * r * g[:, a * LANE:(a + 1) * LANE])
    return out


def _mem_kv_kernel(mem_ref, g_ref, wk_ref, wv_ref, kg_ref, k_out, v_out):
    m = mem_ref[0]
    m_hat = m * _rsqrt_mean(m, D_MODEL)
    lane = lax.broadcasted_iota(jnp.int32, (N_MEM, MEM_WIDTH), 1)
    for layer in range(DEPTH):
        mn = (m_hat * g_ref[layer]).astype(BF16)
        kn = _norm_heads64(_dot(mn, wk_ref[layer]), kg_ref[layer])
        k_out[layer, 0] = jnp.concatenate(kn, axis=1).astype(BF16)
        vv = _dot(mn, wv_ref[layer])
        for h in range(MEM_HEADS):
            vh = jnp.where(lane >= h * MEM_HDIM, jnp.where(lane < (h + 1) * MEM_HDIM, vv, 0.0), 0.0)
            v_out[layer, 0, h] = vh.astype(BF16)


def _mem_kv(mem, mem_g, wk, wv, kg):
    B = mem.shape[0]
    whole = lambda a: pl.BlockSpec(a.shape, lambda b: (0,) * a.ndim)
    return pl.pallas_call(
        _mem_kv_kernel,
        grid=(B,),
        in_specs=[pl.BlockSpec((1, N_MEM, D_MODEL), lambda b: (b, 0, 0)),
                  whole(mem_g), whole(wk), whole(wv), whole(kg)],
        out_specs=[
            pl.BlockSpec((DEPTH, 1, N_MEM, MEM_WIDTH), lambda b: (0, b, 0, 0)),
            pl.BlockSpec((DEPTH, 1, MEM_HEADS, N_MEM, MEM_WIDTH), lambda b: (0, b, 0, 0, 0)),
        ],
        out_shape=[
            jax.ShapeDtypeStruct((DEPTH, B, N_MEM, MEM_WIDTH), BF16),
            jax.ShapeDtypeStruct((DEPTH, B, MEM_HEADS, N_MEM, MEM_WIDTH), BF16),
        ],
        compiler_params=pltpu.CompilerParams(
            dimension_semantics=("arbitrary",), vmem_limit_bytes=VMEM_LIMIT),
        name="mem_kv",
    )(mem, mem_g, wk, wv, kg)


def _mem_specs(layer):
    return [
        pl.BlockSpec((1, 1, N_MEM, MEM_WIDTH), lambda b, i: (layer, b, 0, 0)),
        pl.BlockSpec((1, 1, MEM_HEADS, N_MEM, MEM_WIDTH), lambda b, i: (layer, b, 0, 0, 0)),
    ]


def _memory_attention(qm, qg, kmem_ref, vmem_ref, shift):
    qn = _norm_heads64(qm, qg)
    o = None
    inv_l = []
    for h in range(MEM_HEADS):
        a, upper = divmod(h, 2)
        lo = _low_half(qn[a].shape)
        qh = jnp.where(lo, 0.0, qn[a]) if upper else jnp.where(lo, qn[a], 0.0)
        qh = (qh * (MEM_HDIM ** -0.5 * LOG2E)).astype(BF16)
        s = _dot_nt(qh, kmem_ref[0, 0, :, a * LANE:(a + 1) * LANE])
        if shift is None:
            e = jnp.exp2(s - jnp.max(s, axis=-1, keepdims=True))
            p = (e / jnp.sum(e, axis=-1, keepdims=True)).astype(BF16)
        else:
            e = jnp.exp2(s - shift)
            inv_l.append(1.0 / jnp.sum(e[:, :LANE] + e[:, LANE:], axis=-1, keepdims=True))
            p = e.astype(BF16)
        oh = _dot(p, vmem_ref[0, 0, h])
        o = oh if o is None else o + oh
    if shift is not None:
        lane = lax.broadcasted_iota(jnp.int32, o.shape, 1)
        scale = inv_l[MEM_HEADS - 1]
        for h in range(MEM_HEADS - 2, -1, -1):
            scale = jnp.where(lane < (h + 1) * MEM_HDIM, inv_l[h], scale)
        o = o * scale
    return o


def _mem_shift(mm2_ref, layer, bounded):
    return mm2_ref[layer] if bounded else None


def _pre_a_kernel(mm2_ref, x_ref, ct_ref, st_ref, ctt_ref, stt_ref, ln_ref, wlat_ref, wqm_ref, wgate_ref,
                  qlg_ref, kvlg_ref, wuqt_ref, wk_ref, wvt_ref, qg_ref, kg_ref, mqg_ref, kmem_ref, vmem_ref,
                  qt_out, k_out, vt_out, sg_out, om_out, *, layer, bounded):
    x = x_ref[0]
    h = (x * _rsqrt_mean(x, D_MODEL) * ln_ref[0]).astype(BF16)

    lat = _dot(h, wlat_ref[0])
    q_lat = lat[:, :Q_LORA]
    kv_lat = lat[:, Q_LORA:Q_LORA + KV_LORA]
    kpe = lat[:, Q_LORA + KV_LORA:]

    ct = ct_ref[...]
    st = st_ref[...]

    rows = x.shape[0]
    qn = (q_lat * _rsqrt_mean(q_lat, Q_LORA) * qlg_ref[0]).astype(BF16)
    qt_raw = _dot_nt(wuqt_ref[0], qn)
    q_c = ctt_ref[...] * jnp.tile(qg_ref[0, 0], (1, rows // LANE))
    q_s = stt_ref[...] * jnp.tile(qg_ref[0, 1], (1, rows // LANE))
    scale = QK_HEAD ** -0.5 * LOG2E
    for hd in range(MLA_HEADS):
        qh = qt_raw[hd * LANE:(hd + 1) * LANE, :]
        r = lax.rsqrt(jnp.sum(qh * qh, axis=0, keepdims=True) * (1.0 / QK_HEAD) + EPS)
        partner = jnp.concatenate([qh[HALF_LANE:], qh[:HALF_LANE]], axis=0)
        qt_out[0, hd] = ((qh * q_c + partner * q_s) * (r * scale)).astype(BF16)

    kvn = (kv_lat * _rsqrt_mean(kv_lat, KV_LORA) * kvlg_ref[0]).astype(BF16)
    k_nope = _dot(kvn, wk_ref[0])
    vt_out[0, 0] = _dot_nt(wvt_ref[0], kvn).astype(BF16)
    k_cg = kg_ref[0, 0:1]
    kpe_rot = kpe * (ct * k_cg) + pltpu.roll(kpe, HALF_LANE, 1) * (st * kg_ref[0, 1:2])
    pe_sq = jnp.sum(kpe * kpe, axis=-1, keepdims=True)
    for hd in range(MLA_HEADS):
        kh = k_nope[:, hd * LANE:(hd + 1) * LANE]
        r = lax.rsqrt((jnp.sum(kh * kh, axis=-1, keepdims=True) + pe_sq) * (1.0 / QK_HEAD) + EPS)
        k_out[0, hd] = ((kh * k_cg + kpe_rot) * r).astype(BF16)

    sg = _silu(_dot(h, wgate_ref[0]))
    sg_out[0] = sg[:, :MLA_WIDTH]
    o_mem = _memory_attention(_dot(h, wqm_ref[0]), mqg_ref[0], kmem_ref, vmem_ref,
                              _mem_shift(mm2_ref, layer, bounded))
    om_out[0] = (o_mem * sg[:, MLA_WIDTH:]).astype(BF16)


def _const_spec(shape, layer):
    nd = len(shape)
    return pl.BlockSpec((1,) + tuple(shape[1:]), lambda b, i: (layer,) + (0,) * (nd - 1))


def _pre_a(x, rope, w, j, layer, kmem, vmem, bounded):
    B, S, _ = x.shape
    tm = ROW_TILE
    kv_per_chunk = ATTN_TK // tm
    weights = [w["ln_g"], w["a_wlat"], w["a_wqm"], w["a_wgate"], w["a_qlat_g"], w["a_kvlat_g"],
               w["a_wuqt"], w["a_wk"], w["a_wvt"], w["a_qg"], w["a_kg"], w["mq_g"]]
    wlayer = [layer, j, j, j, j, j, j, j, j, j, j, layer]
    in_specs = [
        pl.BlockSpec(memory_space=pltpu.SMEM),
        pl.BlockSpec((1, tm, D_MODEL), lambda b, i: (b, i, 0)),
        pl.BlockSpec((tm, LANE), lambda b, i: (i, 0)),
        pl.BlockSpec((tm, LANE), lambda b, i: (i, 0)),
        pl.BlockSpec((LANE, tm), lambda b, i: (0, i)),
        pl.BlockSpec((LANE, tm), lambda b, i: (0, i)),
    ] + [_const_spec(a.shape, l) for a, l in zip(weights, wlayer)] + _mem_specs(layer)
    out_specs = [
        pl.BlockSpec((1, MLA_HEADS, LANE, tm), lambda b, i: (b, 0, 0, i)),
        pl.BlockSpec((1, MLA_HEADS, tm, LANE), lambda b, i: (b, 0, i, 0)),
        pl.BlockSpec((1, 1, MLA_WIDTH, tm), lambda b, i: (b, i // kv_per_chunk, 0, i % kv_per_chunk)),
        pl.BlockSpec((1, tm, MLA_WIDTH), lambda b, i: (b, i, 0)),
        pl.BlockSpec((1, tm, MEM_WIDTH), lambda b, i: (b, i, 0)),
    ]
    out_shape = [
        jax.ShapeDtypeStruct((B, MLA_HEADS, LANE, S), BF16),
        jax.ShapeDtypeStruct((B, MLA_HEADS, S, LANE), BF16),
        jax.ShapeDtypeStruct((B, S // ATTN_TK, MLA_WIDTH, ATTN_TK), BF16),
        jax.ShapeDtypeStruct((B, S, MLA_WIDTH), F32),
        jax.ShapeDtypeStruct((B, S, MEM_WIDTH), BF16),
    ]
    return pl.pallas_call(
        functools.partial(_pre_a_kernel, layer=layer, bounded=bounded),
        grid=(B, S // tm),
        in_specs=in_specs,
        out_specs=out_specs,
        out_shape=out_shape,
        compiler_params=pltpu.CompilerParams(
            dimension_semantics=("arbitrary", "arbitrary"), vmem_limit_bytes=VMEM_LIMIT),
        name="pre_a",
    )(w["mem_m2"], x, *rope, *weights, kmem, vmem)


def _attn_kernel(m2_ref, qt_ref, k_ref, vt_ref, sg_ref, o_ref, acc_ref, l_ref, m_ref, *, seq, bounded):
    tq, tk = o_ref.shape[1], ATTN_TK
    acc_ref[...] = jnp.zeros_like(acc_ref)
    l_ref[...] = jnp.zeros_like(l_ref)
    if not bounded:
        m_ref[...] = jnp.full_like(m_ref, -jnp.inf)

    def body(j, carry):
        for hh in range(2):
            k = k_ref[0, hh, pl.ds(pl.multiple_of(j * tk, tk), tk), :]
            st = _dot(k, qt_ref[0, hh])
            vt = vt_ref[0, j, hh * V_HEAD:(hh + 1) * V_HEAD, :]
            if bounded:
                pt = jnp.exp2(st - m2_ref[0])
                l_ref[hh] += jnp.sum(pt.reshape(tk // 8, 8, tq), axis=0)
                acc_ref[hh] += _dot(vt, pt.astype(BF16))
            else:
                m_old = m_ref[hh]
                m_new = jnp.maximum(m_old, jnp.max(st, axis=0, keepdims=True))
                alpha = jnp.exp2(m_old - m_new)
                pt = jnp.exp2(st - m_new)
                l_ref[hh] = alpha * l_ref[hh] + jnp.sum(pt.reshape(tk // 8, 8, tq), axis=0)
                acc_ref[hh] = alpha * acc_ref[hh] + _dot(vt, pt.astype(BF16))
                m_ref[hh] = m_new
        return carry

    lax.fori_loop(0, seq // tk, body, 0, unroll=2)
    ot = [acc_ref[hh] / jnp.sum(l_ref[hh], axis=0, keepdims=True) for hh in range(2)]
    o = jnp.concatenate(ot, axis=0).T
    o_ref[0] = (o * sg_ref[0]).astype(BF16)


def _attention(qt, k, vt, sg, m2, bounded):
    B, _, S, _ = k.shape
    tq = ATTN_TQ if bounded else SAFE_TQ
    return pl.pallas_call(
        functools.partial(_attn_kernel, seq=S, bounded=bounded),
        grid=(B, MLA_HEADS // 2, S // tq),
        in_specs=[
            pl.BlockSpec(memory_space=pltpu.SMEM),
            pl.BlockSpec((1, 2, LANE, tq), lambda b, hp, i: (b, hp, 0, i)),
            pl.BlockSpec((1, 2, S, LANE), lambda b, hp, i: (b, hp, 0, 0)),
            pl.BlockSpec((1, S // ATTN_TK, LANE, ATTN_TK), lambda b, hp, i: (b, 0, hp, 0)),
            pl.BlockSpec((1, tq, LANE), lambda b, hp, i: (b, i, hp)),
        ],
        out_specs=pl.BlockSpec((1, tq, LANE), lambda b, hp, i: (b, i, hp)),
        out_shape=jax.ShapeDtypeStruct((B, S, MLA_WIDTH), BF16),
        scratch_shapes=[pltpu.VMEM((2, V_HEAD, tq), F32), pltpu.VMEM((2, 8, tq), F32),
                        pltpu.VMEM((2, 1, tq), F32)],
        compiler_params=pltpu.CompilerParams(
            dimension_semantics=("arbitrary", "arbitrary", "arbitrary"), vmem_limit_bytes=VMEM_LIMIT),
        name="attn",
    )(m2, qt, k, vt, sg)


def _out_proj(x, o_mix, o_mem, wout_ref):
    return x + _dot(o_mix, wout_ref[0, :MLA_WIDTH, :]) + _dot(o_mem, wout_ref[0, MLA_WIDTH:, :])


def _spatial_mix(vn, ws_ref, bias):
    rows = vn.shape[0]
    n_blk = SG_WIDTH // LANE
    masked = []
    for g in range(SG_GROUPS):
        lo, hi = g * SG_GDIM, (g + 1) * SG_GDIM
        b0, b1 = lo // LANE, (hi - 1) // LANE
        blk = vn[:, b0 * LANE:(b1 + 1) * LANE]
        col = lax.broadcasted_iota(jnp.int32, blk.shape, 1) + b0 * LANE
        masked.append((b0, b1, jnp.where(col >= lo, jnp.where(col < hi, blk, 0.0), 0.0).astype(BF16)))
    chunks = []
    for c in range(rows // CHUNK):
        parts = [[] for _ in range(n_blk)]
        for g, (b0, b1, mv) in enumerate(masked):
            res = _dot(ws_ref[0, g], mv[c * CHUNK:(c + 1) * CHUNK])
            for b in range(b0, b1 + 1):
                parts[b].append(res[:, (b - b0) * LANE:(b - b0 + 1) * LANE])
        chunks.append(jnp.concatenate([functools.reduce(lambda a, b: a + b, p) for p in parts], axis=1) + bias)
    return jnp.concatenate(chunks, axis=0)


def _post_b_kernel(mm2_ref, x_ref, og_ref, omem_ref, wout_a_ref, ln_ref, win_ref, vg_ref, ws_ref, bias_ref,
                   mqg_ref, kmem_ref, vmem_ref, wout_ref, y_ref, *, layer, bounded):
    y_ref[0] = _out_proj(x_ref[0], og_ref[0], omem_ref[0], wout_a_ref)
    x = y_ref[0]
    h = (x * _rsqrt_mean(x, D_MODEL) * ln_ref[0]).astype(BF16)
    u = _gelu_tanh(_dot(h, win_ref[0, :, :SG_WIDTH]))
    v = _gelu_tanh(_dot(h, win_ref[0, :, SG_WIDTH:2 * SG_WIDTH]))
    vn = v * _rsqrt_mean(v, SG_WIDTH) * vg_ref[0]
    sg = _silu(_dot(h, win_ref[0, :, 2 * SG_WIDTH + MEM_WIDTH:]))
    o_mix = (u * _spatial_mix(vn, ws_ref, bias_ref[0]) * sg[:, :SG_WIDTH]).astype(BF16)
    qm = _dot(h, win_ref[0, :, 2 * SG_WIDTH:2 * SG_WIDTH + MEM_WIDTH])
    o_mem = _memory_attention(qm, mqg_ref[0], kmem_ref, vmem_ref, _mem_shift(mm2_ref, layer, bounded))
    o_mem = (o_mem * sg[:, SG_WIDTH:]).astype(BF16)
    y_ref[0] = _out_proj(x, o_mix, o_mem, wout_ref)


def _post_b(x, o_gated, o_mem, w, j, layer, kmem, vmem, bounded):
    B, S, _ = x.shape
    tm = ROW_TILE
    weights = [w["wout_a"], w["ln_g"], w["b_win"], w["b_vg"], w["b_ws"], w["b_bias"], w["mq_g"]]
    wlayer = [j, layer, j, j, j, j, layer]
    in_specs = [
        pl.BlockSpec(memory_space=pltpu.SMEM),
        pl.BlockSpec((1, tm, D_MODEL), lambda b, i: (b, i, 0)),
        pl.BlockSpec((1, tm, MLA_WIDTH), lambda b, i: (b, i, 0)),
        pl.BlockSpec((1, tm, MEM_WIDTH), lambda b, i: (b, i, 0)),
    ]
    in_specs += [_const_spec(a.shape, l) for a, l in zip(weights, wlayer)]
    in_specs += _mem_specs(layer) + [_const_spec(w["wout_b"].shape, j)]
    return pl.pallas_call(
        functools.partial(_post_b_kernel, layer=layer, bounded=bounded),
        grid=(B, S // tm),
        in_specs=in_specs,
        out_specs=pl.BlockSpec((1, tm, D_MODEL), lambda b, i: (b, i, 0)),
        out_shape=jax.ShapeDtypeStruct((B, S, D_MODEL), F32),
        compiler_params=pltpu.CompilerParams(
            dimension_semantics=("arbitrary", "arbitrary"), vmem_limit_bytes=VMEM_LIMIT),
        name="post_b",
    )(w["mem_m2"], x, o_gated, o_mem, *weights, kmem, vmem, w["wout_b"])


def _head_lanes(nope, rope, axis=-1):
    ref = nope if nope is not None else rope
    axis = axis % ref.ndim

    def z(n):
        return jnp.zeros(ref.shape[:axis] + (n,) + ref.shape[axis + 1:], ref.dtype)

    def cut(a, lo, hi):
        return lax.slice_in_dim(a, lo, hi, axis=axis)

    n0, n1 = (cut(nope, 0, 48), cut(nope, 48, QK_NOPE)) if nope is not None else (z(48), z(QK_NOPE - 48))
    r0, r1 = ((cut(rope, 0, HALF_ROPE), cut(rope, HALF_ROPE, QK_ROPE)) if rope is not None
              else (z(HALF_ROPE), z(HALF_ROPE)))
    return jnp.concatenate([r0, n0, r1, n1, z(LANE - QK_HEAD)], axis=axis)


def _prep_weights(ln_g, w_in_a, a_qlat_g, a_kvlat_g, a_w_uq, a_w_ukv, a_q_norm_g, a_k_norm_g, w_in_b,
                  b_v_norm_g, b_w_s, b_bias, mem_norm_g, w_mem_kv, mem_q_norm_g, mem_k_norm_g, w_out):
    w = {}
    na = w_in_a.shape[0]
    w["ln_g"] = ln_g[:, None, :]
    w["mem_g"] = mem_norm_g[:, None, :]

    o2, o3, o4 = Q_LORA + KV_LORA, Q_LORA + KV_LORA + QK_ROPE, Q_LORA + KV_LORA + QK_ROPE + MEM_WIDTH
    w["a_wlat"] = jnp.concatenate([w_in_a[:, :, :o2], _head_lanes(None, w_in_a[:, :, o2:o3])], axis=-1).astype(BF16)
    w["a_wqm"] = w_in_a[:, :, o3:o4].astype(BF16)
    w["a_wgate"] = w_in_a[:, :, o4:].astype(BF16)
    w["a_qlat_g"] = a_qlat_g[:, None, :]
    w["a_kvlat_g"] = a_kvlat_g[:, None, :]
    uq = a_w_uq.reshape(na, Q_LORA, MLA_HEADS, QK_HEAD)
    wuq = _head_lanes(uq[..., :QK_NOPE], uq[..., QK_NOPE:]).reshape(na, Q_LORA, MLA_HEADS * LANE)
    w["a_wuqt"] = jnp.swapaxes(wuq, 1, 2).astype(BF16)
    ukv =a_w_ukv.reshape(na, KV_LORA, MLA_HEADS, QK_NOPE + V_HEAD)
    w["a_wk"] = _head_lanes(ukv[..., :QK_NOPE], None).reshape(na, KV_LORA, MLA_HEADS * LANE).astype(BF16)
    w["a_wvt"] = jnp.swapaxes(ukv[..., QK_NOPE:].reshape(na, KV_LORA, MLA_WIDTH), 1, 2).astype(BF16)

    def norm_gain_rows(g):
        g_c = _head_lanes(g[:, :QK_NOPE], g[:, QK_NOPE:])
        return jnp.stack([g_c, jnp.roll(g_c, HALF_LANE, axis=-1)], axis=1)

    w["a_qg"] = jnp.broadcast_to(norm_gain_rows(a_q_norm_g)[..., None], (na, 2, LANE, LANE))
    w["a_kg"] = norm_gain_rows(a_k_norm_g)
    g_max = jnp.max(jnp.abs(a_q_norm_g), axis=1) * jnp.max(jnp.abs(a_k_norm_g), axis=1)
    w["a_m2"] = (QK_HEAD ** 0.5 * LOG2E * g_max)[:, None]

    w["b_win"] = w_in_b.astype(BF16)
    w["b_vg"] = b_v_norm_g[:, None, :]
    w["b_ws"] = b_w_s.astype(BF16)
    w["b_bias"] = jnp.repeat(jnp.swapaxes(b_bias, 1, 2), SG_GDIM, axis=2)

    w["mem_wk"] = w_mem_kv[:, :, :MEM_WIDTH].astype(BF16)
    w["mem_wv"] = w_mem_kv[:, :, MEM_WIDTH:].astype(BF16)
    w["mem_m2"] = (MEM_HDIM ** 0.5 * LOG2E * jnp.max(jnp.abs(mem_q_norm_g), axis=1)
                   * jnp.max(jnp.abs(mem_k_norm_g), axis=1))
    w["mq_g"] = jnp.tile(mem_q_norm_g, (1, MEM_HEADS))[:, None, :]
    w["mk_g"] = jnp.tile(mem_k_norm_g, (1, MEM_HEADS))[:, None, :]
    w["wout_a"] = w_out[0::2].astype(BF16)
    w["wout_b"] = w_out[1::2].astype(BF16)
    return w


def _rope_tables(seq):
    inv = 1.0 / (ROPE_BASE ** (jnp.arange(0, QK_ROPE, 2, dtype=F32) / QK_ROPE))
    ang = inv[:, None] * jnp.arange(seq, dtype=F32)[None, :]
    cos, sin = jnp.cos(ang), jnp.sin(ang)
    ctt = _head_lanes(jnp.ones((QK_NOPE, seq), F32), jnp.concatenate([cos, cos], axis=0), axis=0)
    stt = _head_lanes(None, jnp.concatenate([-sin, sin], axis=0), axis=0)
    return ctt.T, stt.T, ctt, stt


def _trunk(x, mem, rope, w, bounded):
    kmem, vmem = _mem_kv(mem, w["mem_g"], w["mem_wk"], w["mem_wv"], w["mk_g"])
    for j in range(DEPTH // 2):
        qt, k, vt, sg, o_mem = _pre_a(x, rope, w, j, 2 * j, kmem, vmem, bounded)
        o_gated = _attention(qt, k, vt, sg, w["a_m2"][j], bounded)
        x = _post_b(x, o_gated, o_mem, w, j, 2 * j + 1, kmem, vmem, bounded)
    return x


def kernel(x_prompt, x_sample, mem_prompt, mem_sample, ln_g, w_in_a, a_qlat_g, a_kvlat_g, a_w_uq, a_w_ukv,
           a_q_norm_g, a_k_norm_g, w_in_b, b_v_norm_g, b_w_s, b_bias, mem_norm_g, w_mem_kv, mem_q_norm_g,
           mem_k_norm_g, w_out):
    w = _prep_weights(ln_g, w_in_a, a_qlat_g, a_kvlat_g, a_w_uq, a_w_ukv, a_q_norm_g, a_k_norm_g, w_in_b,
                      b_v_norm_g, b_w_s, b_bias, mem_norm_g, w_mem_kv, mem_q_norm_g, mem_k_norm_g, w_out)

    rope = _rope_tables(max(x_prompt.shape[1], x_sample.shape[1]))

    def run(bounded):
        return lambda ops: (_trunk(ops[0], ops[1], rope, w, bounded), _trunk(ops[2], ops[3], rope, w, bounded))

    shifts_ok = jnp.maximum(jnp.max(w["a_m2"]), jnp.max(w["mem_m2"])) <= MAX_FIXED_SHIFT
    return lax.cond(shifts_ok, run(True), run(False), (x_prompt, mem_prompt, x_sample, mem_sample))
```

```python
import functools
import math

import jax
import jax.numpy as jnp
from jax import lax
from jax.experimental import pallas as pl
from jax.experimental.pallas import tpu as pltpu

D_MODEL = 1024
DEPTH = 4
EPS = 1e-6
N_MEM = 256
MEM_HEADS = 4
MEM_HDIM = 64
MEM_WIDTH = MEM_HEADS * MEM_HDIM
MLA_HEADS = 12
QK_NOPE = 64
QK_ROPE = 32
QK_HEAD = QK_NOPE + QK_ROPE
V_HEAD = 64
Q_LORA = 384
KV_LORA = 256
MLA_WIDTH = MLA_HEADS * V_HEAD
ROPE_BASE = 10000.0
CHUNK = 128
SG_GROUPS = 8
SG_WIDTH = 768
SG_GDIM = SG_WIDTH // SG_GROUPS
BRANCH = MLA_WIDTH + MEM_WIDTH

LANE = 128
HALF_LANE = LANE // 2
HALF_ROPE = QK_ROPE // 2

ROW_TILE = 1024
ATTN_TQ = 2048
ATTN_TK = 2048
SAFE_TQ = 512
VMEM_LIMIT = 56 * 1024 * 1024
LOG2E = 1.4426950408889634
MAX_FIXED_SHIFT = 40.0

BF16 = jnp.bfloat16
F32 = jnp.float32


def _rsqrt_mean(x, n):
    return lax.rsqrt(jnp.sum(x * x, axis=-1, keepdims=True) * (1.0 / n) + EPS)


def _silu(g):
    return g / (1.0 + jnp.exp(-g))


def _gelu_tanh(x):
    c = math.sqrt(2.0 / math.pi)
    return 0.5 * x * (1.0 + jnp.tanh(c * (x + 0.044715 * (x * x * x))))


def _dot(a, b):
    return jnp.dot(a, b, preferred_element_type=F32)


def _dot_nt(a, b):
    return lax.dot_general(a, b, (((1,), (1,)), ((), ())), preferred_element_type=F32)


def _low_half(shape):
    return lax.broadcasted_iota(jnp.int32, shape, 1) < HALF_LANE


def _norm_heads64(x, g):
    out = []
    for a in range(MEM_WIDTH // LANE):
        blk = x[:, a * LANE:(a + 1) * LANE]
        lo = _low_half(blk.shape)
        sq = blk * blk
        s_lo = jnp.sum(jnp.where(lo, sq, 0.0), axis=-1, keepdims=True)
        s_hi = jnp.sum(jnp.where(lo, 0.0, sq), axis=-1, keepdims=True)
        r = jnp.where(lo, lax.rsqrt(s_lo * (1.0 / MEM_HDIM) + EPS), lax.rsqrt(s_hi * (1.0 / MEM_HDIM) + EPS))
        out.append(blk * r * g[:, a * LANE:(a + 1) * LANE])
    return out


def _mem_kv_kernel(mem_ref, g_ref, wk_ref, wv_ref, kg_ref, k_out, v_out):
    m = mem_ref[0]
    m_hat = m * _rsqrt_mean(m, D_MODEL)
    lane = lax.broadcasted_iota(jnp.int32, (N_MEM, MEM_WIDTH), 1)
    for layer in range(DEPTH):
        mn = (m_hat * g_ref[layer]).astype(BF16)
        kn = _norm_heads64(_dot(mn, wk_ref[layer]), kg_ref[layer])
        k_out[layer, 0] = jnp.concatenate(kn, axis=1).astype(BF16)
        vv = _dot(mn, wv_ref[layer])
        for h in range(MEM_HEADS):
            vh = jnp.where(lane >= h * MEM_HDIM, jnp.where(lane < (h + 1) * MEM_HDIM, vv, 0.0), 0.0)
            v_out[layer, 0, h] = vh.astype(BF16)


def _mem_kv(mem, mem_g, wk, wv, kg):
    B = mem.shape[0]
    whole = lambda a: pl.BlockSpec(a.shape, lambda b: (0,) * a.ndim)
    return pl.pallas_call(
        _mem_kv_kernel,
        grid=(B,),
        in_specs=[pl.BlockSpec((1, N_MEM, D_MODEL), lambda b: (b, 0, 0)),
                  whole(mem_g), whole(wk), whole(wv), whole(kg)],
        out_specs=[
            pl.BlockSpec((DEPTH, 1, N_MEM, MEM_WIDTH), lambda b: (0, b, 0, 0)),
            pl.BlockSpec((DEPTH, 1, MEM_HEADS, N_MEM, MEM_WIDTH), lambda b: (0, b, 0, 0, 0)),
        ],
        out_shape=[
            jax.ShapeDtypeStruct((DEPTH, B, N_MEM, MEM_WIDTH), BF16),
            jax.ShapeDtypeStruct((DEPTH, B, MEM_HEADS, N_MEM, MEM_WIDTH), BF16),
        ],
        compiler_params=pltpu.CompilerParams(
            dimension_semantics=("arbitrary",), vmem_limit_bytes=VMEM_LIMIT),
        name="mem_kv",
    )(mem, mem_g, wk, wv, kg)


def _mem_specs(layer):
    return [
        pl.BlockSpec((1, 1, N_MEM, MEM_WIDTH), lambda b, i: (layer, b, 0, 0)),
        pl.BlockSpec((1, 1, MEM_HEADS, N_MEM, MEM_WIDTH), lambda b, i: (layer, b, 0, 0, 0)),
    ]


def _memory_attention(qm, qg, kmem_ref, vmem_ref, shift):
    qn = _norm_heads64(qm, qg)
    o = None
    inv_l = []
    for h in range(MEM_HEADS):
        a, upper = divmod(h, 2)
        lo = _low_half(qn[a].shape)
        qh = jnp.where(lo, 0.0, qn[a]) if upper else jnp.where(lo, qn[a], 0.0)
        qh = (qh * (MEM_HDIM ** -0.5 * LOG2E)).astype(BF16)
        s = _dot_nt(qh, kmem_ref[0, 0, :, a * LANE:(a + 1) * LANE])
        if shift is None:
            e = jnp.exp2(s - jnp.max(s, axis=-1, keepdims=True))
            p = (e / jnp.sum(e, axis=-1, keepdims=True)).astype(BF16)
        else:
            p = jnp.exp2(s - shift).astype(BF16)
            inv_l.append(1.0 / _dot(p, jnp.ones((N_MEM, LANE), BF16)))
        oh = _dot(p, vmem_ref[0, 0, h])
        o = oh if o is None else o + oh
    if shift is not None:
        lo = _low_half(inv_l[0].shape)
        blocks = [jnp.where(lo, inv_l[2 * a], inv_l[2 * a + 1]) for a in range(MEM_WIDTH // LANE)]
        o = o * jnp.concatenate(blocks, axis=1)
    return o


def _mem_shift(mm2_ref, layer, bounded):
    return mm2_ref[layer] if bounded else None


def _pre_a_kernel(mm2_ref, x_ref, ct_ref, st_ref, ctt_ref, stt_ref, ln_ref, wlat_ref, wqm_ref, wgate_ref,
                  qlg_ref, kvlg_ref, wuqt_ref, wk_ref, wvt_ref, qg_ref, kg_ref, mqg_ref, kmem_ref, vmem_ref,
                  qt_out, k_out, vt_out, sg_out, om_out, *, layer, bounded):
    x = x_ref[0]
    h = (x * _rsqrt_mean(x, D_MODEL) * ln_ref[0]).astype(BF16)

    lat = _dot(h, wlat_ref[0])
    q_lat = lat[:, :Q_LORA]
    kv_lat = lat[:, Q_LORA:Q_LORA + KV_LORA]
    kpe = lat[:, Q_LORA + KV_LORA:]

    ct = ct_ref[...]
    st = st_ref[...]

    rows = x.shape[0]
    qn = (q_lat * _rsqrt_mean(q_lat, Q_LORA) * qlg_ref[0]).astype(BF16)
    qt_raw = _dot_nt(wuqt_ref[0], qn)
    q_c = ctt_ref[...] * jnp.tile(qg_ref[0, 0], (1, rows // LANE))
    q_s = stt_ref[...] * jnp.tile(qg_ref[0, 1], (1, rows // LANE))
    scale = QK_HEAD ** -0.5 * LOG2E
    for hd in range(MLA_HEADS):
        qh = qt_raw[hd * LANE:(hd + 1) * LANE, :]
        r = lax.rsqrt(jnp.sum(qh * qh, axis=0, keepdims=True) * (1.0 / QK_HEAD) + EPS)
        partner = jnp.concatenate([qh[HALF_LANE:], qh[:HALF_LANE]], axis=0)
        qt_out[0, hd] = ((qh * q_c + partner * q_s) * (r * scale)).astype(BF16)

    kvn = (kv_lat * _rsqrt_mean(kv_lat, KV_LORA) * kvlg_ref[0]).astype(BF16)
    k_nope = _dot(kvn, wk_ref[0])
    vt_out[0, 0] = _dot_nt(wvt_ref[0], kvn).astype(BF16)
    k_cg = kg_ref[0, 0:1]
    kpe_rot = kpe * (ct * k_cg) + pltpu.roll(kpe, HALF_LANE, 1) * (st * kg_ref[0, 1:2])
    pe_sq = jnp.sum(kpe * kpe, axis=-1, keepdims=True)
    for hd in range(MLA_HEADS):
        kh = k_nope[:, hd * LANE:(hd + 1) * LANE]
        r = lax.rsqrt((jnp.sum(kh * kh, axis=-1, keepdims=True) + pe_sq) * (1.0 / QK_HEAD) + EPS)
        k_out[0, hd] = ((kh * k_cg + kpe_rot) * r).astype(BF16)

    sg = _silu(_dot(h, wgate_ref[0]))
    sg_out[0] = sg[:, :MLA_WIDTH]
    o_mem = _memory_attention(_dot(h, wqm_ref[0]), mqg_ref[0], kmem_ref, vmem_ref,
                              _mem_shift(mm2_ref, layer, bounded))
    om_out[0] = (o_mem * sg[:, MLA_WIDTH:]).astype(BF16)


def _const_spec(shape, layer):
    nd = len(shape)
    return pl.BlockSpec((1,) + tuple(shape[1:]), lambda b, i: (layer,) + (0,) * (nd - 1))


def _pre_a(x, rope, w, j, layer, kmem, vmem, bounded):
    B, S, _ = x.shape
    tm = ROW_TILE
    kv_per_chunk = ATTN_TK // tm
    weights = [w["ln_g"], w["a_wlat"], w["a_wqm"], w["a_wgate"], w["a_qlat_g"], w["a_kvlat_g"],
               w["a_wuqt"], w["a_wk"], w["a_wvt"], w["a_qg"], w["a_kg"], w["mq_g"]]
    wlayer = [layer, j, j, j, j, j, j, j, j, j, j, layer]
    in_specs = [
        pl.BlockSpec(memory_space=pltpu.SMEM),
        pl.BlockSpec((1, tm, D_MODEL), lambda b, i: (b, i, 0)),
        pl.BlockSpec((tm, LANE), lambda b, i: (i, 0)),
        pl.BlockSpec((tm, LANE), lambda b, i: (i, 0)),
        pl.BlockSpec((LANE, tm), lambda b, i: (0, i)),
        pl.BlockSpec((LANE, tm), lambda b, i: (0, i)),
    ] + [_const_spec(a.shape, l) for a, l in zip(weights, wlayer)] + _mem_specs(layer)
    out_specs = [
        pl.BlockSpec((1, MLA_HEADS, LANE, tm), lambda b, i: (b, 0, 0, i)),
        pl.BlockSpec((1, MLA_HEADS, tm, LANE), lambda b, i: (b, 0, i, 0)),
        pl.BlockSpec((1, 1, MLA_WIDTH, tm), lambda b, i: (b, i // kv_per_chunk, 0, i % kv_per_chunk)),
        pl.BlockSpec((1, tm, MLA_WIDTH), lambda b, i: (b, i, 0)),
        pl.BlockSpec((1, tm, MEM_WIDTH), lambda b, i: (b, i, 0)),
    ]
    out_shape = [
        jax.ShapeDtypeStruct((B, MLA_HEADS, LANE, S), BF16),
        jax.ShapeDtypeStruct((B, MLA_HEADS, S, LANE), BF16),
        jax.ShapeDtypeStruct((B, S // ATTN_TK, MLA_WIDTH, ATTN_TK), BF16),
        jax.ShapeDtypeStruct((B, S, MLA_WIDTH), F32),
        jax.ShapeDtypeStruct((B, S, MEM_WIDTH), BF16),
    ]
    return pl.pallas_call(
        functools.partial(_pre_a_kernel, layer=layer, bounded=bounded),
        grid=(B, S // tm),
        in_specs=in_specs,
        out_specs=out_specs,
        out_shape=out_shape,
        compiler_params=pltpu.CompilerParams(
            dimension_semantics=("arbitrary", "arbitrary"), vmem_limit_bytes=VMEM_LIMIT),
        name="pre_a",
    )(w["mem_m2"], x, *rope, *weights, kmem, vmem)


def _attn_kernel(m2_ref, qt_ref, k_ref, vt_ref, sg_ref, o_ref, acc_ref, l_ref, m_ref, *, seq, bounded):
    tq, tk = o_ref.shape[1], ATTN_TK
    acc_ref[...] = jnp.zeros_like(acc_ref)
    l_ref[...] = jnp.zeros_like(l_ref)
    if not bounded:
        m_ref[...] = jnp.full_like(m_ref, -jnp.inf)

    def body(j, carry):
        for hh in range(2):
            k = k_ref[0, hh, pl.ds(pl.multiple_of(j * tk, tk), tk), :]
            st = _dot(k, qt_ref[0, hh])
            vt = vt_ref[0, j, hh * V_HEAD:(hh + 1) * V_HEAD, :]
            if bounded:
                pt = jnp.exp2(st - m2_ref[0])
                l_ref[hh] += jnp.sum(pt.reshape(tk // 8, 8, tq), axis=0)
                acc_ref[hh] += _dot(vt, pt.astype(BF16))
            else:
                m_old = m_ref[hh]
                m_new = jnp.maximum(m_old, jnp.max(st, axis=0, keepdims=True))
                alpha = jnp.exp2(m_old - m_new)
                pt = jnp.exp2(st - m_new)
                l_ref[hh] = alpha * l_ref[hh] + jnp.sum(pt.reshape(tk // 8, 8, tq), axis=0)
                acc_ref[hh] = alpha * acc_ref[hh] + _dot(vt, pt.astype(BF16))
                m_ref[hh] = m_new
        return carry

    lax.fori_loop(0, seq // tk, body, 0, unroll=2)
    ot = [acc_ref[hh] / jnp.sum(l_ref[hh], axis=0, keepdims=True) for hh in range(2)]
    o = jnp.concatenate(ot, axis=0).T
    o_ref[0] = (o * sg_ref[0]).astype(BF16)


def _attention(qt, k, vt, sg, m2, bounded):
    B, _, S, _ = k.shape
    tq = ATTN_TQ if bounded else SAFE_TQ
    return pl.pallas_call(
        functools.partial(_attn_kernel, seq=S, bounded=bounded),
        grid=(B, MLA_HEADS // 2, S // tq),
        in_specs=[
            pl.BlockSpec(memory_space=pltpu.SMEM),
            pl.BlockSpec((1, 2, LANE, tq), lambda b, hp, i: (b, hp, 0, i)),
            pl.BlockSpec((1, 2, S, LANE), lambda b, hp, i: (b, hp, 0, 0)),
            pl.BlockSpec((1, S // ATTN_TK, LANE, ATTN_TK), lambda b, hp, i: (b, 0, hp, 0)),
            pl.BlockSpec((1, tq, LANE), lambda b, hp, i: (b, i, hp)),
        ],
        out_specs=pl.BlockSpec((1, tq, LANE), lambda b, hp, i: (b, i, hp)),
        out_shape=jax.ShapeDtypeStruct((B, S, MLA_WIDTH), BF16),
        scratch_shapes=[pltpu.VMEM((2, V_HEAD, tq), F32), pltpu.VMEM((2, 8, tq), F32),
                        pltpu.VMEM((2, 1, tq), F32)],
        compiler_params=pltpu.CompilerParams(
            dimension_semantics=("arbitrary", "arbitrary", "arbitrary"), vmem_limit_bytes=VMEM_LIMIT),
        name="attn",
    )(m2, qt, k, vt, sg)


def _out_proj(x, o_mix, o_mem, wout_ref):
    return x + _dot(o_mix, wout_ref[0, :MLA_WIDTH, :]) + _dot(o_mem, wout_ref[0, MLA_WIDTH:, :])


def _spatial_mix(vn, ws_ref, bias):
    rows = vn.shape[0]
    n_blk = SG_WIDTH // LANE
    masked = []
    for g in range(SG_GROUPS):
        lo, hi = g * SG_GDIM, (g + 1) * SG_GDIM
        b0, b1 = lo // LANE, (hi - 1) // LANE
        blk = vn[:, b0 * LANE:(b1 + 1) * LANE]
        col = lax.broadcasted_iota(jnp.int32, blk.shape, 1) + b0 * LANE
        masked.append((b0, b1, jnp.where(col >= lo, jnp.where(col < hi, blk, 0.0), 0.0).astype(BF16)))
    chunks = []
    for c in range(rows // CHUNK):
        parts = [[] for _ in range(n_blk)]
        for g, (b0, b1, mv) in enumerate(masked):
            res = _dot(ws_ref[0, g], mv[c * CHUNK:(c + 1) * CHUNK])
            for b in range(b0, b1 + 1):
                parts[b].append(res[:, (b - b0) * LANE:(b - b0 + 1) * LANE])
        chunks.append(jnp.concatenate([functools.reduce(lambda a, b: a + b, p) for p in parts], axis=1) + bias)
    return jnp.concatenate(chunks, axis=0)


def _post_b_kernel(mm2_ref, x_ref, og_ref, omem_ref, wout_a_ref, ln_ref, win_ref, vg_ref, ws_ref, bias_ref,
                   mqg_ref, kmem_ref, vmem_ref, wout_ref, y_ref, *, layer, bounded):
    y_ref[0] = _out_proj(x_ref[0], og_ref[0], omem_ref[0], wout_a_ref)
    x = y_ref[0]
    h = (x * _rsqrt_mean(x, D_MODEL) * ln_ref[0]).astype(BF16)
    u = _gelu_tanh(_dot(h, win_ref[0, :, :SG_WIDTH]))
    v = _gelu_tanh(_dot(h, win_ref[0, :, SG_WIDTH:2 * SG_WIDTH]))
    vn = v * _rsqrt_mean(v, SG_WIDTH) * vg_ref[0]
    sg = _silu(_dot(h, win_ref[0, :, 2 * SG_WIDTH + MEM_WIDTH:]))
    o_mix = (u * _spatial_mix(vn, ws_ref, bias_ref[0]) * sg[:, :SG_WIDTH]).astype(BF16)
    qm = _dot(h, win_ref[0, :, 2 * SG_WIDTH:2 * SG_WIDTH + MEM_WIDTH])
    o_mem = _memory_attention(qm, mqg_ref[0], kmem_ref, vmem_ref, _mem_shift(mm2_ref, layer, bounded))
    o_mem = (o_mem * sg[:, SG_WIDTH:]).astype(BF16)
    y_ref[0] = _out_proj(x, o_mix, o_mem, wout_ref)


def _post_b(x, o_gated, o_mem, w, j, layer, kmem, vmem, bounded):
    B, S, _ = x.shape
    tm = ROW_TILE
    weights = [w["wout_a"], w["ln_g"], w["b_win"], w["b_vg"], w["b_ws"], w["b_bias"], w["mq_g"]]
    wlayer = [j, layer, j, j, j, j, layer]
    in_specs = [
        pl.BlockSpec(memory_space=pltpu.SMEM),
        pl.BlockSpec((1, tm, D_MODEL), lambda b, i: (b, i, 0)),
        pl.BlockSpec((1, tm, MLA_WIDTH), lambda b, i: (b, i, 0)),
        pl.BlockSpec((1, tm, MEM_WIDTH), lambda b, i: (b, i, 0)),
    ]
    in_specs += [_const_spec(a.shape, l) for a, l in zip(weights, wlayer)]
    in_specs += _mem_specs(layer) + [_const_spec(w["wout_b"].shape, j)]
    return pl.pallas_call(
        functools.partial(_post_b_kernel, layer=layer, bounded=bounded),
        grid=(B, S // tm),
        in_specs=in_specs,
        out_specs=pl.BlockSpec((1, tm, D_MODEL), lambda b, i: (b, i, 0)),
        out_shape=jax.ShapeDtypeStruct((B, S, D_MODEL), F32),
        compiler_params=pltpu.CompilerParams(
            dimension_semantics=("arbitrary", "arbitrary"), vmem_limit_bytes=VMEM_LIMIT),
        name="post_b",
    )(w["mem_m2"], x, o_gated, o_mem, *weights, kmem, vmem, w["wout_b"])


def _head_lanes(nope, rope, axis=-1):
    ref = nope if nope is not None else rope
    axis = axis % ref.ndim

    def z(n):
        return jnp.zeros(ref.shape[:axis] + (n,) + ref.shape[axis + 1:], ref.dtype)

    def cut(a, lo, hi):
        return lax.slice_in_dim(a, lo, hi, axis=axis)

    n0, n1 = (cut(nope, 0, 48), cut(nope, 48, QK_NOPE)) if nope is not None else (z(48), z(QK_NOPE - 48))
    r0, r1 = ((cut(rope, 0, HALF_ROPE), cut(rope, HALF_ROPE, QK_ROPE)) if rope is not None
              else (z(HALF_ROPE), z(HALF_ROPE)))
    return jnp.concatenate([r0, n0, r1, n1, z(LANE - QK_HEAD)], axis=axis)


def _prep_weights(ln_g, w_in_a, a_qlat_g, a_kvlat_g, a_w_uq, a_w_ukv, a_q_norm_g, a_k_norm_g, w_in_b,
                  b_v_norm_g, b_w_s, b_bias, mem_norm_g, w_mem_kv, mem_q_norm_g, mem_k_norm_g, w_out):
    w = {}
    na = w_in_a.shape[0]
    w["ln_g"] = ln_g[:, None, :]
    w["mem_g"] = mem_norm_g[:, None, :]

    o2, o3, o4 = Q_LORA + KV_LORA, Q_LORA + KV_LORA + QK_ROPE, Q_LORA + KV_LORA + QK_ROPE + MEM_WIDTH
    w["a_wlat"] = jnp.concatenate([w_in_a[:, :, :o2], _head_lanes(None, w_in_a[:, :, o2:o3])], axis=-1).astype(BF16)
    w["a_wqm"] = w_in_a[:, :, o3:o4].astype(BF16)
    w["a_wgate"] = w_in_a[:, :, o4:].astype(BF16)
    w["a_qlat_g"] = a_qlat_g[:, None, :]
    w["a_kvlat_g"] = a_kvlat_g[:, None, :]
    uq = a_w_uq.reshape(na, Q_LORA, MLA_HEADS, QK_HEAD)
    wuq = _head_lanes(uq[..., :QK_NOPE], uq[..., QK_NOPE:]).reshape(na, Q_LORA, MLA_HEADS * LANE)
    w["a_wuqt"] = jnp.swapaxes(wuq, 1, 2).astype(BF16)
    ukv =a_w_ukv.reshape(na, KV_LORA, MLA_HEADS, QK_NOPE + V_HEAD)
    w["a_wk"] = _head_lanes(ukv[..., :QK_NOPE], None).reshape(na, KV_LORA, MLA_HEADS * LANE).astype(BF16)
    w["a_wvt"] = jnp.swapaxes(ukv[..., QK_NOPE:].reshape(na, KV_LORA, MLA_WIDTH), 1, 2).astype(BF16)

    def norm_gain_rows(g):
        g_c = _head_lanes(g[:, :QK_NOPE], g[:, QK_NOPE:])
        return jnp.stack([g_c, jnp.roll(g_c, HALF_LANE, axis=-1)], axis=1)

    w["a_qg"] = jnp.broadcast_to(norm_gain_rows(a_q_norm_g)[..., None], (na, 2, LANE, LANE))
    w["a_kg"] = norm_gain_rows(a_k_norm_g)
    g_max = jnp.max(jnp.abs(a_q_norm_g), axis=1) * jnp.max(jnp.abs(a_k_norm_g), axis=1)
    w["a_m2"] = (QK_HEAD ** 0.5 * LOG2E * g_max)[:, None]

    w["b_win"] = w_in_b.astype(BF16)
    w["b_vg"] = b_v_norm_g[:, None, :]
    w["b_ws"] = b_w_s.astype(BF16)
    w["b_bias"] = jnp.repeat(jnp.swapaxes(b_bias, 1, 2), SG_GDIM, axis=2)

    w["mem_wk"] = w_mem_kv[:, :, :MEM_WIDTH].astype(BF16)
    w["mem_wv"] = w_mem_kv[:, :, MEM_WIDTH:].astype(BF16)
    w["mem_m2"] = (MEM_HDIM ** 0.5 * LOG2E * jnp.max(jnp.abs(mem_q_norm_g), axis=1)
                   * jnp.max(jnp.abs(mem_k_norm_g), axis=1))
    w["mq_g"] = jnp.tile(mem_q_norm_g, (1, MEM_HEADS))[:, None, :]
    w["mk_g"] = jnp.tile(mem_k_norm_g, (1, MEM_HEADS))[:, None, :]
    w["wout_a"] = w_out[0::2].astype(BF16)
    w["wout_b"] = w_out[1::2].astype(BF16)
    return w


def _rope_tables(seq):
    inv = 1.0 / (ROPE_BASE ** (jnp.arange(0, QK_ROPE, 2, dtype=F32) / QK_ROPE))
    ang = inv[:, None] * jnp.arange(seq, dtype=F32)[None, :]
    cos, sin = jnp.cos(ang), jnp.sin(ang)
    ctt = _head_lanes(jnp.ones((QK_NOPE, seq), F32), jnp.concatenate([cos, cos], axis=0), axis=0)
    stt = _head_lanes(None, jnp.concatenate([-sin, sin], axis=0), axis=0)
    return ctt.T, stt.T, ctt, stt


def _trunk(x, mem, rope, w, bounded):
    kmem, vmem = _mem_kv(mem, w["mem_g"], w["mem_wk"], w["mem_wv"], w["mk_g"])
    for j in range(DEPTH // 2):
        qt, k, vt, sg, o_mem = _pre_a(x, rope, w, j, 2 * j, kmem, vmem, bounded)
        o_gated = _attention(qt, k, vt, sg, w["a_m2"][j], bounded)
        x = _post_b(x, o_gated, o_mem, w, j, 2 * j + 1, kmem, vmem, bounded)
    return x


def kernel(x_prompt, x_sample, mem_prompt, mem_sample, ln_g, w_in_a, a_qlat_g, a_kvlat_g, a_w_uq, a_w_ukv,
           a_q_norm_g, a_k_norm_g, w_in_b, b_v_norm_g, b_w_s, b_bias, mem_norm_g, w_mem_kv, mem_q_norm_g,
           mem_k_norm_g, w_out):
    w = _prep_weights(ln_g, w_in_a, a_qlat_g, a_kvlat_g, a_w_uq, a_w_ukv, a_q_norm_g, a_k_norm_g, w_in_b,
                      b_v_norm_g, b_w_s, b_bias, mem_norm_g, w_mem_kv, mem_q_norm_g, mem_k_norm_g, w_out)

    rope = _rope_tables(max(x_prompt.shape[1], x_sample.shape[1]))

    def run(bounded):
        return lambda ops: (_trunk(ops[0], ops[1], rope, w, bounded), _trunk(ops[2], ops[3], rope, w, bounded))

    shifts_ok = jnp.maximum(jnp.max(w["a_m2"]), jnp.max(w["mem_m2"])) <= MAX_FIXED_SHIFT
    return lax.cond(shifts_ok, run(True), run(False), (x_prompt, mem_prompt, x_sample, mem_sample))
```
